```python
import math, functools
import jax, jax.numpy as jnp
from jax import lax
import numpy as np

D_MODEL = 1024
BATCH = 8
SEQ = 8192
DEPTH = 1
DEC_BATCH = 128
DEC_SEQ = 1
PAST_LEN = 8192
PAGE_SIZE = 128

N_HEADS = 8
HEAD_DIM = D_MODEL // (2 * N_HEADS)
N_KV = 2
GQA = N_HEADS // N_KV
ATTN_DIM = N_HEADS * HEAD_DIM
KV_DIM = N_KV * HEAD_DIM
CMP_BLOCK = 32
SEL_BLOCK = 64
CMP_PER_SEL = SEL_BLOCK // CMP_BLOCK
N_SEL = 16
WINDOW = 512
Q_BLOCK = 64
N_BUCKETS = 32
REL_MAX_DIST = 128
POOL_WINDOWS = (2, 4, 8, 16)
POOL_DIM = D_MODEL - ATTN_DIM
POOL_GROUP = POOL_DIM // len(POOL_WINDOWS)
POOL_STATE = max(POOL_WINDOWS) - 1
PROJ_DIM = ATTN_DIM + 6 * KV_DIM + 3 * N_HEADS + POOL_DIM
MIX_DIM = ATTN_DIM + POOL_DIM
PEER_HEADS = 8
PEER_KEYS = 128
N_EXPERTS = PEER_KEYS * PEER_KEYS
PEER_DK = D_MODEL // 4
PEER_TOPK = 16
PEER_BLOCK = 128
EPS = 1e-6

kernel_name = "hymba_nsa_pool_peer_decode_step"


def rmsnorm(x, g):
    xf = x.astype(jnp.float32)
    return (xf * lax.rsqrt(jnp.mean(xf * xf, -1, keepdims=True) + EPS) * g).astype(x.dtype)


def masked_softmax(s, mask, axis=-1):
    s = jnp.where(mask, s.astype(jnp.float32), -jnp.inf)
    m = jnp.max(s, axis=axis, keepdims=True)
    e = jnp.exp(s - jnp.where(jnp.isfinite(m), m, 0.0))
    return e / jnp.maximum(jnp.sum(e, axis=axis, keepdims=True), 1e-30)


def t5_bucket(rel):
    n = jnp.maximum(rel, 0)
    exact = N_BUCKETS // 2
    ratio = jnp.log(jnp.maximum(n, exact).astype(jnp.float32) / exact) / math.log(REL_MAX_DIST / exact)
    large = jnp.minimum(exact + (ratio * (N_BUCKETS - exact)).astype(jnp.int32), N_BUCKETS - 1)
    return jnp.where(n < exact, n, large)


def dense_bias(qpos, kpos, rel_bias):
    b = rel_bias[t5_bucket(qpos[:, None] - kpos[None, :])]
    return b.transpose(2, 0, 1).reshape(N_KV, GQA, qpos.shape[0], kpos.shape[0])


def compress(rows, alpha, phi):
    return jnp.einsum('bnjhd,jd,de->bnhe', rows, alpha, phi)


def nsa_core(q, qpos, gates, kc, vc, kc_end, n_cand, fetch, kw, vw, wpos, rel_bias):
    B, Tq = q.shape[:2]
    qg = (q * HEAD_DIM ** -0.5).reshape(B, Tq, N_KV, GQA, HEAD_DIM)
    s_c = jnp.einsum('bqhgd,bnhd->bhgqn', qg, kc) + dense_bias(qpos, kc_end, rel_bias)
    p_c = masked_softmax(s_c, kc_end[None, :] <= qpos[:, None])
    o_c = jnp.einsum('bhgqn,bnhd->bqhgd', p_c, vc)
    imp = p_c.sum(axis=2)
    imp = jnp.pad(imp, ((0, 0), (0, 0), (0, 0), (0, CMP_PER_SEL * n_cand - imp.shape[-1])))
    imp = imp.reshape(B, N_KV, Tq, n_cand, CMP_PER_SEL).sum(-1)
    jpos = jnp.arange(n_cand)
    cur = (qpos // SEL_BLOCK)[:, None]
    forced = (jpos == 0) | (jpos == cur) | (jpos == cur - 1)
    score = jnp.where(forced, jnp.inf, imp)
    score = jnp.where(jpos * SEL_BLOCK <= qpos[:, None], score, -jnp.inf)
    _, sel = lax.top_k(score, min(N_SEL, n_cand))
    sel = sel.transpose(0, 2, 1, 3)
    n_top = sel.shape[-1]
    ks, vs = fetch(sel)
    kpos = sel[..., None] * SEL_BLOCK + jnp.arange(SEL_BLOCK)
    hh = (jnp.arange(N_KV)[:, None] * GQA + jnp.arange(GQA)[None, :]).reshape(1, 1, N_KV, 1, 1, GQA)
    bias_s = rel_bias[t5_bucket(qpos[:, None, None, None] - kpos)[..., None], hh]
    s_s = (jnp.einsum('bqhgd,bqhnjd->bqhnjg', qg, ks) + bias_s).reshape(B, Tq, N_KV, n_top * SEL_BLOCK, GQA)
    m_s = (kpos <= qpos[:, None, None, None]).reshape(B, Tq, N_KV, n_top * SEL_BLOCK, 1)
    p_s = masked_softmax(s_s, m_s, axis=-2)
    o_s = jnp.einsum('bqhmg,bqhmd->bqhgd', p_s, vs.reshape(B, Tq, N_KV, n_top * SEL_BLOCK, HEAD_DIM))
    s_w = jnp.einsum('bqhgd,bkhd->bhgqk', qg, kw) + dense_bias(qpos, wpos, rel_bias)
    dq = qpos[:, None] - wpos[None, :]
    p_w = masked_softmax(s_w, (dq >= 0) & (dq < WINDOW) & (wpos[None, :] >= 0))
    o_w = jnp.einsum('bhgqk,bkhd->bqhgd', p_w, vw)
    g = gates.reshape(B, Tq, 3, N_KV, GQA, 1).astype(jnp.float32)
    o = g[:, :, 0] * o_c + g[:, :, 1] * o_s + g[:, :, 2] * o_w
    return o.reshape(B, Tq, ATTN_DIM).astype(q.dtype)


def nsa_prompt(q, kv, gates, alpha, phi, rel_bias):
    B, S = q.shape[:2]
    kc = compress(kv[:, :, 0].reshape(B, S // CMP_BLOCK, CMP_BLOCK, N_KV, HEAD_DIM), alpha[0], phi[0])
    vc = compress(kv[:, :, 1].reshape(B, S // CMP_BLOCK, CMP_BLOCK, N_KV, HEAD_DIM), alpha[1], phi[1])
    kc_end = (jnp.arange(S // CMP_BLOCK) + 1) * CMP_BLOCK - 1
    n_cand = S // SEL_BLOCK
    sel_blk = kv[:, :, 2:4].reshape(B, n_cand, SEL_BLOCK, 2, N_KV, HEAD_DIM)
    bidx = jnp.arange(B).reshape(B, 1, 1, 1)
    hidx = jnp.arange(N_KV).reshape(1, 1, N_KV, 1)

    def fetch(j):
        kvs = sel_blk[bidx, j, :, :, hidx]
        return kvs[..., 0, :], kvs[..., 1, :]

    win = jnp.pad(kv[:, :, 4:6], ((0, 0), (WINDOW, 0), (0, 0), (0, 0), (0, 0)))

    def q_block(i):
        s0 = i * Q_BLOCK
        qb = lax.dynamic_slice_in_dim(q, s0, Q_BLOCK, axis=1)
        gb = lax.dynamic_slice_in_dim(gates, s0, Q_BLOCK, axis=1)
        wb = lax.dynamic_slice_in_dim(win, s0, Q_BLOCK + WINDOW, axis=1)
        qpos = s0 + jnp.arange(Q_BLOCK)
        wpos = s0 - WINDOW + jnp.arange(Q_BLOCK + WINDOW)
        return nsa_core(qb, qpos, gb, kc, vc, kc_end, n_cand, fetch, wb[:, :, 0], wb[:, :, 1], wpos, rel_bias)

    o = lax.map(q_block, jnp.arange(S // Q_BLOCK))
    o = jnp.moveaxis(o, 0, 1).reshape(B, S, ATTN_DIM)
    return o, kv[:, S - min(WINDOW, S):, 4:6]


def nsa_sample(q, kv, gates, cache_kv, cache_win, page_table, layer, alpha, phi, rel_bias):
    DB, T = q.shape[:2]
    n_pages = page_table.shape[1]
    past = cache_kv[layer, page_table, :, 0:2].reshape(DB, n_pages * PAGE_SIZE // CMP_BLOCK, CMP_BLOCK, 2, N_KV, HEAD_DIM)
    nc_new = T // CMP_BLOCK
    new = kv[:, :nc_new * CMP_BLOCK, 0:2].reshape(DB, nc_new, CMP_BLOCK, 2, N_KV, HEAD_DIM)
    kc = jnp.concatenate([compress(past[:, :, :, 0], alpha[0], phi[0]), compress(new[:, :, :, 0], alpha[0], phi[0])], 1)
    vc = jnp.concatenate([compress(past[:, :, :, 1], alpha[1], phi[1]), compress(new[:, :, :, 1], alpha[1], phi[1])], 1)
    kc_end = (jnp.arange(kc.shape[1]) + 1) * CMP_BLOCK - 1
    p_blocks = PAST_LEN // SEL_BLOCK
    nb_new = -(-T // SEL_BLOCK)
    sel_new = jnp.pad(kv[:, :, 2:4], ((0, 0), (0, nb_new * SEL_BLOCK - T), (0, 0), (0, 0), (0, 0)))
    sel_new = sel_new.reshape(DB, nb_new, SEL_BLOCK, 2, N_KV, HEAD_DIM)
    b4 = jnp.arange(DB).reshape(DB, 1, 1, 1)
    h4 = jnp.arange(N_KV).reshape(1, 1, N_KV, 1)

    def fetch(j):
        rows = jnp.minimum(j, p_blocks - 1)[..., None] * SEL_BLOCK + jnp.arange(SEL_BLOCK)
        page = page_table[b4[..., None], rows // PAGE_SIZE]
        kv_past = cache_kv[layer, page, rows % PAGE_SIZE, 2:4, h4[..., None]]
        kv_new = sel_new[b4, jnp.clip(j - p_blocks, 0, nb_new - 1), :, :, h4]
        kvs = jnp.where((j < p_blocks)[..., None, None, None], kv_past, kv_new)
        return kvs[..., 0, :], kvs[..., 1, :]

    w_buf = cache_win.shape[1]
    win = jnp.concatenate([cache_win.astype(kv.dtype), kv[:, :, 4:6]], axis=1)
    qpos = PAST_LEN + jnp.arange(T)
    wpos = PAST_LEN - w_buf + jnp.arange(w_buf + T)
    o = nsa_core(q, qpos, gates, kc, vc, kc_end, p_blocks + nb_new, fetch, win[:, :, 0], win[:, :, 1], wpos, rel_bias)
    return o, win[:, T:]


def pool_mix(p, prev, pos0, w, scale):
    B, T, C = p.shape
    ext = jnp.concatenate([prev.astype(p.dtype), p], axis=1)
    cs = jnp.cumsum(ext.astype(jnp.float32), axis=1)
    cs = jnp.concatenate([jnp.zeros((B, 1, C), jnp.float32), cs], axis=1)
    pos = pos0 + jnp.arange(T)
    means = []
    for gi, wl in enumerate(POOL_WINDOWS):
        ch = slice(gi * POOL_GROUP, (gi + 1) * POOL_GROUP)
        tot = cs[:, POOL_STATE + 1:POOL_STATE + 1 + T, ch] - cs[:, POOL_STATE + 1 - wl:POOL_STATE + 1 - wl + T, ch]
        cnt = jnp.minimum(wl, pos + 1).astype(jnp.float32)[None, :, None]
        means.append(tot / cnt)
    d = jnp.concatenate(means, -1) - p.astype(jnp.float32)
    y = jnp.einsum('btgc,gce->btge', d.reshape(B, T, len(POOL_WINDOWS), POOL_GROUP), w).reshape(B, T, C) * scale
    return y.astype(p.dtype), ext[:, T:]


def peer_ffn(x, wq, keys, u_tab, v_tab):
    n = x.shape[0]
    nb = -(-n // PEER_BLOCK)
    xb = jnp.pad(x, ((0, nb * PEER_BLOCK - n), (0, 0))).reshape(nb, PEER_BLOCK, D_MODEL)

    def one(xt):
        q = (xt @ wq).reshape(PEER_BLOCK, PEER_HEADS, 2, PEER_DK // 2)
        s = jnp.einsum('thpc,hpkc->thpk', q, keys).astype(jnp.float32)
        s1, i1 = lax.top_k(s[:, :, 0], PEER_TOPK)
        s2, i2 = lax.top_k(s[:, :, 1], PEER_TOPK)
        cand = (s1[..., :, None] + s2[..., None, :]).reshape(PEER_BLOCK, PEER_HEADS, PEER_TOPK * PEER_TOPK)
        best, flat = lax.top_k(cand, PEER_TOPK)
        e = jnp.take_along_axis(i1, flat // PEER_TOPK, -1) * PEER_KEYS + jnp.take_along_axis(i2, flat % PEER_TOPK, -1)
        g = jax.nn.softmax(best, axis=-1)
        act = jax.nn.gelu(jnp.einsum('thkd,td->thk', u_tab[e], xt).astype(jnp.float32)) * g
        return jnp.einsum('thk,thkd->td', act.astype(xt.dtype), v_tab[e])

    return lax.map(one, xb).reshape(nb * PEER_BLOCK, D_MODEL)[:n]


def trunk_layer(x, c, nsa_fn, pool_prev, pos0, w_ada, b_ada, norm_g, w_in, w_out, pool_w, pool_scale,
                peer_wq, peer_keys, peer_u, peer_v):
    B, T, _ = x.shape
    mod = (jax.nn.silu(c) @ w_ada + b_ada).reshape(B, 6, 1, D_MODEL).astype(x.dtype)
    h = rmsnorm(x, norm_g[0]) * (1 + mod[:, 1]) + mod[:, 0]
    u = h @ w_in
    o0 = ATTN_DIM
    o1 = o0 + 6 * KV_DIM
    o2 = o1 + 3 * N_HEADS
    q = u[..., :o0].reshape(B, T, N_HEADS, HEAD_DIM)
    kv = u[..., o0:o1].reshape(B, T, 6, N_KV, HEAD_DIM)
    gates = jax.nn.sigmoid(u[..., o1:o2]).reshape(B, T, 3, N_HEADS)
    o_attn, win_state = nsa_fn(q, kv, gates)
    o_pool, pool_state = pool_mix(u[..., o2:], pool_prev, pos0, pool_w, pool_scale)
    mix = jnp.concatenate([o_attn, o_pool.astype(o_attn.dtype)], -1) @ w_out
    x = x + mod[:, 2] * rmsnorm(mix, norm_g[1])
    h2 = rmsnorm(x, norm_g[2]) * (1 + mod[:, 4]) + mod[:, 3]
    f = peer_ffn(h2.reshape(B * T, D_MODEL), peer_wq, peer_keys, peer_u, peer_v).reshape(B, T, D_MODEL)
    x = x + mod[:, 5] * rmsnorm(f, norm_g[3])
    return x, kv[:, :, :4], win_state, pool_state


def setup_inputs(seed: int = 0) -> dict:
    key = jax.random.key(seed)
    ks = jax.random.split(key, 22)
    nrm = jax.random.normal
    f32 = jnp.float32
    n_pages = PAST_LEN // PAGE_SIZE
    n_pool = (DEC_BATCH * n_pages * 5) // 4
    w_buf = min(WINDOW, PAST_LEN)
    page_table = jax.random.permutation(ks[5], n_pool)[:DEC_BATCH * n_pages].reshape(DEC_BATCH, n_pages).astype(jnp.int32)
    return {
        "x_prompt": nrm(ks[0], (BATCH, SEQ, D_MODEL), f32),
        "x_sample": nrm(ks[1], (DEC_BATCH, DEC_SEQ, D_MODEL), f32),
        "cache_kv": nrm(ks[2], (DEPTH, n_pool, PAGE_SIZE, 4, N_KV, HEAD_DIM), f32),
        "cache_win": nrm(ks[3], (DEPTH, DEC_BATCH, w_buf, 2, N_KV, HEAD_DIM), f32),
        "state_pool": nrm(ks[4], (DEPTH, DEC_BATCH, POOL_STATE, POOL_DIM), f32),
        "page_table": page_table,
        "c_prompt": nrm(ks[6], (BATCH, D_MODEL), f32),
        "c_sample": nrm(ks[7], (DEC_BATCH, D_MODEL), f32),
        "w_ada": nrm(ks[8], (DEPTH, D_MODEL, 6 * D_MODEL), f32) * (0.5 * D_MODEL ** -0.5),
        "b_ada": nrm(ks[9], (DEPTH, 6 * D_MODEL), f32) * 0.02,
        "norm_g": 1.0 + 0.02 * nrm(ks[10], (DEPTH, 4, D_MODEL), f32),
        "w_in": nrm(ks[11], (DEPTH, D_MODEL, PROJ_DIM), f32) * D_MODEL ** -0.5,
        "w_out": nrm(ks[12], (DEPTH, MIX_DIM, D_MODEL), f32) * MIX_DIM ** -0.5,
        "cmp_alpha": (1.0 + 0.1 * nrm(ks[13], (DEPTH, 2, CMP_BLOCK, HEAD_DIM), f32)) / CMP_BLOCK,
        "cmp_phi": nrm(ks[14], (DEPTH, 2, HEAD_DIM, HEAD_DIM), f32) * (CMP_BLOCK / HEAD_DIM) ** 0.5,
        "rel_bias": 0.5 * nrm(ks[15], (N_BUCKETS, N_HEADS), f32),
        "pool_w": nrm(ks[16], (DEPTH, len(POOL_WINDOWS), POOL_GROUP, POOL_GROUP), f32) * POOL_GROUP ** -0.5,
        "pool_scale": 1.0 + 0.1 * nrm(ks[17], (DEPTH, POOL_DIM), f32),
        "peer_wq": nrm(ks[18], (DEPTH, D_MODEL, PEER_HEADS * PEER_DK), f32) * D_MODEL ** -0.5,
        "peer_keys": nrm(ks[19], (DEPTH, PEER_HEADS, 2, PEER_KEYS, PEER_DK // 2), f32) * (PEER_DK // 2) ** -0.5,
        "peer_u": nrm(ks[20], (DEPTH, N_EXPERTS, D_MODEL), f32) * D_MODEL ** -0.5,
        "peer_v": nrm(ks[21], (DEPTH, N_EXPERTS, D_MODEL), f32) * D_MODEL ** -0.5,
    }


def reference(x_prompt, x_sample, cache_kv, cache_win, state_pool, page_table, c_prompt, c_sample,
              w_ada, b_ada, norm_g, w_in, w_out, cmp_alpha, cmp_phi, rel_bias, pool_w, pool_scale,
              peer_wq, peer_keys, peer_u, peer_v):
    yp, ys = x_prompt, x_sample
    kv_p, kv_s, win_p, win_s, pool_p, pool_s = [], [], [], [], [], []
    for l in range(DEPTH):
        lw = (w_ada[l], b_ada[l], norm_g[l], w_in[l], w_out[l], pool_w[l], pool_scale[l],
              peer_wq[l], peer_keys[l], peer_u[l], peer_v[l])
        nsa_p = functools.partial(nsa_prompt, alpha=cmp_alpha[l], phi=cmp_phi[l], rel_bias=rel_bias)
        prev0 = jnp.zeros((yp.shape[0], POOL_STATE, POOL_DIM), yp.dtype)
        yp, a_kv, a_win, a_pool = trunk_layer(yp, c_prompt, nsa_p, prev0, 0, *lw)
        nsa_s = functools.partial(nsa_sample, cache_kv=cache_kv, cache_win=cache_win[l], page_table=page_table,
                                  layer=l, alpha=cmp_alpha[l], phi=cmp_phi[l], rel_bias=rel_bias)
        ys, b_kv, b_win, b_pool = trunk_layer(ys, c_sample, nsa_s, state_pool[l], PAST_LEN, *lw)
        kv_p.append(a_kv)
        win_p.append(a_win)
        pool_p.append(a_pool)
        kv_s.append(b_kv)
        win_s.append(b_win)
        pool_s.append(b_pool)
    return (yp, ys, jnp.stack(kv_p), jnp.stack(kv_s), jnp.stack(win_p), jnp.stack(win_s), jnp.stack(pool_p), jnp.stack(pool_s))
```

```python
import functools
import math

import numpy as np
import jax
import jax.numpy as jnp
from jax import lax
from jax.experimental import pallas as pl
from jax.experimental.pallas import tpu as pltpu

F32, BF16, I32, U32 = jnp.float32, jnp.bfloat16, jnp.int32, jnp.uint32

D_MODEL = 1024
N_HEADS = 8
HEAD_DIM = 64
N_KV = 2
GQA = 4
CMP_BLOCK = 32
SEL_BLOCK = 64
N_SEL = 16
WINDOW = 512
N_BUCKETS = 32
REL_MAX_DIST = 128
POOL_WINDOWS = (2, 4, 8, 16)
POOL_DIM = 512
POOL_GROUP = 128
POOL_STATE = 15
PEER_HEADS = 8
PEER_KEYS = 128
PEER_TOPK = 16
EPS = 1e-6
LANE = 128
NEG = -1e30
BIG = 1e30
REMOVED = -3e38
VMEM_LIMIT = 56 * 1024 * 1024

C_Q, C_KV4, C_WIN, C_POOL, C_GATE, C_END = 0, 1024, 1536, 1792, 2304, 2432
C_SELWIN = 1280


def _cparams(sem):
    return pltpu.CompilerParams(dimension_semantics=sem, vmem_limit_bytes=VMEM_LIMIT)


def _bucket_thresholds():
    n = np.arange(0, 4 * REL_MAX_DIST)
    exact = N_BUCKETS // 2
    ratio = np.log(np.maximum(n, exact).astype(np.float32) / np.float32(exact)) / np.float32(
        math.log(REL_MAX_DIST / exact))
    large = np.minimum(exact + (ratio * np.float32(N_BUCKETS - exact)).astype(np.int32), N_BUCKETS - 1)
    b = np.where(n < exact, n, large)
    return [int(np.argmax(b >= j)) for j in range(N_BUCKETS)]


_THR = _bucket_thresholds()
FAR_DIST = _THR[N_BUCKETS - 1]


def _bias_from_dist(d, rb):
    val = jnp.full(d.shape, rb(0), F32)
    for j in range(1, N_BUCKETS):
        if _THR[j] == _THR[j - 1] and j > 1:
            continue
        jj = j
        while jj + 1 < N_BUCKETS and _THR[jj + 1] == _THR[j]:
            jj += 1
        val = jnp.where(d >= _THR[j], rb(jj), val)
    return val


def _dot_nt(a, b):
    return lax.dot_general(a, b, (((1,), (1,)), ((), ())), preferred_element_type=F32)


def _mod_body(c_ref, w_ref, b_ref, o_ref):
    c = c_ref[...]
    a = (c * jax.nn.sigmoid(c)).astype(BF16)
    o_ref[...] = jnp.dot(a, w_ref[...].astype(BF16), preferred_element_type=F32) + b_ref[...]


def _mod(c, w_ada, b_ada):
    r, d = c.shape
    n = w_ada.shape[1]
    tn = 1536
    return pl.pallas_call(
        _mod_body,
        grid=(n // tn,),
        in_specs=[pl.BlockSpec((r, d), lambda j: (0, 0)),
                  pl.BlockSpec((d, tn), lambda j: (0, j)),
                  pl.BlockSpec((1, tn), lambda j: (0, j))],
        out_specs=pl.BlockSpec((r, tn), lambda j: (0, j)),
        out_shape=jax.ShapeDtypeStruct((r, n), F32),
        compiler_params=_cparams(("arbitrary",)),
    )(c, w_ada, b_ada.reshape(1, n))


def _norm_mod(x, g, sc, sh):
    ms = jnp.mean(x * x, axis=-1, keepdims=True)
    h = x * lax.rsqrt(ms + EPS) * g
    return h * (1.0 + sc) + sh


def _inproj_body(x_ref, sc_ref, sh_ref, g_ref, w_ref, q_ref, kv_ref, sw_ref, win_ref, pool_ref, gate_ref):
    h = _norm_mod(x_ref[0], g_ref[...], sc_ref[0], sh_ref[0])
    u = jnp.dot(h.astype(BF16), w_ref[...], preferred_element_type=F32)
    q_ref[0] = u[:, C_Q:C_KV4].astype(BF16)
    kv_ref[0] = u[:, C_KV4:C_WIN]
    sw_ref[0] = u[:, C_SELWIN:C_POOL].astype(BF16)
    win_ref[0] = u[:, C_WIN:C_POOL]
    pool_ref[0] = u[:, C_POOL:C_GATE]
    gate_ref[0] = jax.nn.sigmoid(u[:, C_GATE:C_END])


def _mod_specs(t, tm, d, per_row):
    if per_row:
        return pl.BlockSpec((1, tm, d), lambda b, i: (b, i, 0))
    return pl.BlockSpec((1, 1, d), lambda b, i: (b, 0, 0))


def _inproj(x, sc, sh, g, w, tm):
    bx, t, d = x.shape
    per_row = sc.shape[1] != 1
    ms = _mod_specs(t, tm, d, per_row)
    widths = (1024, 512, 512, 256, 512, 128)
    dtypes = (BF16, F32, BF16, F32, F32, F32)
    return pl.pallas_call(
        _inproj_body,
        grid=(bx, t // tm),
        in_specs=[pl.BlockSpec((1, tm, d), lambda b, i: (b, i, 0)), ms, ms,
                  pl.BlockSpec((1, d), lambda b, i: (0, 0)),
                  pl.BlockSpec((d, C_END), lambda b, i: (0, 0))],
        out_specs=[pl.BlockSpec((1, tm, wd), lambda b, i: (b, i, 0)) for wd in widths],
        out_shape=[jax.ShapeDtypeStruct((bx, t, wd), dt) for wd, dt in zip(widths, dtypes)],
        compiler_params=_cparams(("arbitrary", "arbitrary")),
    )(x, sc, sh, g.reshape(1, d), w)


def _peerq_body(x_ref, sc_ref, sh_ref, g_ref, w_ref, q_ref, h_ref):
    h = _norm_mod(x_ref[0], g_ref[...], sc_ref[0], sh_ref[0])
    h_ref[0] = h
    u = jnp.dot(h.astype(BF16), w_ref[...], preferred_element_type=F32)
    for j in range(2 * PEER_HEADS):
        q_ref[j, 0] = u[:, j * LANE:(j + 1) * LANE].astype(BF16)


def _peerq(x, sc, sh, g, w, tm):
    bx, t, d = x.shape
    per_row = sc.shape[1] != 1
    ms = _mod_specs(t, tm, d, per_row)
    nq = 2 * PEER_HEADS
    return pl.pallas_call(
        _peerq_body,
        grid=(bx, t // tm),
        in_specs=[pl.BlockSpec((1, tm, d), lambda b, i: (b, i, 0)), ms, ms,
                  pl.BlockSpec((1, d), lambda b, i: (0, 0)),
                  pl.BlockSpec((d, nq * LANE), lambda b, i: (0, 0))],
        out_specs=[pl.BlockSpec((nq, 1, tm, LANE), lambda b, i: (0, b, i, 0)),
                   pl.BlockSpec((1, tm, d), lambda b, i: (b, i, 0))],
        out_shape=[jax.ShapeDtypeStruct((nq, bx, t, LANE), BF16),
                   jax.ShapeDtypeStruct((bx, t, d), F32)],
        compiler_params=_cparams(("arbitrary", "arbitrary")),
    )(x, sc, sh, g.reshape(1, d), w)


def _compress_rows(rows, a, phi):
    n = rows.shape[0] // CMP_BLOCK
    w = jnp.sum(rows.reshape(n, CMP_BLOCK, rows.shape[1]) * a[None], axis=1)
    return jnp.dot(w, phi, preferred_element_type=F32, precision=lax.Precision.HIGHEST)


def _compress_body(x_ref, a_ref, phi_ref, o_ref):
    o_ref[0] = _compress_rows(x_ref[0], a_ref[...], phi_ref[...])


def _compress(kv4, a, phi, tm):
    b, s, _ = kv4.shape
    return pl.pallas_call(
        _compress_body,
        grid=(b, s // tm),
        in_specs=[pl.BlockSpec((1, tm, 256), lambda bi, i: (bi, i, 0)),
                  pl.BlockSpec((CMP_BLOCK, 256), lambda bi, i: (0, 0)),
                  pl.BlockSpec((256, 256), lambda bi, i: (0, 0))],
        out_specs=pl.BlockSpec((1, tm // CMP_BLOCK, 256), lambda bi, i: (bi, i, 0)),
        out_shape=jax.ShapeDtypeStruct((b, s // CMP_BLOCK, 256), F32),
        compiler_params=_cparams(("arbitrary", "arbitrary")),
    )(kv4, a, phi)


def _compress_paged_body(pt_ref, p0_ref, p1_ref, a_ref, phi_ref, o_ref):
    rows = jnp.concatenate([p0_ref[0], p1_ref[0]], axis=0)
    o_ref[0] = _compress_rows(rows, a_ref[...], phi_ref[...])


def _compress_paged(cache3, page_table, a, phi):
    db, n_pages = page_table.shape
    page = cache3.shape[1]
    per = page // CMP_BLOCK
    grid_spec = pltpu.PrefetchScalarGridSpec(
        num_scalar_prefetch=1,
        grid=(db, n_pages // 2),
        in_specs=[pl.BlockSpec((1, page, 256), lambda b, i, pt: (pt[b, 2 * i], 0, 0)),
                  pl.BlockSpec((1, page, 256), lambda b, i, pt: (pt[b, 2 * i + 1], 0, 0)),
                  pl.BlockSpec((CMP_BLOCK, 256), lambda b, i, pt: (0, 0)),
                  pl.BlockSpec((256, 256), lambda b, i, pt: (0, 0))],
        out_specs=pl.BlockSpec((1, 2 * per, 256), lambda b, i, pt: (b, i, 0)),
    )
    return pl.pallas_call(
        _compress_paged_body,
        grid_spec=grid_spec,
        out_shape=jax.ShapeDtypeStruct((db, n_pages * per, 256), F32),
        compiler_params=_cparams(("arbitrary", "arbitrary")),
    )(page_table, cache3, cache3, a, phi)


def _select_blocks(score, n_pick):
    lane = lax.broadcasted_iota(I32, score.shape, 1)
    nl = score.shape[1]
    sel = jnp.zeros(score.shape, jnp.bool_)
    idxs = []
    for _ in range(n_pick):
        mx = jnp.max(score, axis=1, keepdims=True)
        idx = jnp.min(jnp.where(score == mx, lane, nl), axis=1, keepdims=True)
        hit = lane == idx
        sel = sel | hit
        score = jnp.where(hit, REMOVED, score)
        idxs.append(idx)
    return sel, idxs


def _flash_update(slot, qh, kc, vc, bias, mask, acc_sc, m_sc, l_sc):
    s = _dot_nt(qh, kc) + bias
    if mask is not None:
        s = jnp.where(mask, s, NEG)
    m_old = m_sc[slot]
    m_new = jnp.maximum(m_old, jnp.max(s, axis=1, keepdims=True))
    alpha = jnp.exp(m_old - m_new)
    p = jnp.exp(s - m_new)
    l_sc[slot] = alpha * l_sc[slot] + jnp.sum(p, axis=1, keepdims=True)
    acc_sc[slot] = alpha * acc_sc[slot] + jnp.dot(p.astype(BF16), vc, preferred_element_type=F32)
    m_sc[slot] = m_new


def _nsa_body(rb_ref, q_ref, sw_ref, kc_ref, g_ref, o_ref,
              bdiag_sc, bcmp_sc, acc_sc, m_sc, l_sc, osum_sc, *, seq, tq):
    i = pl.program_id(1)
    q0 = i * tq
    nb = seq // SEL_BLOCK
    per = tq // SEL_BLOCK

    @pl.when((pl.program_id(0) == 0) & (i == 0))
    def _build_bias_tiles():
        t = lax.broadcasted_iota(I32, (tq, tq), 0)
        k = lax.broadcasted_iota(I32, (tq, tq), 1)
        tc = lax.broadcasted_iota(I32, (tq, nb), 0)
        m = lax.broadcasted_iota(I32, (tq, nb), 1)
        r = jnp.where(m < per, m, m - nb)
        for hd in range(N_HEADS):
            rb = lambda j, hd=hd: rb_ref[j, hd]
            for dl in range(2):
                bdiag_sc[hd, dl] = _bias_from_dist(t - k + dl * tq, rb)
            for par in range(2):
                bcmp_sc[hd, par] = _bias_from_dist(
                    tc - (SEL_BLOCK * r + (par + 1) * CMP_BLOCK - 1), rb)

    kc2 = kc_ref[0]
    kce, vce = kc2[:, 0:128].astype(BF16), kc2[:, 128:256].astype(BF16)
    kco, vco = kc2[:, 256:384].astype(BF16), kc2[:, 384:512].astype(BF16)
    qpos = q0 + lax.broadcasted_iota(I32, (tq, nb), 0)
    jl = lax.broadcasted_iota(I32, (tq, nb), 1)
    valid_e = qpos >= SEL_BLOCK * jl + CMP_BLOCK - 1
    valid_o = qpos >= SEL_BLOCK * jl + 2 * CMP_BLOCK - 1
    cur = qpos // SEL_BLOCK
    forced = (jl == 0) | (jl == cur) | (jl == cur - 1)
    causal_blk = jl * SEL_BLOCK <= qpos
    shift = (i * per) % nb
    tt = lax.broadcasted_iota(I32, (tq, tq), 0)
    kk = lax.broadcasted_iota(I32, (tq, tq), 1)
    causal = tt >= kk
    gates = g_ref[0]
    ej = lax.broadcasted_iota(I32, (nb, tq), 0)
    ek = lax.broadcasted_iota(I32, (nb, tq), 1) // SEL_BLOCK

    def reset_state():
        m_sc[...] = jnp.full(m_sc.shape, NEG, F32)
        l_sc[...] = jnp.zeros(l_sc.shape, F32)
        acc_sc[...] = jnp.zeros(acc_sc.shape, F32)

    def qhead(hd):
        return q_ref[0, :, hd * LANE:(hd + 1) * LANE]

    def kv_chunk(c, col):
        start = pl.multiple_of(c * tq, tq)
        return (sw_ref[0, pl.ds(start, tq), col:col + LANE],
                sw_ref[0, pl.ds(start, tq), col + LANE:col + 2 * LANE])

    def finish(kvh, branch):
        for g in range(GQA):
            hd = kvh * GQA + g
            o = acc_sc[g] / l_sc[g]
            gc = gates[:, branch * N_HEADS + hd:branch * N_HEADS + hd + 1]
            osum_sc[:, hd * LANE:(hd + 1) * LANE] += gc * o

    for kvh in range(N_KV):
        imp = jnp.zeros((tq, nb), F32)
        for g in range(GQA):
            hd = kvh * GQA + g
            qh = qhead(hd)
            s_e = _dot_nt(qh, kce) + pltpu.roll(bcmp_sc[hd, 0], shift, 1)
            s_o = _dot_nt(qh, kco) + pltpu.roll(bcmp_sc[hd, 1], shift, 1)
            s_e = jnp.where(valid_e, s_e, NEG)
            s_o = jnp.where(valid_o, s_o, NEG)
            mx = jnp.maximum(jnp.max(s_e, axis=1, keepdims=True), jnp.max(s_o, axis=1, keepdims=True))
            e_e = jnp.where(valid_e, jnp.exp(s_e - mx), 0.0)
            e_o = jnp.where(valid_o, jnp.exp(s_o - mx), 0.0)
            den = jnp.maximum(jnp.sum(e_e, axis=1, keepdims=True) + jnp.sum(e_o, axis=1, keepdims=True), 1e-30)
            p_e, p_o = e_e / den, e_o / den
            imp = imp + (p_e + p_o)
            o_c = (jnp.dot(p_e.astype(BF16), vce, preferred_element_type=F32)
                   + jnp.dot(p_o.astype(BF16), vco, preferred_element_type=F32))
            gc = gates[:, hd:hd + 1]
            osum_sc[:, hd * LANE:(hd + 1) * LANE] = gc * o_c

        score = jnp.where(forced, BIG, imp)
        score = jnp.where(causal_blk, score, NEG)
        selm, _ = _select_blocks(score, min(N_SEL, nb))
        selm = jnp.where(selm, 1.0, 0.0).astype(BF16)

        def sel_mask(c):
            expand = jnp.where(ej == ek + c * per, 1.0, 0.0).astype(BF16)
            return jnp.dot(selm, expand, preferred_element_type=F32) > 0.5

        reset_state()

        def far_chunk(c, carry):
            kc, vc = kv_chunk(c, 0)
            mask = sel_mask(c)
            for g in range(GQA):
                hd = kvh * GQA + g
                _flash_update(g, qhead(hd), kc, vc, rb_ref[N_BUCKETS - 1, hd], mask, acc_sc, m_sc, l_sc)
            return carry

        lax.fori_loop(0, jnp.maximum(i - 1, 0), far_chunk, 0)

        @pl.when(i >= 1)
        def _prev_chunk():
            kc, vc = kv_chunk(i - 1, 0)
            mask = sel_mask(i - 1)
            for g in range(GQA):
                hd = kvh * GQA + g
                _flash_update(g, qhead(hd), kc, vc, bdiag_sc[hd, 1], mask, acc_sc, m_sc, l_sc)

        kc, vc = kv_chunk(i, 0)
        mask = sel_mask(i) & causal
        for g in range(GQA):
            hd = kvh * GQA + g
            _flash_update(g, qhead(hd), kc, vc, bdiag_sc[hd, 0], mask, acc_sc, m_sc, l_sc)
        finish(kvh, 1)

        reset_state()

        @pl.when(i >= 2)
        def _win_far():
            kc, vc = kv_chunk(i - 2, 2 * LANE)
            for g in range(GQA):
                hd = kvh * GQA + g
                _flash_update(g, qhead(hd), kc, vc, rb_ref[N_BUCKETS - 1, hd], kk > tt, acc_sc, m_sc, l_sc)

        @pl.when(i >= 1)
        def _win_prev():
            kc, vc = kv_chunk(i - 1, 2 * LANE)
            for g in range(GQA):
                hd = kvh * GQA + g
                _flash_update(g, qhead(hd), kc, vc, bdiag_sc[hd, 1], None, acc_sc, m_sc, l_sc)

        kc, vc = kv_chunk(i, 2 * LANE)
        for g in range(GQA):
            hd = kvh * GQA + g
            _flash_update(g, qhead(hd), kc, vc, bdiag_sc[hd, 0], causal, acc_sc, m_sc, l_sc)
        finish(kvh, 2)

    o_ref[0] = osum_sc[...].astype(BF16)


def _nsa_prompt(qpad, selwin, kcvc2, gates, rel_bias, tq):
    b, s, _ = qpad.shape
    nb = s // SEL_BLOCK
    assert WINDOW == 2 * tq and tq + 1 >= FAR_DIST and s % tq == 0
    body = functools.partial(_nsa_body, seq=s, tq=tq)
    return pl.pallas_call(
        body,
        grid=(b, s // tq),
        in_specs=[pl.BlockSpec(memory_space=pltpu.SMEM),
                  pl.BlockSpec((1, tq, N_HEADS * LANE), lambda bi, i: (bi, i, 0)),
                  pl.BlockSpec((1, s, 512), lambda bi, i: (bi, 0, 0)),
                  pl.BlockSpec((1, nb, 512), lambda bi, i: (bi, 0, 0)),
                  pl.BlockSpec((1, tq, LANE), lambda bi, i: (bi, i, 0))],
        out_specs=pl.BlockSpec((1, tq, N_HEADS * LANE), lambda bi, i: (bi, i, 0)),
        out_shape=jax.ShapeDtypeStruct((b, s, N_HEADS * LANE), BF16),
        scratch_shapes=[pltpu.VMEM((N_HEADS, 2, tq, tq), F32),
                        pltpu.VMEM((N_HEADS, 2, tq, nb), F32),
                        pltpu.VMEM((GQA, tq, LANE), F32),
                        pltpu.VMEM((GQA, tq, 1), F32),
                        pltpu.VMEM((GQA, tq, 1), F32),
                        pltpu.VMEM((tq, N_HEADS * LANE), F32)],
        compiler_params=_cparams(("arbitrary", "arbitrary")),
    )(rel_bias, qpad, selwin, kcvc2, gates)


def _head_rows(x):
    sub = lax.broadcasted_iota(I32, (N_HEADS, LANE), 0)
    lane = lax.broadcasted_iota(I32, (N_HEADS, LANE), 1)
    return jnp.sum(jnp.where(sub == lane, jnp.broadcast_to(x, (N_HEADS, LANE)), 0.0), axis=1, keepdims=True)


def _nsa_s1_body(q_ref, kc_ref, cw_ref, kvn_ref, wn_ref, g_ref, rbt_ref, ocw_ref, idx_ref, *, past):
    nb = kc_ref.shape[1]
    wbuf = cw_ref.shape[1]
    q = q_ref[0]
    rbt = rbt_ref[...]
    rb = lambda j: rbt[:, j:j + 1]
    gates = g_ref[0]
    g_c = _head_rows(gates[:, 0:LANE])
    g_w = _head_rows(pltpu.roll(gates, LANE - 2 * N_HEADS, 1))

    kc2 = kc_ref[0]
    jl = lax.broadcasted_iota(I32, (N_HEADS, nb), 1)
    d_e = past - (SEL_BLOCK * jl + CMP_BLOCK - 1)
    d_o = past - (SEL_BLOCK * jl + 2 * CMP_BLOCK - 1)
    s_e = _dot_nt(q, kc2[:, 0:128].astype(BF16)) + _bias_from_dist(d_e, rb)
    s_o = _dot_nt(q, kc2[:, 256:384].astype(BF16)) + _bias_from_dist(d_o, rb)
    valid_e, valid_o = d_e >= 0, d_o >= 0
    s_e = jnp.where(valid_e, s_e, NEG)
    s_o = jnp.where(valid_o, s_o, NEG)
    mx = jnp.maximum(jnp.max(s_e, axis=1, keepdims=True), jnp.max(s_o, axis=1, keepdims=True))
    e_e = jnp.where(valid_e, jnp.exp(s_e - mx), 0.0)
    e_o = jnp.where(valid_o, jnp.exp(s_o - mx), 0.0)
    den = jnp.maximum(jnp.sum(e_e, axis=1, keepdims=True) + jnp.sum(e_o, axis=1, keepdims=True), 1e-30)
    p_e, p_o = e_e / den, e_o / den
    o_c = (jnp.dot(p_e.astype(BF16), kc2[:, 128:256].astype(BF16), preferred_element_type=F32)
           + jnp.dot(p_o.astype(BF16), kc2[:, 384:512].astype(BF16), preferred_element_type=F32))

    pe = p_e + p_o
    imp = jnp.concatenate([jnp.sum(pe[0:GQA], axis=0, keepdims=True),
                           jnp.sum(pe[GQA:2 * GQA], axis=0, keepdims=True)], axis=0)
    jl2 = lax.broadcasted_iota(I32, (N_KV, nb), 1)
    cur = past // SEL_BLOCK
    forced = (jl2 == 0) | (jl2 == cur) | (jl2 == cur - 1)
    score = jnp.where(forced, BIG, imp)
    score = jnp.where(jl2 * SEL_BLOCK <= past, score, NEG)
    _, idxs = _select_blocks(score, N_SEL - 1)
    lane = lax.broadcasted_iota(I32, (N_KV, LANE), 1)
    out_idx = jnp.zeros((N_KV, LANE), I32)
    for r, ix in enumerate(idxs):
        out_idx = jnp.where(lane == r, ix, out_idx)
    idx_ref[0] = jnp.concatenate([out_idx, jnp.zeros((8 - N_KV, LANE), I32)], axis=0)

    cw = cw_ref[0]
    il = lax.broadcasted_iota(I32, (N_HEADS, wbuf), 1)
    dq = wbuf - il
    s_w = _dot_nt(q, cw[:, 0:128].astype(BF16)) + _bias_from_dist(dq, rb)
    valid_w = dq < WINDOW
    s_w = jnp.where(valid_w, s_w, NEG)
    wn = wn_ref[0]
    qf = q.astype(F32)
    s_n = jnp.sum(qf * wn[:, 0:128].astype(BF16).astype(F32), axis=1, keepdims=True) + rb(0)
    mw = jnp.maximum(jnp.max(s_w, axis=1, keepdims=True), s_n)
    e_w = jnp.where(valid_w, jnp.exp(s_w - mw), 0.0)
    e_n = jnp.exp(s_n - mw)
    den_w = jnp.sum(e_w, axis=1, keepdims=True) + e_n
    o_w = (jnp.dot(e_w.astype(BF16), cw[:, 128:256].astype(BF16), preferred_element_type=F32)
           + e_n * wn[:, 128:256]) / den_w
    ocw_ref[0] = g_c * o_c + g_w * o_w


def _nsa_s1(q8, kcvc2, cache_win3, kv4n, winn, gates, rbt, past):
    db = q8.shape[0]
    nb = kcvc2.shape[1]
    wbuf = cache_win3.shape[1]
    body = functools.partial(_nsa_s1_body, past=past)
    return pl.pallas_call(
        body,
        grid=(db,),
        in_specs=[pl.BlockSpec((1, N_HEADS, LANE), lambda b: (b, 0, 0)),
                  pl.BlockSpec((1, nb, 512), lambda b: (b, 0, 0)),
                  pl.BlockSpec((1, wbuf, 256), lambda b: (b, 0, 0)),
                  pl.BlockSpec((1, 1, 512), lambda b: (b, 0, 0)),
                  pl.BlockSpec((1, 1, 256), lambda b: (b, 0, 0)),
                  pl.BlockSpec((1, 1, LANE), lambda b: (b, 0, 0)),
                  pl.BlockSpec((N_HEADS, N_BUCKETS), lambda b: (0, 0))],
        out_specs=[pl.BlockSpec((1, N_HEADS, LANE), lambda b: (b, 0, 0)),
                   pl.BlockSpec((1, 8, LANE), lambda b: (b, 0, 0))],
        out_shape=[jax.ShapeDtypeStruct((db, N_HEADS, LANE), F32),
                   jax.ShapeDtypeStruct((db, 8, LANE), I32)],
        compiler_params=_cparams(("arbitrary",)),
    )(q8, kcvc2, cache_win3, kv4n, winn, gates, rbt)


def _nsa_s2_body(sel_ref, pt_ref, q_ref, blk_ref, kvn_ref, g_ref, rbt_ref, ocw_ref, o_ref,
                 acc_sc, m_sc, l_sc, *, past, npick):
    b, h, k = pl.program_id(0), pl.program_id(1), pl.program_id(2)
    q = q_ref[0]
    rbt = rbt_ref[...]
    rb = lambda j: rbt[:, j:j + 1]
    kvn = kvn_ref[0]

    @pl.when((h == 0) & (k == 0))
    def _init():
        s_n = jnp.sum(q.astype(F32) * kvn[:, 256:384].astype(BF16).astype(F32), axis=1, keepdims=True) + rb(0)
        m_sc[...] = s_n
        l_sc[...] = jnp.ones(l_sc.shape, F32)
        acc_sc[...] = jnp.broadcast_to(kvn[:, 384:512], acc_sc.shape)

    j = sel_ref[(b * N_KV + h) * npick + k]
    blk = blk_ref[0]
    r = lax.broadcasted_iota(I32, (N_HEADS, SEL_BLOCK), 1)
    d = past - (j * SEL_BLOCK + r)
    s = _dot_nt(q, blk[:, 0:128].astype(BF16)) + _bias_from_dist(d, rb)
    s = jnp.where(d >= 0, s, NEG)
    m_old = m_sc[...]
    m_new = jnp.maximum(m_old, jnp.max(s, axis=1, keepdims=True))
    alpha = jnp.exp(m_old - m_new)
    p = jnp.exp(s - m_new)
    l_new = alpha * l_sc[...] + jnp.sum(p, axis=1, keepdims=True)
    acc_new = alpha * acc_sc[...] + jnp.dot(p.astype(BF16), blk[:, 128:256].astype(BF16),
                                            preferred_element_type=F32)
    rows = lax.broadcasted_iota(I32, (N_HEADS, 1), 0) // GQA == h
    m_sc[...] = jnp.where(rows, m_new, m_old)
    l_sc[...] = jnp.where(rows, l_new, l_sc[...])
    acc_sc[...] = jnp.where(rows, acc_new, acc_sc[...])

    @pl.when((h == N_KV - 1) & (k == npick - 1))
    def _fin():
        g_s = _head_rows(pltpu.roll(g_ref[0], LANE - N_HEADS, 1))
        o_ref[0] = ocw_ref[0] + g_s * (acc_sc[...] / l_sc[...])


def _nsa_s2(sel_flat, page_table, q8, cache_blk, kv4n, gates, rbt, ocw, past, page):
    db = q8.shape[0]
    npick = N_SEL - 1
    bpp = page // SEL_BLOCK
    body = functools.partial(_nsa_s2_body, past=past, npick=npick)

    def blk_map(b, h, k, sel, pt):
        j = sel[(b * N_KV + h) * npick + k]
        return (pt[b, j // bpp] * bpp + j % bpp, 0, 1)

    grid_spec = pltpu.PrefetchScalarGridSpec(
        num_scalar_prefetch=2,
        grid=(db, N_KV, npick),
        in_specs=[pl.BlockSpec((1, N_HEADS, LANE), lambda b, h, k, sel, pt: (b, 0, 0)),
                  pl.BlockSpec((1, SEL_BLOCK, 256), blk_map),
                  pl.BlockSpec((1, 1, 512), lambda b, h, k, sel, pt: (b, 0, 0)),
                  pl.BlockSpec((1, 1, LANE), lambda b, h, k, sel, pt: (b, 0, 0)),
                  pl.BlockSpec((N_HEADS, N_BUCKETS), lambda b, h, k, sel, pt: (0, 0)),
                  pl.BlockSpec((1, N_HEADS, LANE), lambda b, h, k, sel, pt: (b, 0, 0))],
        out_specs=pl.BlockSpec((1, N_HEADS, LANE), lambda b, h, k, sel, pt: (b, 0, 0)),
        scratch_shapes=[pltpu.VMEM((N_HEADS, LANE), F32),
                        pltpu.VMEM((N_HEADS, 1), F32),
                        pltpu.VMEM((N_HEADS, 1), F32)],
    )
    return pl.pallas_call(
        body,
        grid_spec=grid_spec,
        out_shape=jax.ShapeDtypeStruct((db, N_HEADS, LANE), F32),
        compiler_params=_cparams(("arbitrary", "arbitrary", "arbitrary")),
    )(sel_flat, page_table, q8, cache_blk, kv4n, gates, rbt, ocw)


def _pool_body(x_ref, halo_ref, prev_ref, w_ref, sc_ref, o_ref, *, tm, pos_base):
    i = pl.program_id(1)
    halo = jnp.where(i == 0, prev_ref[0], halo_ref[0])
    cur = x_ref[0]
    ext = jnp.concatenate([halo, cur], axis=0)
    sums = {1: ext}
    w = 1
    while w < max(POOL_WINDOWS):
        sums[2 * w] = sums[w] + pltpu.roll(sums[w], w, 0)
        w *= 2
    pos = pos_base + i * tm + lax.broadcasted_iota(I32, (tm, 1), 0)
    outs = []
    for gi, wl in enumerate(POOL_WINDOWS):
        ch = slice(gi * POOL_GROUP, (gi + 1) * POOL_GROUP)
        tot = sums[wl][16:, ch]
        cnt = jnp.minimum(wl, pos + 1).astype(F32)
        d = tot / cnt - cur[:, ch]
        outs.append(jnp.dot(d.astype(BF16), w_ref[gi].astype(BF16), preferred_element_type=F32))
    o_ref[0] = jnp.concatenate(outs, axis=1) * sc_ref[...]


def _pool(p, prev16, pool_w, pool_scale, tm, pos_base):
    b, t, c = p.shape
    body = functools.partial(_pool_body, tm=tm, pos_base=pos_base)
    hb = tm // 16
    return pl.pallas_call(
        body,
        grid=(b, t // tm),
        in_specs=[pl.BlockSpec((1, tm, c), lambda bi, i: (bi, i, 0)),
                  pl.BlockSpec((1, 16, c), lambda bi, i: (bi, jnp.maximum(i * hb - 1, 0), 0)),
                  pl.BlockSpec((1, 16, c), lambda bi, i: (bi, 0, 0)),
                  pl.BlockSpec((len(POOL_WINDOWS), POOL_GROUP, POOL_GROUP), lambda bi, i: (0, 0, 0)),
                  pl.BlockSpec((1, c), lambda bi, i: (0, 0))],
        out_specs=pl.BlockSpec((1, tm, c), lambda bi, i: (bi, i, 0)),
        out_shape=jax.ShapeDtypeStruct((b, t, c), F32),
        compiler_params=_cparams(("arbitrary", "arbitrary")),
    )(p, p, prev16, pool_w, pool_scale.reshape(1, c))


def _rms(x, g):
    return x * lax.rsqrt(jnp.mean(x * x, axis=-1, keepdims=True) + EPS) * g


def _outproj_body(oa_ref, op_ref, x_ref, gate_ref, g_ref, wa_ref, wp_ref, o_ref):
    mix = (jnp.dot(oa_ref[0].astype(BF16), wa_ref[...], preferred_element_type=F32)
           + jnp.dot(op_ref[0].astype(BF16), wp_ref[...], preferred_element_type=F32))
    o_ref[0] = x_ref[0] + gate_ref[0] * _rms(mix, g_ref[...])


def _outproj(oa, op, x, gate, g, wa, wp, tm):
    bx, t, d = x.shape
    per_row = gate.shape[1] != 1
    ms = _mod_specs(t, tm, d, per_row)
    return pl.pallas_call(
        _outproj_body,
        grid=(bx, t // tm),
        in_specs=[pl.BlockSpec((1, tm, oa.shape[2]), lambda b, i: (b, i, 0)),
                  pl.BlockSpec((1, tm, op.shape[2]), lambda b, i: (b, i, 0)),
                  pl.BlockSpec((1, tm, d), lambda b, i: (b, i, 0)), ms,
                  pl.BlockSpec((1, d), lambda b, i: (0, 0)),
                  pl.BlockSpec(wa.shape, lambda b, i: (0, 0)),
                  pl.BlockSpec(wp.shape, lambda b, i: (0, 0))],
        out_specs=pl.BlockSpec((1, tm, d), lambda b, i: (b, i, 0)),
        out_shape=jax.ShapeDtypeStruct((bx, t, d), F32),
        compiler_params=_cparams(("arbitrary", "arbitrary")),
    )(oa, op, x, gate, g.reshape(1, d), wa, wp)


def _final_body(f_ref, x_ref, gate_ref, g_ref, o_ref):
    o_ref[0] = x_ref[0] + gate_ref[0] * _rms(f_ref[0], g_ref[...])


def _final(f, x, gate, g, tm):
    bx, t, d = x.shape
    per_row = gate.shape[1] != 1
    ms = _mod_specs(t, tm, d, per_row)
    xs = pl.BlockSpec((1, tm, d), lambda b, i: (b, i, 0))
    return pl.pallas_call(
        _final_body,
        grid=(bx, t // tm),
        in_specs=[xs, xs, ms, pl.BlockSpec((1, d), lambda b, i: (0, 0))],
        out_specs=xs,
        out_shape=jax.ShapeDtypeStruct((bx, t, d), F32),
        compiler_params=_cparams(("arbitrary", "arbitrary")),
    )(f, x, gate, g.reshape(1, d))


def _topk_rows(s, k):
    n = s.shape[0]
    row = lax.broadcasted_iota(I32, s.shape, 0)
    krow = lax.broadcasted_iota(I32, (k, s.shape[1]), 0)
    vals = jnp.zeros((k, s.shape[1]), F32)
    idxs = jnp.zeros((k, s.shape[1]), I32)
    for it in range(k):
        mx = jnp.max(s, axis=0, keepdims=True)
        ix = jnp.min(jnp.where(s == mx, row, n), axis=0, keepdims=True)
        vals = jnp.where(krow == it, mx, vals)
        idxs = jnp.where(krow == it, ix, idxs)
        s = jnp.where(row == ix, REMOVED, s)
    return vals, idxs


def _pick_rows(table, ix):
    k = table.shape[0]
    out = jnp.zeros(ix.shape, table.dtype)
    for a in range(k):
        out = jnp.where(ix == a, table[a:a + 1, :], out)
    return out


def _peer_topk_body(q_ref, keys_ref, idx_ref, gate_ref):
    kk = PEER_TOPK

    def head(h, carry):
        s1 = _dot_nt(keys_ref[2 * h], q_ref[2 * h])
        s2 = _dot_nt(keys_ref[2 * h + 1], q_ref[2 * h + 1])
        v1, i1 = _topk_rows(s1, kk)
        v2, i2 = _topk_rows(s2, kk)
        cand = jnp.concatenate([v1[a:a + 1, :] + v2 for a in range(kk)], axis=0)
        best, flat = _topk_rows(cand, kk)
        e = _pick_rows(i1, flat // kk) * PEER_KEYS + _pick_rows(i2, flat % kk)
        ex = jnp.exp(best - jnp.max(best, axis=0, keepdims=True))
        gate = ex / jnp.sum(ex, axis=0, keepdims=True)
        off = pl.multiple_of(h * kk, kk)
        idx_ref[pl.ds(off, kk), :] = e
        gate_ref[pl.ds(off, kk), :] = gate
        return carry

    lax.fori_loop(0, PEER_HEADS, head, 0)


def _peer_topk(q16, keys16, tt):
    nq, n, _ = q16.shape
    rows = PEER_HEADS * PEER_TOPK
    return pl.pallas_call(
        _peer_topk_body,
        grid=(n // tt,),
        in_specs=[pl.BlockSpec((nq, tt, LANE), lambda i: (0, i, 0)),
                  pl.BlockSpec(keys16.shape, lambda i: (0, 0, 0))],
        out_specs=[pl.BlockSpec((rows, tt), lambda i: (0, i)),
                   pl.BlockSpec((rows, tt), lambda i: (0, i))],
        out_shape=[jax.ShapeDtypeStruct((rows, n), I32),
                   jax.ShapeDtypeStruct((rows, n), F32)],
        compiler_params=_cparams(("arbitrary",)),
    )(q16, keys16)


def _expert_row(tab_ref, e):
    w = tab_ref[e >> 1]
    sh = ((1 - (e & 1)) * 16).astype(U32)
    return pltpu.bitcast((w << sh) & jnp.uint32(0xFFFF0000), F32)


def _peer_u_body(idx_ref, x_ref, gate_ref, tab_ref, act_ref, *, tt):
    rows = PEER_HEADS * PEER_TOPK
    ngrp = rows // 8
    lane = lax.broadcasted_iota(I32, (8, tt), 1)

    def token(t, accs):
        x = x_ref[t]
        new = []
        for gidx in range(ngrp):
            parts = []
            for r in range(8):
                e = idx_ref[gidx * 8 + r, t]
                parts.append(jnp.sum(_expert_row(tab_ref, e) * x, axis=0, keepdims=True))
            q = jnp.concatenate(parts, axis=0)
            dcol = jnp.sum(q, axis=1, keepdims=True)
            new.append(jnp.where(lane == t, dcol, accs[gidx]))
        return tuple(new)

    accs = lax.fori_loop(0, tt, token, tuple(jnp.zeros((8, tt), F32) for _ in range(ngrp)))
    d = jnp.concatenate(accs, axis=0)
    act_ref[...] = jax.nn.gelu(d, approximate=True) * gate_ref[...]


def _peer_u(idx, x3, gate, tab, tt):
    rows, n = idx.shape
    body = functools.partial(_peer_u_body, tt=tt)
    return pl.pallas_call(
        body,
        grid=(n // tt,),
        in_specs=[pl.BlockSpec((rows, tt), lambda i: (0, i), memory_space=pltpu.SMEM),
                  pl.BlockSpec((tt, 8, LANE), lambda i: (i, 0, 0)),
                  pl.BlockSpec((rows, tt), lambda i: (0, i)),
                  pl.BlockSpec(tab.shape, lambda i: (0, 0, 0), pipeline_mode=pl.Buffered(1))],
        out_specs=pl.BlockSpec((rows, tt), lambda i: (0, i)),
        out_shape=jax.ShapeDtypeStruct((rows, n), F32),
        compiler_params=_cparams(("arbitrary",)),
    )(idx, x3, gate, tab)


def _peer_v_body(idx_ref, act_ref, tab_ref, f_ref, *, tt):
    rows = PEER_HEADS * PEER_TOPK
    nacc = 4

    def token(t, carry):
        accs = [jnp.zeros((8, LANE), F32) for _ in range(nacc)]
        for r in range(rows):
            e = idx_ref[r, t]
            a = act_ref[r, t]
            accs[r % nacc] = accs[r % nacc] + a * _expert_row(tab_ref, e)
        f_ref[t] = (accs[0] + accs[1]) + (accs[2] + accs[3])
        return carry

    lax.fori_loop(0, tt, token, 0)


def _peer_v(idx, act, tab, tt):
    rows, n = idx.shape
    body = functools.partial(_peer_v_body, tt=tt)
    return pl.pallas_call(
        body,
        grid=(n // tt,),
        in_specs=[pl.BlockSpec((rows, tt), lambda i: (0, i), memory_space=pltpu.SMEM),
                  pl.BlockSpec((rows, tt), lambda i: (0, i), memory_space=pltpu.SMEM),
                  pl.BlockSpec(tab.shape, lambda i: (0, 0, 0), pipeline_mode=pl.Buffered(1))],
        out_specs=pl.BlockSpec((tt, 8, LANE), lambda i: (i, 0, 0)),
        out_shape=jax.ShapeDtypeStruct((n, 8, LANE), F32),
        compiler_params=_cparams(("arbitrary",)),
    )(idx, act, tab)


def _pack_table(tab):
    n, d = tab.shape
    bits = lax.bitcast_convert_type(tab.astype(BF16), jnp.uint16).astype(U32).reshape(n // 2, 2, d)
    return (bits[:, 0] | (bits[:, 1] << 16)).reshape(n // 2, d // LANE, LANE)


def _peer(x1, sc, sh, g, wq, keys16, tab_u, tab_v, tm):
    bx, t, d = x1.shape
    n = bx * t
    q16, h2 = _peerq(x1, sc, sh, g, wq, tm)
    tt = min(128, n)
    idx, gate = _peer_topk(q16.reshape(2 * PEER_HEADS, n, LANE), keys16, tt)
    act = _peer_u(idx, h2.reshape(n, d // LANE, LANE), gate, tab_u, tt)
    f = _peer_v(idx, act, tab_v, tt)
    return f.reshape(bx, t, d)


def _pack_w_in(w_in):
    d = w_in.shape[0]
    scale = HEAD_DIM ** -0.5
    cols = []
    for hd in range(N_HEADS):
        blk = w_in[:, hd * HEAD_DIM:(hd + 1) * HEAD_DIM] * scale
        z = jnp.zeros((d, HEAD_DIM), w_in.dtype)
        cols += [blk, z] if hd // GQA == 0 else [z, blk]
    o0 = N_HEADS * HEAD_DIM
    o1 = o0 + 6 * N_KV * HEAD_DIM
    o2 = o1 + 3 * N_HEADS
    cols.append(w_in[:, o0:o0 + 512])
    cols.append(w_in[:, o0 + 512:o1])
    cols.append(w_in[:, o2:])
    cols.append(w_in[:, o1:o2])
    cols.append(jnp.zeros((d, LANE - 3 * N_HEADS), w_in.dtype))
    return jnp.concatenate(cols, axis=1).astype(BF16)


def _pack_w_out(w_out):
    d = w_out.shape[1]
    rows = []
    for hd in range(N_HEADS):
        blk = w_out[hd * HEAD_DIM:(hd + 1) * HEAD_DIM]
        z = jnp.zeros((HEAD_DIM, d), w_out.dtype)
        rows += [blk, z] if hd // GQA == 0 else [z, blk]
    wa = jnp.concatenate(rows, axis=0).astype(BF16)
    wp = w_out[N_HEADS * HEAD_DIM:].astype(BF16)
    return wa, wp


def _pack_compress(alpha, phi):
    a = jnp.concatenate([alpha[0], alpha[0], alpha[1], alpha[1]], axis=1)
    z = jnp.zeros((HEAD_DIM, HEAD_DIM), phi.dtype)
    blocks = [phi[0], phi[0], phi[1], phi[1]]
    p = jnp.concatenate([jnp.concatenate([blocks[r] if c == r else z for c in range(4)], axis=1)
                         for r in range(4)], axis=0)
    return a, p


def kernel(x_prompt, x_sample, cache_kv, cache_win, state_pool, page_table, c_prompt, c_sample,
           w_ada, b_ada, norm_g, w_in, w_out, cmp_alpha, cmp_phi, rel_bias, pool_w, pool_scale,
           peer_wq, peer_keys, peer_u, peer_v):
    depth = w_ada.shape[0]
    assert depth == 1 and x_sample.shape[1] == 1
    l = 0
    b, s, d = x_prompt.shape
    db = x_sample.shape[0]
    n_pool, page = cache_kv.shape[1], cache_kv.shape[2]
    n_pages = page_table.shape[1]
    past = n_pages * page
    wbuf = cache_win.shape[2]
    tq = WINDOW // 2

    w_in_p = _pack_w_in(w_in[l])
    wa, wp = _pack_w_out(w_out[l])
    a_cmp, phi_cmp = _pack_compress(cmp_alpha[l], cmp_phi[l])
    wq = peer_wq[l].astype(BF16)
    keys16 = peer_keys[l].reshape(2 * PEER_HEADS, PEER_KEYS, LANE).astype(BF16)
    tab_u = _pack_table(peer_u[l])
    tab_v = _pack_table(peer_v[l])
    g = norm_g[l]

    c_all = jnp.concatenate([c_prompt, c_sample], axis=0)
    mod = _mod(c_all, w_ada[l], b_ada[l]).reshape(b + db, 6, d)
    mod_p = [mod[:b, k][:, None, :] for k in range(6)]
    mod_s = [mod[b:, k][None, :, :] for k in range(6)]

    tm = min(512, s)
    qpad, kv4, selwin, win, pool_in, gates = _inproj(x_prompt, mod_p[1], mod_p[0], g[0], w_in_p, tm)
    kcvc = _compress(kv4, a_cmp, phi_cmp, min(1024, s))
    o_attn = _nsa_prompt(qpad, selwin, kcvc.reshape(b, s // SEL_BLOCK, 512), gates, rel_bias, tq)
    prev0 = jnp.zeros((b, 16, POOL_DIM), F32)
    o_pool = _pool(pool_in, prev0, pool_w[l], pool_scale[l], min(1024, s), 0)
    x1 = _outproj(o_attn, o_pool, x_prompt, mod_p[2], g[1], wa, wp, tm)
    f = _peer(x1, mod_p[4], mod_p[3], g[2], wq, keys16, tab_u, tab_v, tm)
    y_prompt = _final(f, x1, mod_p[5], g[3], tm)

    kv_prompt = kv4.reshape(1, b, s, 4, N_KV, HEAD_DIM)
    win_prompt = win[:, s - min(WINDOW, s):].reshape(1, b, min(WINDOW, s), 2, N_KV, HEAD_DIM)
    pool_prompt = pool_in[:, s - POOL_STATE:][None]

    xs = x_sample.reshape(1, db, d)
    qpad_s, kv4_s, _, win_s, pool_s, gates_s = _inproj(xs, mod_s[1], mod_s[0], g[0], w_in_p, db)
    cache3 = cache_kv[l].reshape(n_pool, page, 512)
    kcvc_s = _compress_paged(cache3, page_table, a_cmp, phi_cmp)
    q8 = qpad_s.reshape(db, N_HEADS, LANE)
    kv4n = kv4_s.reshape(db, 1, 512)
    winn = win_s.reshape(db, 1, 256)
    gat = gates_s.reshape(db, 1, LANE)
    cw3 = cache_win[l].reshape(db, wbuf, 256)
    rbt = rel_bias.T
    ocw, sel_idx = _nsa_s1(q8, kcvc_s.reshape(db, past // SEL_BLOCK, 512), cw3, kv4n, winn, gat, rbt, past)
    sel_flat = sel_idx[:, :N_KV, :N_SEL - 1].reshape(-1)
    cache_blk = cache3.reshape(n_pool * (page // SEL_BLOCK), SEL_BLOCK, 512)
    o_attn_s = _nsa_s2(sel_flat, page_table, q8, cache_blk, kv4n, gat, rbt, ocw, past, page)
    ext = jnp.concatenate([jnp.zeros((db, 1, POOL_DIM), F32), state_pool[l], pool_s.reshape(db, 1, POOL_DIM)], axis=1)
    o_pool_s = _pool(ext[:, 1:], jnp.zeros((db, 16, POOL_DIM), F32), pool_w[l], pool_scale[l], 16,
                     past - POOL_STATE)[:, POOL_STATE:]
    x1s = _outproj(o_attn_s.reshape(1, db, N_HEADS * LANE).astype(BF16), o_pool_s.reshape(1, db, POOL_DIM),
                   xs, mod_s[2], g[1], wa, wp, db)
    fs = _peer(x1s, mod_s[4], mod_s[3], g[2], wq, keys16, tab_u, tab_v, db)
    y_sample = _final(fs, x1s, mod_s[5], g[3], db).reshape(db, 1, d)

    kv_sample = kv4_s.reshape(1, db, 1, 4, N_KV, HEAD_DIM)
    win_sample = jnp.concatenate([cache_win[l][:, 1:], win_s.reshape(db, 1, 2, N_KV, HEAD_DIM)], axis=1)[None]
    pool_sample = ext[:, 2:][None]
    return (y_prompt, y_sample, kv_prompt, kv_sample, win_prompt, win_sample, pool_prompt, pool_sample)
```

```python
import functools
import math

import numpy as np
import jax
import jax.numpy as jnp
from jax import lax
from jax.experimental import pallas as pl
from jax.experimental.pallas import tpu as pltpu

F32, BF16, I32, U32 = jnp.float32, jnp.bfloat16, jnp.int32, jnp.uint32

D_MODEL = 1024
N_HEADS = 8
HEAD_DIM = 64
N_KV = 2
GQA = 4
CMP_BLOCK = 32
SEL_BLOCK = 64
N_SEL = 16
WINDOW = 512
N_BUCKETS = 32
REL_MAX_DIST = 128
POOL_WINDOWS = (2, 4, 8, 16)
POOL_DIM = 512
POOL_GROUP = 128
POOL_STATE = 15
PEER_HEADS = 8
PEER_KEYS = 128
PEER_TOPK = 16
EPS = 1e-6
LANE = 128
NEG = -1e30
BIG = 1e30
REMOVED = -3e38
VMEM_LIMIT = 56 * 1024 * 1024

C_Q, C_KV4, C_WIN, C_POOL, C_GATE, C_END = 0, 1024, 1536, 1792, 2304, 2432
C_SELWIN = 1280


def _cparams(sem):
    return pltpu.CompilerParams(dimension_semantics=sem, vmem_limit_bytes=VMEM_LIMIT)


def _bucket_thresholds():
    n = np.arange(0, 4 * REL_MAX_DIST)
    exact = N_BUCKETS // 2
    ratio = np.log(np.maximum(n, exact).astype(np.float32) / np.float32(exact)) / np.float32(
        math.log(REL_MAX_DIST / exact))
    large = np.minimum(exact + (ratio * np.float32(N_BUCKETS - exact)).astype(np.int32), N_BUCKETS - 1)
    b = np.where(n < exact, n, large)
    return [int(np.argmax(b >= j)) for j in range(N_BUCKETS)]


_THR = _bucket_thresholds()
FAR_DIST = _THR[N_BUCKETS - 1]


def _bias_from_dist(d, rb):
    val = jnp.full(d.shape, rb(0), F32)
    for j in range(1, N_BUCKETS):
        if _THR[j] == _THR[j - 1] and j > 1:
            continue
        jj = j
        while jj + 1 < N_BUCKETS and _THR[jj + 1] == _THR[j]:
            jj += 1
        val = jnp.where(d >= _THR[j], rb(jj), val)
    return val


def _dot_nt(a, b):
    return lax.dot_general(a, b, (((1,), (1,)), ((), ())), preferred_element_type=F32)


def _mod_body(c_ref, w_ref, b_ref, o_ref):
    c = c_ref[...]
    a = (c * jax.nn.sigmoid(c)).astype(BF16)
    o_ref[...] = jnp.dot(a, w_ref[...].astype(BF16), preferred_element_type=F32) + b_ref[...]


def _mod(c, w_ada, b_ada):
    r, d = c.shape
    n = w_ada.shape[1]
    tn = 1536
    return pl.pallas_call(
        _mod_body,
        grid=(n // tn,),
        in_specs=[pl.BlockSpec((r, d), lambda j: (0, 0)),
                  pl.BlockSpec((d, tn), lambda j: (0, j)),
                  pl.BlockSpec((1, tn), lambda j: (0, j))],
        out_specs=pl.BlockSpec((r, tn), lambda j: (0, j)),
        out_shape=jax.ShapeDtypeStruct((r, n), F32),
        compiler_params=_cparams(("arbitrary",)),
    )(c, w_ada, b_ada.reshape(1, n))


def _norm_mod(x, g, sc, sh):
    ms = jnp.mean(x * x, axis=-1, keepdims=True)
    h = x * lax.rsqrt(ms + EPS) * g
    return h * (1.0 + sc) + sh


def _inproj_body(x_ref, sc_ref, sh_ref, g_ref, w_ref, q_ref, kv_ref, sw_ref, win_ref, pool_ref, gate_ref):
    h = _norm_mod(x_ref[0], g_ref[...], sc_ref[0], sh_ref[0])
    u = jnp.dot(h.astype(BF16), w_ref[...], preferred_element_type=F32)
    q_ref[0] = u[:, C_Q:C_KV4].astype(BF16)
    kv_ref[0] = u[:, C_KV4:C_WIN]
    sw_ref[0] = u[:, C_SELWIN:C_POOL].astype(BF16)
    win_ref[0] = u[:, C_WIN:C_POOL]
    pool_ref[0] = u[:, C_POOL:C_GATE]
    gate_ref[0] = jax.nn.sigmoid(u[:, C_GATE:C_END])


def _mod_specs(t, tm, d, per_row):
    if per_row:
        return pl.BlockSpec((1, tm, d), lambda b, i: (b, i, 0))
    return pl.BlockSpec((1, 1, d), lambda b, i: (b, 0, 0))


def _inproj(x, sc, sh, g, w, tm):
    bx, t, d = x.shape
    per_row = sc.shape[1] != 1
    ms = _mod_specs(t, tm, d, per_row)
    widths = (1024, 512, 512, 256, 512, 128)
    dtypes = (BF16, F32, BF16, F32, F32, F32)
    return pl.pallas_call(
        _inproj_body,
        grid=(bx, t // tm),
        in_specs=[pl.BlockSpec((1, tm, d), lambda b, i: (b, i, 0)), ms, ms,
                  pl.BlockSpec((1, d), lambda b, i: (0, 0)),
                  pl.BlockSpec((d, C_END), lambda b, i: (0, 0))],
        out_specs=[pl.BlockSpec((1, tm, wd), lambda b, i: (b, i, 0)) for wd in widths],
        out_shape=[jax.ShapeDtypeStruct((bx, t, wd), dt) for wd, dt in zip(widths, dtypes)],
        compiler_params=_cparams(("arbitrary", "arbitrary")),
    )(x, sc, sh, g.reshape(1, d), w)


def _peerq_body(x_ref, sc_ref, sh_ref, g_ref, w_ref, q_ref, h_ref):
    h = _norm_mod(x_ref[0], g_ref[...], sc_ref[0], sh_ref[0])
    for c in range(h.shape[1] // LANE):
        h_ref[0, :, c, :] = h[:, c * LANE:(c + 1) * LANE]
    u = jnp.dot(h.astype(BF16), w_ref[...], preferred_element_type=F32)
    for j in range(2 * PEER_HEADS):
        q_ref[j, 0] = u[:, j * LANE:(j + 1) * LANE].astype(BF16)


def _peerq(x, sc, sh, g, w, tm):
    bx, t, d = x.shape
    per_row = sc.shape[1] != 1
    ms = _mod_specs(t, tm, d, per_row)
    nq = 2 * PEER_HEADS
    return pl.pallas_call(
        _peerq_body,
        grid=(bx, t // tm),
        in_specs=[pl.BlockSpec((1, tm, d), lambda b, i: (b, i, 0)), ms, ms,
                  pl.BlockSpec((1, d), lambda b, i: (0, 0)),
                  pl.BlockSpec((d, nq * LANE), lambda b, i: (0, 0))],
        out_specs=[pl.BlockSpec((nq, 1, tm, LANE), lambda b, i: (0, b, i, 0)),
                   pl.BlockSpec((1, tm, d // LANE, LANE), lambda b, i: (b, i, 0, 0))],
        out_shape=[jax.ShapeDtypeStruct((nq, bx, t, LANE), BF16),
                   jax.ShapeDtypeStruct((bx, t, d // LANE, LANE), F32)],
        compiler_params=_cparams(("arbitrary", "arbitrary")),
    )(x, sc, sh, g.reshape(1, d), w)


def _compress_rows(rows, a, phi):
    n = rows.shape[0] // CMP_BLOCK
    w = jnp.sum(rows.reshape(n, CMP_BLOCK, rows.shape[1]) * a[None], axis=1)
    return jnp.dot(w, phi, preferred_element_type=F32, precision=lax.Precision.HIGHEST)


def _compress_body(x_ref, a_ref, phi_ref, o_ref):
    o_ref[0] = _compress_rows(x_ref[0], a_ref[...], phi_ref[...])


def _compress(kv4, a, phi, tm):
    b, s, _ = kv4.shape
    return pl.pallas_call(
        _compress_body,
        grid=(b, s // tm),
        in_specs=[pl.BlockSpec((1, tm, 256), lambda bi, i: (bi, i, 0)),
                  pl.BlockSpec((CMP_BLOCK, 256), lambda bi, i: (0, 0)),
                  pl.BlockSpec((256, 256), lambda bi, i: (0, 0))],
        out_specs=pl.BlockSpec((1, tm // CMP_BLOCK, 256), lambda bi, i: (bi, i, 0)),
        out_shape=jax.ShapeDtypeStruct((b, s // CMP_BLOCK, 256), F32),
        compiler_params=_cparams(("arbitrary", "arbitrary")),
    )(kv4, a, phi)


PAGES_PER_STEP = 4


def _wsum_paged_body(pt_ref, *refs):
    page_refs, a_ref, o_ref = refs[:PAGES_PER_STEP], refs[PAGES_PER_STEP], refs[PAGES_PER_STEP + 1]
    a = a_ref[...]
    for p, p_ref in enumerate(page_refs):
        x = p_ref[0, 0]
        per = x.shape[0] // CMP_BLOCK
        w = jnp.sum(x.reshape(per, CMP_BLOCK, 2, N_KV, HEAD_DIM) * a[None], axis=1)
        o_ref[0, p * per:(p + 1) * per] = w


def _wsum_paged(cache_kv_l, page_table, alpha):
    db, n_pages = page_table.shape
    page = cache_kv_l.shape[2]
    per = page // CMP_BLOCK
    a4 = jnp.transpose(alpha, (1, 0, 2))[:, :, None, :]

    def page_spec(r):
        return pl.BlockSpec((1, 1, page, 2, N_KV, HEAD_DIM),
                            lambda b, i, pt: (0, pt[b, PAGES_PER_STEP * i + r], 0, 0, 0, 0))

    grid_spec = pltpu.PrefetchScalarGridSpec(
        num_scalar_prefetch=1,
        grid=(db, n_pages // PAGES_PER_STEP),
        in_specs=[page_spec(r) for r in range(PAGES_PER_STEP)]
        + [pl.BlockSpec((CMP_BLOCK, 2, 1, HEAD_DIM), lambda b, i, pt: (0, 0, 0, 0))],
        out_specs=pl.BlockSpec((1, PAGES_PER_STEP * per, 2, N_KV, HEAD_DIM), lambda b, i, pt: (b, i, 0, 0, 0)),
    )
    return pl.pallas_call(
        _wsum_paged_body,
        grid_spec=grid_spec,
        out_shape=jax.ShapeDtypeStruct((db, n_pages * per, 2, N_KV, HEAD_DIM), F32),
        compiler_params=_cparams(("arbitrary", "arbitrary")),
    )(page_table, *([cache_kv_l] * PAGES_PER_STEP), a4)


def _select_blocks(score, n_pick):
    lane = lax.broadcasted_iota(I32, score.shape, 1)
    nl = score.shape[1]
    sel = jnp.zeros(score.shape, jnp.bool_)
    idxs = []
    for _ in range(n_pick):
        mx = jnp.max(score, axis=1, keepdims=True)
        idx = jnp.min(jnp.where(score == mx, lane, nl), axis=1, keepdims=True)
        hit = lane == idx
        sel = sel | hit
        score = jnp.where(hit, REMOVED, score)
        idxs.append(idx)
    return sel, idxs


def _flash_update(slot, qh, kc, vc, bias, mask, acc_sc, m_sc):
    s = _dot_nt(qh, kc) + bias
    if mask is not None:
        s = jnp.where(mask, s, NEG)
    m_old = m_sc[slot]
    m_new = jnp.maximum(m_old, jnp.max(s, axis=1, keepdims=True))
    alpha = jnp.exp(m_old - m_new)
    p = jnp.exp(s - m_new)
    acc_sc[slot] = alpha * acc_sc[slot] + jnp.dot(p.astype(BF16), vc, preferred_element_type=F32)
    m_sc[slot] = m_new


def _nsa_body(rb_ref, q_ref, sw_ref, kc_ref, g_ref, o_ref,
              bdiag_sc, bcmp_sc, acc_sc, m_sc, osum_sc, *, seq, tq):
    i = pl.program_id(1)
    q0 = i * tq
    nb = seq // SEL_BLOCK
    per = tq // SEL_BLOCK

    @pl.when((pl.program_id(0) == 0) & (i == 0))
    def _build_bias_tiles():
        t = lax.broadcasted_iota(I32, (tq, tq), 0)
        k = lax.broadcasted_iota(I32, (tq, tq), 1)
        tc = lax.broadcasted_iota(I32, (tq, nb), 0)
        m = lax.broadcasted_iota(I32, (tq, nb), 1)
        r = jnp.where(m < per, m, m - nb)
        for hd in range(N_HEADS):
            rb = lambda j, hd=hd: rb_ref[j, hd]
            for dl in range(2):
                bdiag_sc[hd, dl] = _bias_from_dist(t - k + dl * tq, rb)
            for par in range(2):
                bcmp_sc[hd, par] = _bias_from_dist(
                    tc - (SEL_BLOCK * r + (par + 1) * CMP_BLOCK - 1), rb)

    kc2 = kc_ref[0]
    kce, vce = kc2[:, 0:128].astype(BF16), kc2[:, 128:256].astype(BF16)
    kco, vco = kc2[:, 256:384].astype(BF16), kc2[:, 384:512].astype(BF16)
    qpos = q0 + lax.broadcasted_iota(I32, (tq, nb), 0)
    jl = lax.broadcasted_iota(I32, (tq, nb), 1)
    valid_e = qpos >= SEL_BLOCK * jl + CMP_BLOCK - 1
    valid_o = qpos >= SEL_BLOCK * jl + 2 * CMP_BLOCK - 1
    cur = qpos // SEL_BLOCK
    forced = (jl == 0) | (jl == cur) | (jl == cur - 1)
    causal_blk = jl * SEL_BLOCK <= qpos
    shift = (i * per) % nb
    tt = lax.broadcasted_iota(I32, (tq, tq), 0)
    kk = lax.broadcasted_iota(I32, (tq, tq), 1)
    causal = tt >= kk
    gates = g_ref[0]
    ej = lax.broadcasted_iota(I32, (nb, tq), 0)
    ek = lax.broadcasted_iota(I32, (nb, tq), 1) // SEL_BLOCK

    def reset_state():
        m_sc[...] = jnp.full(m_sc.shape, NEG, F32)
        acc_sc[...] = jnp.zeros(acc_sc.shape, F32)

    def qhead(hd):
        return q_ref[0, :, hd * LANE:(hd + 1) * LANE]

    vlane = lax.broadcasted_iota(I32, (tq, LANE), 1) // HEAD_DIM

    def kv_chunk(c, col, kvh):
        start = pl.multiple_of(c * tq, tq)
        v = sw_ref[0, pl.ds(start, tq), col + LANE:col + 2 * LANE]
        return (sw_ref[0, pl.ds(start, tq), col:col + LANE],
                jnp.where(vlane == kvh, v, jnp.ones_like(v)))

    def finish(kvh, branch):
        lcol = (1 - kvh) * HEAD_DIM
        for g in range(GQA):
            hd = kvh * GQA + g
            acc = acc_sc[g]
            o = acc / acc[:, lcol:lcol + 1]
            gc = gates[:, branch * N_HEADS + hd:branch * N_HEADS + hd + 1]
            osum_sc[:, hd * LANE:(hd + 1) * LANE] += gc * o

    for kvh in range(N_KV):
        imp = jnp.zeros((tq, nb), F32)
        for g in range(GQA):
            hd = kvh * GQA + g
            qh = qhead(hd)
            s_e = _dot_nt(qh, kce) + pltpu.roll(bcmp_sc[hd, 0], shift, 1)
            s_o = _dot_nt(qh, kco) + pltpu.roll(bcmp_sc[hd, 1], shift, 1)
            s_e = jnp.where(valid_e, s_e, NEG)
            s_o = jnp.where(valid_o, s_o, NEG)
            mx = jnp.maximum(jnp.max(s_e, axis=1, keepdims=True), jnp.max(s_o, axis=1, keepdims=True))
            e_e = jnp.where(valid_e, jnp.exp(s_e - mx), 0.0)
            e_o = jnp.where(valid_o, jnp.exp(s_o - mx), 0.0)
            den = jnp.maximum(jnp.sum(e_e, axis=1, keepdims=True) + jnp.sum(e_o, axis=1, keepdims=True), 1e-30)
            p_e, p_o = e_e / den, e_o / den
            imp = imp + (p_e + p_o)
            o_c = (jnp.dot(p_e.astype(BF16), vce, preferred_element_type=F32)
                   + jnp.dot(p_o.astype(BF16), vco, preferred_element_type=F32))
            gc = gates[:, hd:hd + 1]
            osum_sc[:, hd * LANE:(hd + 1) * LANE] = gc * o_c

        score = jnp.where(forced, BIG, imp)
        score = jnp.where(causal_blk, score, NEG)
        selm, _ = _select_blocks(score, min(N_SEL, nb))
        selm = jnp.where(selm, 1.0, 0.0).astype(BF16)

        def sel_mask(c):
            expand = jnp.where(ej == ek + c * per, 1.0, 0.0).astype(BF16)
            return jnp.dot(selm, expand, preferred_element_type=F32) > 0.5

        reset_state()

        def far_chunk(c, carry):
            kc, vc = kv_chunk(c, 0, kvh)
            mask = sel_mask(c)
            for g in range(GQA):
                hd = kvh * GQA + g
                _flash_update(g, qhead(hd), kc, vc, rb_ref[N_BUCKETS - 1, hd], mask, acc_sc, m_sc)
            return carry

        lax.fori_loop(0, jnp.maximum(i - 1, 0), far_chunk, 0)

        @pl.when(i >= 1)
        def _prev_chunk():
            kc, vc = kv_chunk(i - 1, 0, kvh)
            mask = sel_mask(i - 1)
            for g in range(GQA):
                hd = kvh * GQA + g
                _flash_update(g, qhead(hd), kc, vc, bdiag_sc[hd, 1], mask, acc_sc, m_sc)

        kc, vc = kv_chunk(i, 0, kvh)
        mask = sel_mask(i) & causal
        for g in range(GQA):
            hd = kvh * GQA + g
            _flash_update(g, qhead(hd), kc, vc, bdiag_sc[hd, 0], mask, acc_sc, m_sc)
        finish(kvh, 1)

        reset_state()

        @pl.when(i >= 2)
        def _win_far():
            kc, vc = kv_chunk(i - 2, 2 * LANE, kvh)
            for g in range(GQA):
                hd = kvh * GQA + g
                _flash_update(g, qhead(hd), kc, vc, rb_ref[N_BUCKETS - 1, hd], kk > tt, acc_sc, m_sc)

        @pl.when(i >= 1)
        def _win_prev():
            kc, vc = kv_chunk(i - 1, 2 * LANE, kvh)
            for g in range(GQA):
                hd = kvh * GQA + g
                _flash_update(g, qhead(hd), kc, vc, bdiag_sc[hd, 1], None, acc_sc, m_sc)

        kc, vc = kv_chunk(i, 2 * LANE, kvh)
        for g in range(GQA):
            hd = kvh * GQA + g
            _flash_update(g, qhead(hd), kc, vc, bdiag_sc[hd, 0], causal, acc_sc, m_sc)
        finish(kvh, 2)

    o_ref[0] = osum_sc[...].astype(BF16)


def _nsa_prompt(qpad, selwin, kcvc2, gates, rel_bias, tq):
    b, s, _ = qpad.shape
    nb = s // SEL_BLOCK
    assert WINDOW == 2 * tq and tq + 1 >= FAR_DIST and s % tq == 0
    body = functools.partial(_nsa_body, seq=s, tq=tq)
    return pl.pallas_call(
        body,
        grid=(b, s // tq),
        in_specs=[pl.BlockSpec(memory_space=pltpu.SMEM),
                  pl.BlockSpec((1, tq, N_HEADS * LANE), lambda bi, i: (bi, i, 0)),
                  pl.BlockSpec((1, s, 512), lambda bi, i: (bi, 0, 0)),
                  pl.BlockSpec((1, nb, 512), lambda bi, i: (bi, 0, 0)),
                  pl.BlockSpec((1, tq, LANE), lambda bi, i: (bi, i, 0))],
        out_specs=pl.BlockSpec((1, tq, N_HEADS * LANE), lambda bi, i: (bi, i, 0)),
        out_shape=jax.ShapeDtypeStruct((b, s, N_HEADS * LANE), BF16),
        scratch_shapes=[pltpu.VMEM((N_HEADS, 2, tq, tq), F32),
                        pltpu.VMEM((N_HEADS, 2, tq, nb), F32),
                        pltpu.VMEM((GQA, tq, LANE), F32),
                        pltpu.VMEM((GQA, tq, 1), F32),
                        pltpu.VMEM((tq, N_HEADS * LANE), F32)],
        compiler_params=_cparams(("arbitrary", "arbitrary")),
    )(rel_bias, qpad, selwin, kcvc2, gates)


def _head_rows(x):
    sub = lax.broadcasted_iota(I32, (N_HEADS, LANE), 0)
    lane = lax.broadcasted_iota(I32, (N_HEADS, LANE), 1)
    return jnp.sum(jnp.where(sub == lane, jnp.broadcast_to(x, (N_HEADS, LANE)), 0.0), axis=1, keepdims=True)


def _nsa_s1_body(q_ref, kc_ref, cw_ref, wn_ref, g_ref, rbt_ref, phik_ref, phiv_ref, ocw_ref, idx_ref, *, past):
    nb = kc_ref.shape[1]
    wbuf = cw_ref.shape[1]
    q = q_ref[0]
    rbt = rbt_ref[...]
    rb = lambda j: rbt[:, j:j + 1]
    gates = g_ref[0]
    g_c = _head_rows(gates[:, 0:LANE])
    g_w = _head_rows(pltpu.roll(gates, LANE - 2 * N_HEADS, 1))

    ws = kc_ref[0]
    hi_dot = functools.partial(jnp.dot, preferred_element_type=F32, precision=lax.Precision.HIGHEST)
    phik, phiv = phik_ref[...], phiv_ref[...]
    kc2 = jnp.concatenate([hi_dot(ws[:, 0:128], phik), hi_dot(ws[:, 128:256], phiv),
                           hi_dot(ws[:, 256:384], phik), hi_dot(ws[:, 384:512], phiv)], axis=1)
    jl = lax.broadcasted_iota(I32, (N_HEADS, nb), 1)
    d_e = past - (SEL_BLOCK * jl + CMP_BLOCK - 1)
    d_o = past - (SEL_BLOCK * jl + 2 * CMP_BLOCK - 1)
    s_e = _dot_nt(q, kc2[:, 0:128].astype(BF16)) + _bias_from_dist(d_e, rb)
    s_o = _dot_nt(q, kc2[:, 256:384].astype(BF16)) + _bias_from_dist(d_o, rb)
    valid_e, valid_o = d_e >= 0, d_o >= 0
    s_e = jnp.where(valid_e, s_e, NEG)
    s_o = jnp.where(valid_o, s_o, NEG)
    mx = jnp.maximum(jnp.max(s_e, axis=1, keepdims=True), jnp.max(s_o, axis=1, keepdims=True))
    e_e = jnp.where(valid_e, jnp.exp(s_e - mx), 0.0)
    e_o = jnp.where(valid_o, jnp.exp(s_o - mx), 0.0)
    den = jnp.maximum(jnp.sum(e_e, axis=1, keepdims=True) + jnp.sum(e_o, axis=1, keepdims=True), 1e-30)
    p_e, p_o = e_e / den, e_o / den
    o_c = (jnp.dot(p_e.astype(BF16), kc2[:, 128:256].astype(BF16), preferred_element_type=F32)
           + jnp.dot(p_o.astype(BF16), kc2[:, 384:512].astype(BF16), preferred_element_type=F32))

    pe = p_e + p_o
    imp = jnp.concatenate([jnp.sum(pe[0:GQA], axis=0, keepdims=True),
                           jnp.sum(pe[GQA:2 * GQA], axis=0, keepdims=True)], axis=0)
    jl2 = lax.broadcasted_iota(I32, (N_KV, nb), 1)
    cur = past // SEL_BLOCK
    forced = (jl2 == 0) | (jl2 == cur) | (jl2 == cur - 1)
    score = jnp.where(forced, BIG, imp)
    score = jnp.where(jl2 * SEL_BLOCK <= past, score, NEG)
    _, idxs = _select_blocks(score, N_SEL - 1)
    lane = lax.broadcasted_iota(I32, (N_KV, LANE), 1)
    out_idx = jnp.zeros((N_KV, LANE), I32)
    for r, ix in enumerate(idxs):
        out_idx = jnp.where(lane == r, ix, out_idx)
    idx_ref[0] = jnp.concatenate([out_idx, jnp.zeros((8 - N_KV, LANE), I32)], axis=0)

    cw = cw_ref[0]
    il = lax.broadcasted_iota(I32, (N_HEADS, wbuf), 1)
    dq = wbuf - il
    s_w = _dot_nt(q, cw[:, 0:128].astype(BF16)) + _bias_from_dist(dq, rb)
    valid_w = dq < WINDOW
    s_w = jnp.where(valid_w, s_w, NEG)
    wn = wn_ref[0]
    qf = q.astype(F32)
    s_n = jnp.sum(qf * wn[:, 0:128].astype(BF16).astype(F32), axis=1, keepdims=True) + rb(0)
    mw = jnp.maximum(jnp.max(s_w, axis=1, keepdims=True), s_n)
    e_w = jnp.where(valid_w, jnp.exp(s_w - mw), 0.0)
    e_n = jnp.exp(s_n - mw)
    den_w = jnp.sum(e_w, axis=1, keepdims=True) + e_n
    o_w = (jnp.dot(e_w.astype(BF16), cw[:, 128:256].astype(BF16), preferred_element_type=F32)
           + e_n * wn[:, 128:256]) / den_w
    ocw_ref[0] = g_c * o_c + g_w * o_w


def _nsa_s1(q8, wsum2, cache_win3, winn, gates, rbt, phik, phiv, past):
    db = q8.shape[0]
    nb = wsum2.shape[1]
    wbuf = cache_win3.shape[1]
    body = functools.partial(_nsa_s1_body, past=past)
    return pl.pallas_call(
        body,
        grid=(db,),
        in_specs=[pl.BlockSpec((1, N_HEADS, LANE), lambda b: (b, 0, 0)),
                  pl.BlockSpec((1, nb, 512), lambda b: (b, 0, 0)),
                  pl.BlockSpec((1, wbuf, 256), lambda b: (b, 0, 0)),
                  pl.BlockSpec((1, 1, 256), lambda b: (b, 0, 0)),
                  pl.BlockSpec((1, 1, LANE), lambda b: (b, 0, 0)),
                  pl.BlockSpec((N_HEADS, N_BUCKETS), lambda b: (0, 0)),
                  pl.BlockSpec((LANE, LANE), lambda b: (0, 0)),
                  pl.BlockSpec((LANE, LANE), lambda b: (0, 0))],
        out_specs=[pl.BlockSpec((1, N_HEADS, LANE), lambda b: (b, 0, 0)),
                   pl.BlockSpec((1, 8, LANE), lambda b: (b, 0, 0))],
        out_shape=[jax.ShapeDtypeStruct((db, N_HEADS, LANE), F32),
                   jax.ShapeDtypeStruct((db, 8, LANE), I32)],
        compiler_params=_cparams(("arbitrary",)),
    )(q8, wsum2, cache_win3, winn, gates, rbt, phik, phiv)


def _nsa_s2_body(sel_ref, pt_ref, q_ref, blk0_ref, blk1_ref, kvn_ref, g_ref, rbt_ref, ocw_ref, o_ref,
                 acc_sc, m_sc, l_sc, *, past, npick):
    b, k = pl.program_id(0), pl.program_id(1)
    q = q_ref[0]
    rbt = rbt_ref[...]
    rb = lambda j: rbt[:, j:j + 1]
    kvn = kvn_ref[0]
    top = lax.broadcasted_iota(I32, (N_HEADS, 1), 0) < GQA

    @pl.when(k == 0)
    def _init():
        s_n = jnp.sum(q.astype(F32) * kvn[:, 256:384].astype(BF16).astype(F32), axis=1, keepdims=True) + rb(0)
        m_sc[...] = s_n
        l_sc[...] = jnp.ones(l_sc.shape, F32)
        acc_sc[...] = jnp.where(top, kvn[:, 384:384 + HEAD_DIM], kvn[:, 384 + HEAD_DIM:512])

    j0 = sel_ref[(b * N_KV + 0) * npick + k]
    j1 = sel_ref[(b * N_KV + 1) * npick + k]
    k0 = blk0_ref[0, 0, :, 0, 0, :].astype(BF16)
    v0 = blk0_ref[0, 0, :, 1, 0, :].astype(BF16)
    k1 = blk1_ref[0, 0, :, 0, 1, :].astype(BF16)
    v1 = blk1_ref[0, 0, :, 1, 1, :].astype(BF16)
    s = jnp.concatenate([_dot_nt(q[0:GQA, 0:HEAD_DIM], k0), _dot_nt(q[GQA:, HEAD_DIM:], k1)], axis=0)
    r = lax.broadcasted_iota(I32, (N_HEADS, SEL_BLOCK), 1)
    d = past - (jnp.where(top, j0, j1) * SEL_BLOCK + r)
    s = s + _bias_from_dist(d, rb)
    s = jnp.where(d >= 0, s, NEG)
    m_old = m_sc[...]
    m_new = jnp.maximum(m_old, jnp.max(s, axis=1, keepdims=True))
    alpha = jnp.exp(m_old - m_new)
    p = jnp.exp(s - m_new).astype(BF16)
    pv = jnp.concatenate([jnp.dot(p[0:GQA], v0, preferred_element_type=F32),
                          jnp.dot(p[GQA:], v1, preferred_element_type=F32)], axis=0)
    l_sc[...] = alpha * l_sc[...] + jnp.sum(p.astype(F32), axis=1, keepdims=True)
    acc_sc[...] = alpha * acc_sc[...] + pv
    m_sc[...] = m_new

    @pl.when(k == npick - 1)
    def _fin():
        g_s = _head_rows(pltpu.roll(g_ref[0], LANE - N_HEADS, 1))
        o = acc_sc[...] / l_sc[...]
        o_ref[0] = ocw_ref[0] + g_s * jnp.concatenate([o, o], axis=1)


def _nsa_s2(sel_flat, page_table, q8, cache_kv_l, kv4n, gates, rbt, ocw, past):
    db = q8.shape[0]
    npick = N_SEL - 1
    page = cache_kv_l.shape[2]
    bpp = page // SEL_BLOCK
    body = functools.partial(_nsa_s2_body, past=past, npick=npick)

    def blk_spec(h):
        def blk_map(b, k, sel, pt):
            j = sel[(b * N_KV + h) * npick + k]
            return (0, pt[b, j // bpp], j % bpp, 1, 0, 0)
        return pl.BlockSpec((1, 1, SEL_BLOCK, 2, N_KV, HEAD_DIM), blk_map)

    grid_spec = pltpu.PrefetchScalarGridSpec(
        num_scalar_prefetch=2,
        grid=(db, npick),
        in_specs=[pl.BlockSpec((1, N_HEADS, LANE), lambda b, k, sel, pt: (b, 0, 0)),
                  blk_spec(0), blk_spec(1),
                  pl.BlockSpec((1, 1, 512), lambda b, k, sel, pt: (b, 0, 0)),
                  pl.BlockSpec((1, 1, LANE), lambda b, k, sel, pt: (b, 0, 0)),
                  pl.BlockSpec((N_HEADS, N_BUCKETS), lambda b, k, sel, pt: (0, 0)),
                  pl.BlockSpec((1, N_HEADS, LANE), lambda b, k, sel, pt: (b, 0, 0))],
        out_specs=pl.BlockSpec((1, N_HEADS, LANE), lambda b, k, sel, pt: (b, 0, 0)),
        scratch_shapes=[pltpu.VMEM((N_HEADS, HEAD_DIM), F32),
                        pltpu.VMEM((N_HEADS, 1), F32),
                        pltpu.VMEM((N_HEADS, 1), F32)],
    )
    return pl.pallas_call(
        body,
        grid_spec=grid_spec,
        out_shape=jax.ShapeDtypeStruct((db, N_HEADS, LANE), F32),
        compiler_params=_cparams(("arbitrary", "arbitrary")),
    )(sel_flat, page_table, q8, cache_kv_l, cache_kv_l, kv4n, gates, rbt, ocw)


def _pool_body(x_ref, halo_ref, prev_ref, w_ref, sc_ref, o_ref, *, tm, pos_base):
    i = pl.program_id(1)
    halo = jnp.where(i == 0, prev_ref[0], halo_ref[0])
    cur = x_ref[0]
    ext = jnp.concatenate([halo, cur], axis=0)
    sums = {1: ext}
    w = 1
    while w < max(POOL_WINDOWS):
        sums[2 * w] = sums[w] + pltpu.roll(sums[w], w, 0)
        w *= 2
    pos = pos_base + i * tm + lax.broadcasted_iota(I32, (tm, 1), 0)
    outs = []
    for gi, wl in enumerate(POOL_WINDOWS):
        ch = slice(gi * POOL_GROUP, (gi + 1) * POOL_GROUP)
        tot = sums[wl][16:, ch]
        cnt = jnp.minimum(wl, pos + 1).astype(F32)
        d = tot / cnt - cur[:, ch]
        outs.append(jnp.dot(d.astype(BF16), w_ref[gi].astype(BF16), preferred_element_type=F32))
    o_ref[0] = jnp.concatenate(outs, axis=1) * sc_ref[...]


def _pool(p, prev16, pool_w, pool_scale, tm, pos_base):
    b, t, c = p.shape
    body = functools.partial(_pool_body, tm=tm, pos_base=pos_base)
    hb = tm // 16
    return pl.pallas_call(
        body,
        grid=(b, t // tm),
        in_specs=[pl.BlockSpec((1, tm, c), lambda bi, i: (bi, i, 0)),
                  pl.BlockSpec((1, 16, c), lambda bi, i: (bi, jnp.maximum(i * hb - 1, 0), 0)),
                  pl.BlockSpec((1, 16, c), lambda bi, i: (bi, 0, 0)),
                  pl.BlockSpec((len(POOL_WINDOWS), POOL_GROUP, POOL_GROUP), lambda bi, i: (0, 0, 0)),
                  pl.BlockSpec((1, c), lambda bi, i: (0, 0))],
        out_specs=pl.BlockSpec((1, tm, c), lambda bi, i: (bi, i, 0)),
        out_shape=jax.ShapeDtypeStruct((b, t, c), F32),
        compiler_params=_cparams(("arbitrary", "arbitrary")),
    )(p, p, prev16, pool_w, pool_scale.reshape(1, c))


def _rms(x, g):
    return x * lax.rsqrt(jnp.mean(x * x, axis=-1, keepdims=True) + EPS) * g


def _outproj_body(oa_ref, op_ref, x_ref, gate_ref, g_ref, wa_ref, wp_ref, o_ref):
    mix = (jnp.dot(oa_ref[0].astype(BF16), wa_ref[...], preferred_element_type=F32)
           + jnp.dot(op_ref[0].astype(BF16), wp_ref[...], preferred_element_type=F32))
    o_ref[0] = x_ref[0] + gate_ref[0] * _rms(mix, g_ref[...])


def _outproj(oa, op, x, gate, g, wa, wp, tm):
    bx, t, d = x.shape
    per_row = gate.shape[1] != 1
    ms = _mod_specs(t, tm, d, per_row)
    return pl.pallas_call(
        _outproj_body,
        grid=(bx, t // tm),
        in_specs=[pl.BlockSpec((1, tm, oa.shape[2]), lambda b, i: (b, i, 0)),
                  pl.BlockSpec((1, tm, op.shape[2]), lambda b, i: (b, i, 0)),
                  pl.BlockSpec((1, tm, d), lambda b, i: (b, i, 0)), ms,
                  pl.BlockSpec((1, d), lambda b, i: (0, 0)),
                  pl.BlockSpec(wa.shape, lambda b, i: (0, 0)),
                  pl.BlockSpec(wp.shape, lambda b, i: (0, 0))],
        out_specs=pl.BlockSpec((1, tm, d), lambda b, i: (b, i, 0)),
        out_shape=jax.ShapeDtypeStruct((bx, t, d), F32),
        compiler_params=_cparams(("arbitrary", "arbitrary")),
    )(oa, op, x, gate, g.reshape(1, d), wa, wp)


def _final_body(f_ref, x_ref, gate_ref, g_ref, o_ref):
    f = jnp.concatenate([f_ref[0, :, c, :] for c in range(f_ref.shape[2])], axis=1)
    o_ref[0] = x_ref[0] + gate_ref[0] * _rms(f, g_ref[...])


def _final(f, x, gate, g, tm):
    bx, t, d = x.shape
    per_row = gate.shape[1] != 1
    ms = _mod_specs(t, tm, d, per_row)
    xs = pl.BlockSpec((1, tm, d), lambda b, i: (b, i, 0))
    fs = pl.BlockSpec((1, tm, d // LANE, LANE), lambda b, i: (b, i, 0, 0))
    return pl.pallas_call(
        _final_body,
        grid=(bx, t // tm),
        in_specs=[fs, xs, ms, pl.BlockSpec((1, d), lambda b, i: (0, 0))],
        out_specs=xs,
        out_shape=jax.ShapeDtypeStruct((bx, t, d), F32),
        compiler_params=_cparams(("arbitrary", "arbitrary")),
    )(f, x, gate, g.reshape(1, d))


def _topk_rows(s, k):
    n = s.shape[0]
    row = lax.broadcasted_iota(I32, s.shape, 0)
    krow = lax.broadcasted_iota(I32, (k, s.shape[1]), 0)
    vals = jnp.zeros((k, s.shape[1]), F32)
    idxs = jnp.zeros((k, s.shape[1]), I32)
    for it in range(k):
        mx = jnp.max(s, axis=0, keepdims=True)
        ix = jnp.min(jnp.where(s == mx, row, n), axis=0, keepdims=True)
        vals = jnp.where(krow == it, mx, vals)
        idxs = jnp.where(krow == it, ix, idxs)
        s = jnp.where(row == ix, REMOVED, s)
    return vals, idxs


def _pick_rows(table, ix):
    k = table.shape[0]
    out = jnp.zeros(ix.shape, table.dtype)
    for a in range(k):
        out = jnp.where(ix == a, table[a:a + 1, :], out)
    return out


def _peer_topk_body(q_ref, keys_ref, idx_ref, gate_ref, e_sc):
    kk = PEER_TOPK

    def head(h, carry):
        s1 = _dot_nt(keys_ref[2 * h], q_ref[2 * h])
        s2 = _dot_nt(keys_ref[2 * h + 1], q_ref[2 * h + 1])
        v1, i1 = _topk_rows(s1, kk)
        v2, i2 = _topk_rows(s2, kk)
        cand = jnp.concatenate([v1[a:a + 1, :] + v2 for a in range(kk)], axis=0)
        best, flat = _topk_rows(cand, kk)
        e = _pick_rows(i1, flat // kk) * PEER_KEYS + _pick_rows(i2, flat % kk)
        ex = jnp.exp(best - jnp.max(best, axis=0, keepdims=True))
        gate = ex / jnp.sum(ex, axis=0, keepdims=True)
        off = pl.multiple_of(h * kk, kk)
        e_sc[pl.ds(off, kk), :] = e
        gate_ref[pl.ds(off, kk), :] = gate
        return carry

    lax.fori_loop(0, PEER_HEADS, head, 0)
    idx_ref[...] = e_sc[...].T


def _peer_topk(q16, keys16, tt):
    nq, n, _ = q16.shape
    rows = PEER_HEADS * PEER_TOPK
    return pl.pallas_call(
        _peer_topk_body,
        grid=(n // tt,),
        in_specs=[pl.BlockSpec((nq, tt, LANE), lambda i: (0, i, 0)),
                  pl.BlockSpec(keys16.shape, lambda i: (0, 0, 0))],
        out_specs=[pl.BlockSpec((tt, rows), lambda i: (i, 0)),
                   pl.BlockSpec((rows, tt), lambda i: (0, i))],
        out_shape=[jax.ShapeDtypeStruct((n, rows), I32),
                   jax.ShapeDtypeStruct((rows, n), F32)],
        scratch_shapes=[pltpu.VMEM((rows, tt), I32)],
        compiler_params=_cparams(("arbitrary",)),
    )(q16, keys16)


PAD_ROWS = 4
HALF_ROWS = 4
_SLOT_OF_ROW = (6, 2, 4, 0, 7, 3, 5, 1)


def _expert_slab(tab_ref, e, late):
    off = pl.multiple_of(e * HALF_ROWS + (0 if late else PAD_ROWS), HALF_ROWS)
    w = tab_ref[pl.ds(off, 8), :]
    lo = pltpu.bitcast(w << 16, F32)
    hi = pltpu.bitcast(w & jnp.uint32(0xFFFF0000), F32)
    return lo, hi


def _peer_u_body(idx_ref, x_ref, gate_ref, tab_ref, act_ref, *, tt):
    rows = PEER_HEADS * PEER_TOPK
    ngrp = rows // 8
    lane = lax.broadcasted_iota(I32, (8, tt), 1)
    sub = lax.broadcasted_iota(I32, (8, LANE), 0)
    m1 = (sub & 2) != 0
    m2 = (sub & 1) != 0
    low = sub < HALF_ROWS

    def token(t, accs):
        x8 = x_ref[t]
        xr = pltpu.roll(x8, HALF_ROWS, 0)
        xa = (jnp.where(low, x8, 0.0), jnp.where(low, 0.0, xr))
        xb = (jnp.where(low, xr, 0.0), jnp.where(low, 0.0, x8))
        new = []
        for gidx in range(ngrp):
            prod = [None] * 8
            for r in range(8):
                slot = _SLOT_OF_ROW[r]
                late = slot % 2
                lo, hi = _expert_slab(tab_ref, idx_ref[t, gidx * 8 + r], late)
                prod[slot] = lo * xa[late] + hi * xb[late]
            c = [prod[2 * k] + prod[2 * k + 1] for k in range(4)]
            d = [c[k] + pltpu.roll(c[k], 2 if k % 2 == 0 else 6, 0) for k in range(4)]
            e0, e1 = jnp.where(m1, d[0], d[1]), jnp.where(m1, d[2], d[3])
            f0 = e0 + pltpu.roll(e0, 1, 0)
            f1 = e1 + pltpu.roll(e1, 7, 0)
            dcol = jnp.sum(jnp.where(m2, f0, f1), axis=1, keepdims=True)
            new.append(jnp.where(lane == t, dcol, accs[gidx]))
        return tuple(new)

    accs = lax.fori_loop(0, tt, token, tuple(jnp.zeros((8, tt), F32) for _ in range(ngrp)))
    d = jnp.concatenate(accs, axis=0)
    act_ref[...] = (jax.nn.gelu(d, approximate=True) * gate_ref[...]).T


def _peer_u(idx, x3, gate, tab, tt):
    n, rows = idx.shape
    body = functools.partial(_peer_u_body, tt=tt)
    return pl.pallas_call(
        body,
        grid=(n // tt,),
        in_specs=[pl.BlockSpec((tt, rows), lambda i: (i, 0), memory_space=pltpu.SMEM),
                  pl.BlockSpec((tt, 8, LANE), lambda i: (i, 0, 0)),
                  pl.BlockSpec((rows, tt), lambda i: (0, i)),
                  pl.BlockSpec(tab.shape, lambda i: (0, 0), pipeline_mode=pl.Buffered(1))],
        out_specs=pl.BlockSpec((tt, rows), lambda i: (i, 0)),
        out_shape=jax.ShapeDtypeStruct((n, rows), F32),
        compiler_params=_cparams(("arbitrary",)),
    )(idx, x3, gate, tab)


def _peer_v_body(idx_ref, act_ref, tab_ref, f_ref, *, tt):
    rows = PEER_HEADS * PEER_TOPK
    sub = lax.broadcasted_iota(I32, (8, LANE), 0)

    def token(t, carry):
        al = [jnp.zeros((8, LANE), F32) for _ in range(2)]
        ah = [jnp.zeros((8, LANE), F32) for _ in range(2)]
        for r in range(rows):
            lo, hi = _expert_slab(tab_ref, idx_ref[t, r], False)
            a = act_ref[t, r]
            al[r % 2] = al[r % 2] + a * lo
            ah[r % 2] = ah[r % 2] + a * hi
        f_ref[t] = jnp.where(sub < HALF_ROWS, al[0] + al[1], pltpu.roll(ah[0] + ah[1], HALF_ROWS, 0))
        return carry

    lax.fori_loop(0, tt, token, 0)


def _peer_v(idx, act, tab, tt):
    n, rows = idx.shape
    body = functools.partial(_peer_v_body, tt=tt)
    return pl.pallas_call(
        body,
        grid=(n // tt,),
        in_specs=[pl.BlockSpec((tt, rows), lambda i: (i, 0), memory_space=pltpu.SMEM),
                  pl.BlockSpec((tt, rows), lambda i: (i, 0), memory_space=pltpu.SMEM),
                  pl.BlockSpec(tab.shape, lambda i: (0, 0), pipeline_mode=pl.Buffered(1))],
        out_specs=pl.BlockSpec((tt, 8, LANE), lambda i: (i, 0, 0)),
        out_shape=jax.ShapeDtypeStruct((n, 8, LANE), F32),
        compiler_params=_cparams(("arbitrary",)),
    )(idx, act, tab)


def _pack_table(tab):
    n, d = tab.shape
    bits = lax.bitcast_convert_type(tab.astype(jnp.bfloat16), jnp.uint16).astype(U32)
    words = (bits[:, :d // 2] | (bits[:, d // 2:] << 16)).reshape(n * HALF_ROWS, LANE)
    pad = jnp.zeros((PAD_ROWS, LANE), U32)
    return jnp.concatenate([pad, words, pad], axis=0)


def _peer(x1, sc, sh, g, wq, keys16, tab_u, tab_v, tm):
    bx, t, d = x1.shape
    n = bx * t
    q16, h2 = _peerq(x1, sc, sh, g, wq, tm)
    tt = min(128, n)
    idx, gate = _peer_topk(q16.reshape(2 * PEER_HEADS, n, LANE), keys16, tt)
    act = _peer_u(idx, h2.reshape(n, d // LANE, LANE), gate, tab_u, tt)
    f = _peer_v(idx, act, tab_v, tt)
    return f.reshape(bx, t, d // LANE, LANE)


def _pack_w_in(w_in):
    d = w_in.shape[0]
    scale = HEAD_DIM ** -0.5
    cols = []
    for hd in range(N_HEADS):
        blk = w_in[:, hd * HEAD_DIM:(hd + 1) * HEAD_DIM] * scale
        z = jnp.zeros((d, HEAD_DIM), w_in.dtype)
        cols += [blk, z] if hd // GQA == 0 else [z, blk]
    o0 = N_HEADS * HEAD_DIM
    o1 = o0 + 6 * N_KV * HEAD_DIM
    o2 = o1 + 3 * N_HEADS
    cols.append(w_in[:, o0:o0 + 512])
    cols.append(w_in[:, o0 + 512:o1])
    cols.append(w_in[:, o2:])
    cols.append(w_in[:, o1:o2])
    cols.append(jnp.zeros((d, LANE - 3 * N_HEADS), w_in.dtype))
    return jnp.concatenate(cols, axis=1).astype(BF16)


def _pack_w_out(w_out):
    d = w_out.shape[1]
    rows = []
    for hd in range(N_HEADS):
        blk = w_out[hd * HEAD_DIM:(hd + 1) * HEAD_DIM]
        z = jnp.zeros((HEAD_DIM, d), w_out.dtype)
        rows += [blk, z] if hd // GQA == 0 else [z, blk]
    wa = jnp.concatenate(rows, axis=0).astype(BF16)
    wp = w_out[N_HEADS * HEAD_DIM:].astype(BF16)
    return wa, wp


def _pack_compress(alpha, phi):
    a = jnp.concatenate([alpha[0], alpha[0], alpha[1], alpha[1]], axis=1)
    z = jnp.zeros((HEAD_DIM, HEAD_DIM), phi.dtype)
    blocks = [phi[0], phi[0], phi[1], phi[1]]
    p = jnp.concatenate([jnp.concatenate([blocks[r] if c == r else z for c in range(4)], axis=1)
                         for r in range(4)], axis=0)
    return a, p


def kernel(x_prompt, x_sample, cache_kv, cache_win, state_pool, page_table, c_prompt, c_sample,
           w_ada, b_ada, norm_g, w_in, w_out, cmp_alpha, cmp_phi, rel_bias, pool_w, pool_scale,
           peer_wq, peer_keys, peer_u, peer_v):
    depth = w_ada.shape[0]
    assert depth == 1 and x_sample.shape[1] == 1
    l = 0
    b, s, d = x_prompt.shape
    db = x_sample.shape[0]
    n_pool, page = cache_kv.shape[1], cache_kv.shape[2]
    n_pages = page_table.shape[1]
    past = n_pages * page
    wbuf = cache_win.shape[2]
    tq = WINDOW // 2

    w_in_p = _pack_w_in(w_in[l])
    wa, wp = _pack_w_out(w_out[l])
    a_cmp, phi_cmp = _pack_compress(cmp_alpha[l], cmp_phi[l])
    wq = peer_wq[l].astype(BF16)
    keys16 = peer_keys[l].reshape(2 * PEER_HEADS, PEER_KEYS, LANE).astype(BF16)
    tab_u = _pack_table(peer_u[l])
    tab_v = _pack_table(peer_v[l])
    g = norm_g[l]

    c_all = jnp.concatenate([c_prompt, c_sample], axis=0)
    mod = _mod(c_all, w_ada[l], b_ada[l]).reshape(b + db, 6, d)
    mod_p = [mod[:b, k][:, None, :] for k in range(6)]
    mod_s = [mod[b:, k][None, :, :] for k in range(6)]

    tm = min(512, s)
    qpad, kv4, selwin, win, pool_in, gates = _inproj(x_prompt, mod_p[1], mod_p[0], g[0], w_in_p, tm)
    kcvc = _compress(kv4, a_cmp, phi_cmp, min(1024, s))
    o_attn = _nsa_prompt(qpad, selwin, kcvc.reshape(b, s // SEL_BLOCK, 512), gates, rel_bias, tq)
    prev0 = jnp.zeros((b, 16, POOL_DIM), F32)
    o_pool = _pool(pool_in, prev0, pool_w[l], pool_scale[l], min(1024, s), 0)
    x1 = _outproj(o_attn, o_pool, x_prompt, mod_p[2], g[1], wa, wp, tm)
    f = _peer(x1, mod_p[4], mod_p[3], g[2], wq, keys16, tab_u, tab_v, tm)
    y_prompt = _final(f, x1, mod_p[5], g[3], tm)

    kv_prompt = kv4.reshape(1, b, s, 4, N_KV, HEAD_DIM)
    win_prompt = win[:, s - min(WINDOW, s):].reshape(1, b, min(WINDOW, s), 2, N_KV, HEAD_DIM)
    pool_prompt = pool_in[:, s - POOL_STATE:][None]

    xs = x_sample.reshape(1, db, d)
    qpad_s, kv4_s, _, win_s, pool_s, gates_s = _inproj(xs, mod_s[1], mod_s[0], g[0], w_in_p, db)
    cache_l = cache_kv[l:l + 1]
    wsum = _wsum_paged(cache_l, page_table, cmp_alpha[l])
    q8 = qpad_s.reshape(db, N_HEADS, LANE)
    kv4n = kv4_s.reshape(db, 1, 512)
    winn = win_s.reshape(db, 1, 256)
    gat = gates_s.reshape(db, 1, LANE)
    cw3 = cache_win[l].reshape(db, wbuf, 256)
    rbt = rel_bias.T
    ocw, sel_idx = _nsa_s1(q8, wsum.reshape(db, past // SEL_BLOCK, 512), cw3, winn, gat, rbt,
                           phi_cmp[:LANE, :LANE], phi_cmp[LANE:, LANE:], past)
    sel_flat = sel_idx[:, :N_KV, :N_SEL - 1].reshape(-1)
    o_attn_s = _nsa_s2(sel_flat, page_table, q8, cache_l, kv4n, gat, rbt, ocw, past)
    ext = jnp.concatenate([jnp.zeros((db, 1, POOL_DIM), F32), state_pool[l], pool_s.reshape(db, 1, POOL_DIM)], axis=1)
    o_pool_s = _pool(ext[:, 1:], jnp.zeros((db, 16, POOL_DIM), F32), pool_w[l], pool_scale[l], 16,
                     past - POOL_STATE)[:, POOL_STATE:]
    x1s = _outproj(o_attn_s.reshape(1, db, N_HEADS * LANE).astype(BF16), o_pool_s.reshape(1, db, POOL_DIM),
                   xs, mod_s[2], g[1], wa, wp, db)
    fs = _peer(x1s, mod_s[4], mod_s[3], g[2], wq, keys16, tab_u, tab_v, db)
    y_sample = _final(fs, x1s, mod_s[5], g[3], db).reshape(db, 1, d)

    kv_sample = kv4_s.reshape(1, db, 1, 4, N_KV, HEAD_DIM)
    win_sample = jnp.concatenate([cache_win[l][:, 1:], win_s.reshape(db, 1, 2, N_KV, HEAD_DIM)], axis=1)[None]
    pool_sample = ext[:, 2:][None]
    return (y_prompt, y_sample, kv_prompt, kv_sample, win_prompt, win_sample, pool_prompt, pool_sample)
```

```python
import functools
import math

import numpy as np
import jax
import jax.numpy as jnp
from jax import lax
from jax.experimental import pallas as pl
from jax.experimental.pallas import tpu as pltpu

F32, BF16, I32, U32 = jnp.float32, jnp.bfloat16, jnp.int32, jnp.uint32

D_MODEL = 1024
N_HEADS = 8
HEAD_DIM = 64
N_KV = 2
GQA = 4
CMP_BLOCK = 32
SEL_BLOCK = 64
N_SEL = 16
WINDOW = 512
N_BUCKETS = 32
REL_MAX_DIST = 128
POOL_WINDOWS = (2, 4, 8, 16)
POOL_DIM = 512
POOL_GROUP = 128
POOL_STATE = 15
PEER_HEADS = 8
PEER_KEYS = 128
PEER_TOPK = 16
EPS = 1e-6
LANE = 128
NEG = -1e30
BIG = 1e30
REMOVED = -3e38
VMEM_LIMIT = 56 * 1024 * 1024

A_KV4, A_WIN, A_POOL, A_END = 0, 512, 768, 1280
B_Q, B_V, B_GATE, B_END = 0, 1024, 1280, 1408


def _cparams(sem):
    return pltpu.CompilerParams(dimension_semantics=sem, vmem_limit_bytes=VMEM_LIMIT)


def _bucket_thresholds():
    n = np.arange(0, 4 * REL_MAX_DIST)
    exact = N_BUCKETS // 2
    ratio = np.log(np.maximum(n, exact).astype(np.float32) / np.float32(exact)) / np.float32(
        math.log(REL_MAX_DIST / exact))
    large = np.minimum(exact + (ratio * np.float32(N_BUCKETS - exact)).astype(np.int32), N_BUCKETS - 1)
    b = np.where(n < exact, n, large)
    return [int(np.argmax(b >= j)) for j in range(N_BUCKETS)]


_THR = _bucket_thresholds()
FAR_DIST = _THR[N_BUCKETS - 1]


def _bias_from_dist(d, rb):
    val = jnp.full(d.shape, rb(0), F32)
    for j in range(1, N_BUCKETS):
        if _THR[j] == _THR[j - 1] and j > 1:
            continue
        jj = j
        while jj + 1 < N_BUCKETS and _THR[jj + 1] == _THR[j]:
            jj += 1
        val = jnp.where(d >= _THR[j], rb(jj), val)
    return val


def _dot_nt(a, b):
    return lax.dot_general(a, b, (((1,), (1,)), ((), ())), preferred_element_type=F32)


def _mod_body(c_ref, w_ref, b_ref, o_ref):
    c = c_ref[...]
    a = (c * jax.nn.sigmoid(c)).astype(BF16)
    o_ref[...] = jnp.dot(a, w_ref[...].astype(BF16), preferred_element_type=F32) + b_ref[...]


def _mod(c, w_ada, b_ada):
    r, d = c.shape
    n = w_ada.shape[1]
    tn = 1536
    return pl.pallas_call(
        _mod_body,
        grid=(n // tn,),
        in_specs=[pl.BlockSpec((r, d), lambda j: (0, 0)),
                  pl.BlockSpec((d, tn), lambda j: (0, j)),
                  pl.BlockSpec((1, tn), lambda j: (0, j))],
        out_specs=pl.BlockSpec((r, tn), lambda j: (0, j)),
        out_shape=jax.ShapeDtypeStruct((r, n), F32),
        compiler_params=_cparams(("arbitrary",)),
    )(c, w_ada, b_ada.reshape(1, n))


def _norm_mod(x, g, sc, sh):
    ms = jnp.mean(x * x, axis=-1, keepdims=True)
    h = x * lax.rsqrt(ms + EPS) * g
    return h * (1.0 + sc) + sh


def _inproj_body(x_ref, sc_ref, sh_ref, g_ref, wa_ref, wb_ref, kv_ref, kk_ref, win_ref, pool_ref,
                 qt_ref, gt_ref, *vt_refs, tq):
    h = _norm_mod(x_ref[0], g_ref[...], sc_ref[0], sh_ref[0]).astype(BF16)
    u = jnp.dot(h, wa_ref[...], preferred_element_type=F32)
    kv_ref[0] = u[:, A_KV4:A_WIN]
    win_ref[0] = u[:, A_WIN:A_POOL]
    pool_ref[0] = u[:, A_POOL:A_END]
    kk_ref[0, :, 0:LANE] = u[:, A_KV4 + 2 * LANE:A_KV4 + 3 * LANE].astype(BF16)
    kk_ref[0, :, LANE:2 * LANE] = u[:, A_WIN:A_WIN + LANE].astype(BF16)
    ut = _dot_nt(wb_ref[...], h)
    qt_ref[0] = ut[B_Q:B_V].astype(BF16)
    gt_ref[0] = jax.nn.sigmoid(ut[B_GATE:B_END])
    for vt_ref in vt_refs:
        for c in range(ut.shape[1] // tq):
            vt_ref[0, c] = ut[B_V:B_GATE, c * tq:(c + 1) * tq].astype(BF16)


def _mod_specs(t, tm, d, per_row):
    if per_row:
        return pl.BlockSpec((1, tm, d), lambda b, i: (b, i, 0))
    return pl.BlockSpec((1, 1, d), lambda b, i: (b, 0, 0))


def _inproj(x, sc, sh, g, wa, wbt, tm, tq):
    bx, t, d = x.shape
    per_row = sc.shape[1] != 1
    ms = _mod_specs(t, tm, d, per_row)
    widths = (512, 256, 256, 512)
    dtypes = (F32, BF16, F32, F32)
    out_specs = [pl.BlockSpec((1, tm, wd), lambda b, i: (b, i, 0)) for wd in widths]
    out_shape = [jax.ShapeDtypeStruct((bx, t, wd), dt) for wd, dt in zip(widths, dtypes)]
    out_specs += [pl.BlockSpec((1, B_V - B_Q, tm), lambda b, i: (b, 0, i)),
                  pl.BlockSpec((1, B_END - B_GATE, tm), lambda b, i: (b, 0, i))]
    out_shape += [jax.ShapeDtypeStruct((bx, B_V - B_Q, t), BF16),
                  jax.ShapeDtypeStruct((bx, B_END - B_GATE, t), F32)]
    if tq is not None:
        out_specs.append(pl.BlockSpec((1, tm // tq, B_GATE - B_V, tq), lambda b, i: (b, i, 0, 0)))
        out_shape.append(jax.ShapeDtypeStruct((bx, t // tq, B_GATE - B_V, tq), BF16))
    return pl.pallas_call(
        functools.partial(_inproj_body, tq=tq),
        grid=(bx, t // tm),
        in_specs=[pl.BlockSpec((1, tm, d), lambda b, i: (b, i, 0)), ms, ms,
                  pl.BlockSpec((1, d), lambda b, i: (0, 0)),
                  pl.BlockSpec(wa.shape, lambda b, i: (0, 0)),
                  pl.BlockSpec(wbt.shape, lambda b, i: (0, 0))],
        out_specs=out_specs,
        out_shape=out_shape,
        compiler_params=_cparams(("arbitrary", "arbitrary")),
    )(x, sc, sh, g.reshape(1, d), wa, wbt)


def _peerq_body(x_ref, sc_ref, sh_ref, g_ref, w_ref, q_ref, h_ref):
    h = _norm_mod(x_ref[0], g_ref[...], sc_ref[0], sh_ref[0])
    for c in range(h.shape[1] // LANE):
        h_ref[0, :, c, :] = h[:, c * LANE:(c + 1) * LANE]
    u = jnp.dot(h.astype(BF16), w_ref[...], preferred_element_type=F32)
    for j in range(2 * PEER_HEADS):
        q_ref[j, 0] = u[:, j * LANE:(j + 1) * LANE].astype(BF16)


def _peerq(x, sc, sh, g, w, tm):
    bx, t, d = x.shape
    per_row = sc.shape[1] != 1
    ms = _mod_specs(t, tm, d, per_row)
    nq = 2 * PEER_HEADS
    return pl.pallas_call(
        _peerq_body,
        grid=(bx, t // tm),
        in_specs=[pl.BlockSpec((1, tm, d), lambda b, i: (b, i, 0)), ms, ms,
                  pl.BlockSpec((1, d), lambda b, i: (0, 0)),
                  pl.BlockSpec((d, nq * LANE), lambda b, i: (0, 0))],
        out_specs=[pl.BlockSpec((nq, 1, tm, LANE), lambda b, i: (0, b, i, 0)),
                   pl.BlockSpec((1, tm, d // LANE, LANE), lambda b, i: (b, i, 0, 0))],
        out_shape=[jax.ShapeDtypeStruct((nq, bx, t, LANE), BF16),
                   jax.ShapeDtypeStruct((bx, t, d // LANE, LANE), F32)],
        compiler_params=_cparams(("arbitrary", "arbitrary")),
    )(x, sc, sh, g.reshape(1, d), w)


def _compress_rows(rows, a, phi):
    n = rows.shape[0] // CMP_BLOCK
    w = jnp.sum(rows.reshape(n, CMP_BLOCK, rows.shape[1]) * a[None], axis=1)
    return jnp.dot(w, phi, preferred_element_type=F32, precision=lax.Precision.HIGHEST)


def _compress_body(x_ref, a_ref, phi_ref, o_ref):
    o_ref[0] = _compress_rows(x_ref[0], a_ref[...], phi_ref[...])


def _compress(kv4, a, phi, tm):
    b, s, _ = kv4.shape
    return pl.pallas_call(
        _compress_body,
        grid=(b, s // tm),
        in_specs=[pl.BlockSpec((1, tm, 256), lambda bi, i: (bi, i, 0)),
                  pl.BlockSpec((CMP_BLOCK, 256), lambda bi, i: (0, 0)),
                  pl.BlockSpec((256, 256), lambda bi, i: (0, 0))],
        out_specs=pl.BlockSpec((1, tm // CMP_BLOCK, 256), lambda bi, i: (bi, i, 0)),
        out_shape=jax.ShapeDtypeStruct((b, s // CMP_BLOCK, 256), F32),
        compiler_params=_cparams(("arbitrary", "arbitrary")),
    )(kv4, a, phi)


PAGES_PER_STEP = 4


def _wsum_paged_body(pt_ref, *refs):
    page_refs, a_ref, o_ref = refs[:PAGES_PER_STEP], refs[PAGES_PER_STEP], refs[PAGES_PER_STEP + 1]
    a = a_ref[...]
    for p, p_ref in enumerate(page_refs):
        x = p_ref[0, 0]
        per = x.shape[0] // CMP_BLOCK
        w = jnp.sum(x.reshape(per, CMP_BLOCK, 2, N_KV, HEAD_DIM) * a[None], axis=1)
        o_ref[0, p * per:(p + 1) * per] = w


def _wsum_paged(cache_kv_l, page_table, alpha):
    db, n_pages = page_table.shape
    page = cache_kv_l.shape[2]
    per = page // CMP_BLOCK
    a4 = jnp.transpose(alpha, (1, 0, 2))[:, :, None, :]

    def page_spec(r):
        return pl.BlockSpec((1, 1, page, 2, N_KV, HEAD_DIM),
                            lambda b, i, pt: (0, pt[b, PAGES_PER_STEP * i + r], 0, 0, 0, 0))

    grid_spec = pltpu.PrefetchScalarGridSpec(
        num_scalar_prefetch=1,
        grid=(db, n_pages // PAGES_PER_STEP),
        in_specs=[page_spec(r) for r in range(PAGES_PER_STEP)]
        + [pl.BlockSpec((CMP_BLOCK, 2, 1, HEAD_DIM), lambda b, i, pt: (0, 0, 0, 0))],
        out_specs=pl.BlockSpec((1, PAGES_PER_STEP * per, 2, N_KV, HEAD_DIM), lambda b, i, pt: (b, i, 0, 0, 0)),
    )
    return pl.pallas_call(
        _wsum_paged_body,
        grid_spec=grid_spec,
        out_shape=jax.ShapeDtypeStruct((db, n_pages * per, 2, N_KV, HEAD_DIM), F32),
        compiler_params=_cparams(("arbitrary", "arbitrary")),
    )(page_table, *([cache_kv_l] * PAGES_PER_STEP), a4)


def _select_blocks(score, n_pick):
    lane = lax.broadcasted_iota(I32, score.shape, 1)
    nl = score.shape[1]
    sel = jnp.zeros(score.shape, jnp.bool_)
    idxs = []
    for _ in range(n_pick):
        mx = jnp.max(score, axis=1, keepdims=True)
        idx = jnp.min(jnp.where(score == mx, lane, nl), axis=1, keepdims=True)
        hit = lane == idx
        sel = sel | hit
        score = jnp.where(hit, REMOVED, score)
        idxs.append(idx)
    return sel, idxs


def _flash_update(slot, qt, kc, vt, bias, mask, acc_sc, m_sc):
    s = jnp.dot(kc, qt, preferred_element_type=F32)
    if bias is not None:
        s = s + bias
    if mask is not None:
        s = jnp.where(mask, s, NEG)
    m_old = m_sc[slot]
    m_new = jnp.maximum(m_old, jnp.max(s, axis=0, keepdims=True))
    alpha = jnp.exp(m_old - m_new)
    p = jnp.exp(s - m_new)
    acc_sc[slot] = alpha * acc_sc[slot] + jnp.dot(vt, p.astype(BF16), preferred_element_type=F32)
    m_sc[slot] = m_new


def _select_rows(score, n_pick):
    row = lax.broadcasted_iota(I32, score.shape, 0)
    n = score.shape[0]
    sel = jnp.zeros(score.shape, jnp.bool_)
    for _ in range(n_pick):
        mx = jnp.max(score, axis=0, keepdims=True)
        idx = jnp.min(jnp.where(score == mx, row, n), axis=0, keepdims=True)
        hit = row == idx
        sel = sel | hit
        score = jnp.where(hit, REMOVED, score)
    return sel


def _nsa_body(rb_ref, q_ref, kk_ref, vt_ref, kc_ref, g_ref, o_ref,
              bdiag_sc, bcmp_sc, acc_sc, m_sc, osum_sc, *, seq, tq):
    i = pl.program_id(1)
    q0 = i * tq
    nb = seq // SEL_BLOCK
    per = tq // SEL_BLOCK
    ncopy = bcmp_sc.shape[2]

    @pl.when((pl.program_id(0) == 0) & (i == 0))
    def _build_bias_tiles():
        k = lax.broadcasted_iota(I32, (tq, tq), 0)
        t = lax.broadcasted_iota(I32, (tq, tq), 1)
        rr = lax.broadcasted_iota(I32, (2 * nb, tq), 0) - nb
        tc = lax.broadcasted_iota(I32, (2 * nb, tq), 1)
        for hd in range(N_HEADS):
            rb = lambda j, hd=hd: rb_ref[j, hd]
            kvh, cols = hd // GQA, slice((hd % GQA) * tq, (hd % GQA + 1) * tq)
            for dl in range(2):
                bdiag_sc[kvh, dl, :, cols] = _bias_from_dist(t - k + dl * tq, rb) - rb(N_BUCKETS - 1)
            for par in range(2):
                for cp in range(ncopy):
                    bcmp_sc[kvh, par, cp, :, cols] = _bias_from_dist(
                        tc - (SEL_BLOCK * (rr + cp * per) + (par + 1) * CMP_BLOCK - 1), rb)

    gw = GQA * tq
    kc2 = kc_ref[0]
    kce, kco = kc2[:, 0:128].astype(BF16), kc2[:, 256:384].astype(BF16)
    vcet, vcot = kc2[:, 128:256].T.astype(BF16), kc2[:, 384:512].T.astype(BF16)
    qpos = q0 + lax.broadcasted_iota(I32, (nb, tq), 1)
    jl = lax.broadcasted_iota(I32, (nb, tq), 0)
    cur = qpos // SEL_BLOCK
    forced = (jl == 0) | (jl == cur) | (jl == cur - 1)
    causal_blk = jl * SEL_BLOCK <= qpos
    qpos_g = q0 + lax.broadcasted_iota(I32, (nb, gw), 1) % tq
    jl_g = lax.broadcasted_iota(I32, (nb, gw), 0)
    valid_e = qpos_g >= SEL_BLOCK * jl_g + CMP_BLOCK - 1
    valid_o = qpos_g >= SEL_BLOCK * jl_g + 2 * CMP_BLOCK - 1
    cmp_copy = (ncopy - i % ncopy) % ncopy
    cmp_start = pl.multiple_of(nb - (i + cmp_copy) * per, 8)
    kk = lax.broadcasted_iota(I32, (tq, gw), 0)
    tt = lax.broadcasted_iota(I32, (tq, gw), 1) % tq
    causal = tt >= kk
    gates = g_ref[0]
    ek = lax.broadcasted_iota(I32, (tq, nb), 0) // SEL_BLOCK
    ej = lax.broadcasted_iota(I32, (tq, nb), 1)

    def reset_state():
        m_sc[...] = jnp.full(m_sc.shape, NEG, F32)
        acc_sc[...] = jnp.zeros(acc_sc.shape, F32)

    def qgroup(kvh):
        return jnp.concatenate([q_ref[0, hd * LANE:(hd + 1) * LANE, :]
                                for hd in range(kvh * GQA, (kvh + 1) * GQA)], axis=1)

    vrow = lax.broadcasted_iota(I32, (LANE, tq), 0) // HEAD_DIM

    def kv_chunk(c, branch, kvh):
        start = pl.multiple_of(c * tq, tq)
        v = vt_ref[0, c, branch * LANE:(branch + 1) * LANE, :]
        return (kk_ref[0, pl.ds(start, tq), branch * LANE:(branch + 1) * LANE],
                jnp.where(vrow == kvh, v, jnp.ones_like(v)))

    def finish(kvh, branch):
        lrow = (1 - kvh) * HEAD_DIM
        acc = acc_sc[0]
        o = acc / acc[lrow:lrow + 1, :]
        for g in range(GQA):
            hd = kvh * GQA + g
            r = branch * N_HEADS + hd
            osum_sc[hd] += gates[r:r + 1, :] * o[:, g * tq:(g + 1) * tq]

    for kvh in range(N_KV):
        qt = qgroup(kvh)
        s_e = jnp.dot(kce, qt, preferred_element_type=F32) + bcmp_sc[kvh, 0, cmp_copy, pl.ds(cmp_start, nb), :]
        s_o = jnp.dot(kco, qt, preferred_element_type=F32) + bcmp_sc[kvh, 1, cmp_copy, pl.ds(cmp_start, nb), :]
        s_e = jnp.where(valid_e, s_e, NEG)
        s_o = jnp.where(valid_o, s_o, NEG)
        mx = jnp.maximum(jnp.max(s_e, axis=0, keepdims=True), jnp.max(s_o, axis=0, keepdims=True))
        e_e = jnp.where(valid_e, jnp.exp(s_e - mx), 0.0)
        e_o = jnp.where(valid_o, jnp.exp(s_o - mx), 0.0)
        den = jnp.maximum(jnp.sum(e_e, axis=0, keepdims=True) + jnp.sum(e_o, axis=0, keepdims=True), 1e-30)
        p_e, p_o = e_e / den, e_o / den
        pe = p_e + p_o
        imp = (pe[:, 0:tq] + pe[:, tq:2 * tq]) + (pe[:, 2 * tq:3 * tq] + pe[:, 3 * tq:4 * tq])
        o_c = (jnp.dot(vcet, p_e.astype(BF16), preferred_element_type=F32)
               + jnp.dot(vcot, p_o.astype(BF16), preferred_element_type=F32))
        for g in range(GQA):
            hd = kvh * GQA + g
            osum_sc[hd] = gates[hd:hd + 1, :] * o_c[:, g * tq:(g + 1) * tq]

        score = jnp.where(forced, BIG, imp)
        score = jnp.where(causal_blk, score, NEG)
        selm = jnp.where(_select_rows(score, min(N_SEL, nb)), 1.0, 0.0).astype(BF16)
        selm = jnp.concatenate([selm] * GQA, axis=1)

        def sel_mask(c):
            expand = jnp.where(ej == ek + c * per, 1.0, 0.0).astype(BF16)
            return jnp.dot(expand, selm, preferred_element_type=F32) > 0.5

        reset_state()

        def far_chunk(c, carry):
            kc, vt = kv_chunk(c, 0, kvh)
            _flash_update(0, qgroup(kvh), kc, vt, None, sel_mask(c), acc_sc, m_sc)
            return carry

        lax.fori_loop(0, jnp.maximum(i - 1, 0), far_chunk, 0)

        @pl.when(i >= 1)
        def _prev_chunk():
            kc, vt = kv_chunk(i - 1, 0, kvh)
            _flash_update(0, qt, kc, vt, bdiag_sc[kvh, 1], sel_mask(i - 1), acc_sc, m_sc)

        kc, vt = kv_chunk(i, 0, kvh)
        _flash_update(0, qt, kc, vt, bdiag_sc[kvh, 0], sel_mask(i) & causal, acc_sc, m_sc)
        finish(kvh, 1)

        reset_state()

        @pl.when(i >= 2)
        def _win_far():
            kc, vt = kv_chunk(i - 2, 1, kvh)
            _flash_update(0, qt, kc, vt, None, kk > tt, acc_sc, m_sc)

        @pl.when(i >= 1)
        def _win_prev():
            kc, vt = kv_chunk(i - 1, 1, kvh)
            _flash_update(0, qt, kc, vt, bdiag_sc[kvh, 1], None, acc_sc, m_sc)

        kc, vt = kv_chunk(i, 1, kvh)
        _flash_update(0, qt, kc, vt, bdiag_sc[kvh, 0], causal, acc_sc, m_sc)
        finish(kvh, 2)

    for hd in range(N_HEADS):
        o_ref[0, :, hd * LANE:(hd + 1) * LANE] = osum_sc[hd].T.astype(BF16)


def _nsa_prompt(qt, kk, vt, kcvc2, gates_t, rel_bias, tq):
    b, _, s = qt.shape
    nb = s // SEL_BLOCK
    assert WINDOW == 2 * tq and tq + 1 >= FAR_DIST and s % tq == 0
    body = functools.partial(_nsa_body, seq=s, tq=tq)
    return pl.pallas_call(
        body,
        grid=(b, s // tq),
        in_specs=[pl.BlockSpec(memory_space=pltpu.SMEM),
                  pl.BlockSpec((1, N_HEADS * LANE, tq), lambda bi, i: (bi, 0, i)),
                  pl.BlockSpec((1, s, 2 * LANE), lambda bi, i: (bi, 0, 0)),
                  pl.BlockSpec((1, s // tq, 2 * LANE, tq), lambda bi, i: (bi, 0, 0, 0)),
                  pl.BlockSpec((1, nb, 512), lambda bi, i: (bi, 0, 0)),
                  pl.BlockSpec((1, LANE, tq), lambda bi, i: (bi, 0, i))],
        out_specs=pl.BlockSpec((1, tq, N_HEADS * LANE), lambda bi, i: (bi, i, 0)),
        out_shape=jax.ShapeDtypeStruct((b, s, N_HEADS * LANE), BF16),
        scratch_shapes=[pltpu.VMEM((N_KV, 2, tq, GQA * tq), F32),
                        pltpu.VMEM((N_KV, 2, max(1, 8 // (tq // SEL_BLOCK)), 2 * nb, GQA * tq), F32),
                        pltpu.VMEM((1, LANE, GQA * tq), F32),
                        pltpu.VMEM((1, 1, GQA * tq), F32),
                        pltpu.VMEM((N_HEADS, LANE, tq), F32)],
        compiler_params=_cparams(("arbitrary", "arbitrary")),
    )(rel_bias, qt, kk, vt, kcvc2, gates_t)


def _head_rows(x):
    sub = lax.broadcasted_iota(I32, (N_HEADS, LANE), 0)
    lane = lax.broadcasted_iota(I32, (N_HEADS, LANE), 1)
    return jnp.sum(jnp.where(sub == lane, jnp.broadcast_to(x, (N_HEADS, LANE)), 0.0), axis=1, keepdims=True)


def _nsa_s1_body(q_ref, kc_ref, cw_ref, wn_ref, g_ref, rbt_ref, phik_ref, phiv_ref, ocw_ref, idx_ref, *, past):
    nb = kc_ref.shape[1]
    wbuf = cw_ref.shape[1]
    q = q_ref[0]
    rbt = rbt_ref[...]
    rb = lambda j: rbt[:, j:j + 1]
    gates = g_ref[0]
    g_c = _head_rows(gates[:, 0:LANE])
    g_w = _head_rows(pltpu.roll(gates, LANE - 2 * N_HEADS, 1))

    ws = kc_ref[0]
    hi_dot = functools.partial(jnp.dot, preferred_element_type=F32, precision=lax.Precision.HIGHEST)
    phik, phiv = phik_ref[...], phiv_ref[...]
    kc2 = jnp.concatenate([hi_dot(ws[:, 0:128], phik), hi_dot(ws[:, 128:256], phiv),
                           hi_dot(ws[:, 256:384], phik), hi_dot(ws[:, 384:512], phiv)], axis=1)
    jl = lax.broadcasted_iota(I32, (N_HEADS, nb), 1)
    d_e = past - (SEL_BLOCK * jl + CMP_BLOCK - 1)
    d_o = past - (SEL_BLOCK * jl + 2 * CMP_BLOCK - 1)
    s_e = _dot_nt(q, kc2[:, 0:128].astype(BF16)) + _bias_from_dist(d_e, rb)
    s_o = _dot_nt(q, kc2[:, 256:384].astype(BF16)) + _bias_from_dist(d_o, rb)
    valid_e, valid_o = d_e >= 0, d_o >= 0
    s_e = jnp.where(valid_e, s_e, NEG)
    s_o = jnp.where(valid_o, s_o, NEG)
    mx = jnp.maximum(jnp.max(s_e, axis=1, keepdims=True), jnp.max(s_o, axis=1, keepdims=True))
    e_e = jnp.where(valid_e, jnp.exp(s_e - mx), 0.0)
    e_o = jnp.where(valid_o, jnp.exp(s_o - mx), 0.0)
    den = jnp.maximum(jnp.sum(e_e, axis=1, keepdims=True) + jnp.sum(e_o, axis=1, keepdims=True), 1e-30)
    p_e, p_o = e_e / den, e_o / den
    o_c = (jnp.dot(p_e.astype(BF16), kc2[:, 128:256].astype(BF16), preferred_element_type=F32)
           + jnp.dot(p_o.astype(BF16), kc2[:, 384:512].astype(BF16), preferred_element_type=F32))

    pe = p_e + p_o
    imp = jnp.concatenate([jnp.sum(pe[0:GQA], axis=0, keepdims=True),
                           jnp.sum(pe[GQA:2 * GQA], axis=0, keepdims=True)], axis=0)
    jl2 = lax.broadcasted_iota(I32, (N_KV, nb), 1)
    cur = past // SEL_BLOCK
    forced = (jl2 == 0) | (jl2 == cur) | (jl2 == cur - 1)
    score = jnp.where(forced, BIG, imp)
    score = jnp.where(jl2 * SEL_BLOCK <= past, score, NEG)
    _, idxs = _select_blocks(score, N_SEL - 1)
    lane = lax.broadcasted_iota(I32, (N_KV, LANE), 1)
    out_idx = jnp.zeros((N_KV, LANE), I32)
    for r, ix in enumerate(idxs):
        out_idx = jnp.where(lane == r, ix, out_idx)
    idx_ref[0] = jnp.concatenate([out_idx, jnp.zeros((8 - N_KV, LANE), I32)], axis=0)

    cw = cw_ref[0]
    il = lax.broadcasted_iota(I32, (N_HEADS, wbuf), 1)
    dq = wbuf - il
    s_w = _dot_nt(q, cw[:, 0:128].astype(BF16)) + _bias_from_dist(dq, rb)
    valid_w = dq < WINDOW
    s_w = jnp.where(valid_w, s_w, NEG)
    wn = wn_ref[0]
    qf = q.astype(F32)
    s_n = jnp.sum(qf * wn[:, 0:128].astype(BF16).astype(F32), axis=1, keepdims=True) + rb(0)
    mw = jnp.maximum(jnp.max(s_w, axis=1, keepdims=True), s_n)
    e_w = jnp.where(valid_w, jnp.exp(s_w - mw), 0.0)
    e_n = jnp.exp(s_n - mw)
    den_w = jnp.sum(e_w, axis=1, keepdims=True) + e_n
    o_w = (jnp.dot(e_w.astype(BF16), cw[:, 128:256].astype(BF16), preferred_element_type=F32)
           + e_n * wn[:, 128:256]) / den_w
    ocw_ref[0] = g_c * o_c + g_w * o_w


def _nsa_s1(q8, wsum2, cache_win3, winn, gates, rbt, phik, phiv, past):
    db = q8.shape[0]
    nb = wsum2.shape[1]
    wbuf = cache_win3.shape[1]
    body = functools.partial(_nsa_s1_body, past=past)
    return pl.pallas_call(
        body,
        grid=(db,),
        in_specs=[pl.BlockSpec((1, N_HEADS, LANE), lambda b: (b, 0, 0)),
                  pl.BlockSpec((1, nb, 512), lambda b: (b, 0, 0)),
                  pl.BlockSpec((1, wbuf, 256), lambda b: (b, 0, 0)),
                  pl.BlockSpec((1, 1, 256), lambda b: (b, 0, 0)),
                  pl.BlockSpec((1, 1, LANE), lambda b: (b, 0, 0)),
                  pl.BlockSpec((N_HEADS, N_BUCKETS), lambda b: (0, 0)),
                  pl.BlockSpec((LANE, LANE), lambda b: (0, 0)),
                  pl.BlockSpec((LANE, LANE), lambda b: (0, 0))],
        out_specs=[pl.BlockSpec((1, N_HEADS, LANE), lambda b: (b, 0, 0)),
                   pl.BlockSpec((1, 8, LANE), lambda b: (b, 0, 0))],
        out_shape=[jax.ShapeDtypeStruct((db, N_HEADS, LANE), F32),
                   jax.ShapeDtypeStruct((db, 8, LANE), I32)],
        compiler_params=_cparams(("arbitrary",)),
    )(q8, wsum2, cache_win3, winn, gates, rbt, phik, phiv)


def _nsa_s2_body(sel_ref, pt_ref, q_ref, blk0_ref, blk1_ref, kvn_ref, g_ref, rbt_ref, ocw_ref, o_ref,
                 acc_sc, m_sc, l_sc, *, past, npick):
    b, k = pl.program_id(0), pl.program_id(1)
    q = q_ref[0]
    rbt = rbt_ref[...]
    rb = lambda j: rbt[:, j:j + 1]
    kvn = kvn_ref[0]
    top = lax.broadcasted_iota(I32, (N_HEADS, 1), 0) < GQA

    @pl.when(k == 0)
    def _init():
        s_n = jnp.sum(q.astype(F32) * kvn[:, 256:384].astype(BF16).astype(F32), axis=1, keepdims=True) + rb(0)
        m_sc[...] = s_n
        l_sc[...] = jnp.ones(l_sc.shape, F32)
        acc_sc[...] = jnp.where(top, kvn[:, 384:384 + HEAD_DIM], kvn[:, 384 + HEAD_DIM:512])

    j0 = sel_ref[(b * N_KV + 0) * npick + k]
    j1 = sel_ref[(b * N_KV + 1) * npick + k]
    k0 = blk0_ref[0, 0, :, 0, 0, :].astype(BF16)
    v0 = blk0_ref[0, 0, :, 1, 0, :].astype(BF16)
    k1 = blk1_ref[0, 0, :, 0, 1, :].astype(BF16)
    v1 = blk1_ref[0, 0, :, 1, 1, :].astype(BF16)
    s = jnp.concatenate([_dot_nt(q[0:GQA, 0:HEAD_DIM], k0), _dot_nt(q[GQA:, HEAD_DIM:], k1)], axis=0)
    r = lax.broadcasted_iota(I32, (N_HEADS, SEL_BLOCK), 1)
    d = past - (jnp.where(top, j0, j1) * SEL_BLOCK + r)
    s = s + _bias_from_dist(d, rb)
    s = jnp.where(d >= 0, s, NEG)
    m_old = m_sc[...]
    m_new = jnp.maximum(m_old, jnp.max(s, axis=1, keepdims=True))
    alpha = jnp.exp(m_old - m_new)
    p = jnp.exp(s - m_new).astype(BF16)
    pv = jnp.concatenate([jnp.dot(p[0:GQA], v0, preferred_element_type=F32),
                          jnp.dot(p[GQA:], v1, preferred_element_type=F32)], axis=0)
    l_sc[...] = alpha * l_sc[...] + jnp.sum(p.astype(F32), axis=1, keepdims=True)
    acc_sc[...] = alpha * acc_sc[...] + pv
    m_sc[...] = m_new

    @pl.when(k == npick - 1)
    def _fin():
        g_s = _head_rows(pltpu.roll(g_ref[0], LANE - N_HEADS, 1))
        o = acc_sc[...] / l_sc[...]
        o_ref[0] = ocw_ref[0] + g_s * jnp.concatenate([o, o], axis=1)


def _nsa_s2(sel_flat, page_table, q8, cache_kv_l, kv4n, gates, rbt, ocw, past):
    db = q8.shape[0]
    npick = N_SEL - 1
    page = cache_kv_l.shape[2]
    bpp = page // SEL_BLOCK
    body = functools.partial(_nsa_s2_body, past=past, npick=npick)

    def blk_spec(h):
        def blk_map(b, k, sel, pt):
            j = sel[(b * N_KV + h) * npick + k]
            return (0, pt[b, j // bpp], j % bpp, 1, 0, 0)
        return pl.BlockSpec((1, 1, SEL_BLOCK, 2, N_KV, HEAD_DIM), blk_map)

    grid_spec = pltpu.PrefetchScalarGridSpec(
        num_scalar_prefetch=2,
        grid=(db, npick),
        in_specs=[pl.BlockSpec((1, N_HEADS, LANE), lambda b, k, sel, pt: (b, 0, 0)),
                  blk_spec(0), blk_spec(1),
                  pl.BlockSpec((1, 1, 512), lambda b, k, sel, pt: (b, 0, 0)),
                  pl.BlockSpec((1, 1, LANE), lambda b, k, sel, pt: (b, 0, 0)),
                  pl.BlockSpec((N_HEADS, N_BUCKETS), lambda b, k, sel, pt: (0, 0)),
                  pl.BlockSpec((1, N_HEADS, LANE), lambda b, k, sel, pt: (b, 0, 0))],
        out_specs=pl.BlockSpec((1, N_HEADS, LANE), lambda b, k, sel, pt: (b, 0, 0)),
        scratch_shapes=[pltpu.VMEM((N_HEADS, HEAD_DIM), F32),
                        pltpu.VMEM((N_HEADS, 1), F32),
                        pltpu.VMEM((N_HEADS, 1), F32)],
    )
    return pl.pallas_call(
        body,
        grid_spec=grid_spec,
        out_shape=jax.ShapeDtypeStruct((db, N_HEADS, LANE), F32),
        compiler_params=_cparams(("arbitrary", "arbitrary")),
    )(sel_flat, page_table, q8, cache_kv_l, cache_kv_l, kv4n, gates, rbt, ocw)


def _pool_body(x_ref, halo_ref, prev_ref, w_ref, sc_ref, o_ref, *, tm, pos_base):
    i = pl.program_id(1)
    halo = jnp.where(i == 0, prev_ref[0], halo_ref[0])
    cur = x_ref[0]
    ext = jnp.concatenate([halo, cur], axis=0)
    sums = {1: ext}
    w = 1
    while w < max(POOL_WINDOWS):
        sums[2 * w] = sums[w] + pltpu.roll(sums[w], w, 0)
        w *= 2
    pos = pos_base + i * tm + lax.broadcasted_iota(I32, (tm, 1), 0)
    outs = []
    for gi, wl in enumerate(POOL_WINDOWS):
        ch = slice(gi * POOL_GROUP, (gi + 1) * POOL_GROUP)
        tot = sums[wl][16:, ch]
        cnt = jnp.minimum(wl, pos + 1).astype(F32)
        d = tot / cnt - cur[:, ch]
        outs.append(jnp.dot(d.astype(BF16), w_ref[gi].astype(BF16), preferred_element_type=F32))
    o_ref[0] = jnp.concatenate(outs, axis=1) * sc_ref[...]


def _pool(p, prev16, pool_w, pool_scale, tm, pos_base):
    b, t, c = p.shape
    body = functools.partial(_pool_body, tm=tm, pos_base=pos_base)
    hb = tm // 16
    return pl.pallas_call(
        body,
        grid=(b, t // tm),
        in_specs=[pl.BlockSpec((1, tm, c), lambda bi, i: (bi, i, 0)),
                  pl.BlockSpec((1, 16, c), lambda bi, i: (bi, jnp.maximum(i * hb - 1, 0), 0)),
                  pl.BlockSpec((1, 16, c), lambda bi, i: (bi, 0, 0)),
                  pl.BlockSpec((len(POOL_WINDOWS), POOL_GROUP, POOL_GROUP), lambda bi, i: (0, 0, 0)),
                  pl.BlockSpec((1, c), lambda bi, i: (0, 0))],
        out_specs=pl.BlockSpec((1, tm, c), lambda bi, i: (bi, i, 0)),
        out_shape=jax.ShapeDtypeStruct((b, t, c), F32),
        compiler_params=_cparams(("arbitrary", "arbitrary")),
    )(p, p, prev16, pool_w, pool_scale.reshape(1, c))


def _rms(x, g):
    return x * lax.rsqrt(jnp.mean(x * x, axis=-1, keepdims=True) + EPS) * g


def _outproj_body(oa_ref, op_ref, x_ref, gate_ref, g_ref, wa_ref, wp_ref, o_ref):
    mix = (jnp.dot(oa_ref[0].astype(BF16), wa_ref[...], preferred_element_type=F32)
           + jnp.dot(op_ref[0].astype(BF16), wp_ref[...], preferred_element_type=F32))
    o_ref[0] = x_ref[0] + gate_ref[0] * _rms(mix, g_ref[...])


def _outproj(oa, op, x, gate, g, wa, wp, tm):
    bx, t, d = x.shape
    per_row = gate.shape[1] != 1
    ms = _mod_specs(t, tm, d, per_row)
    return pl.pallas_call(
        _outproj_body,
        grid=(bx, t // tm),
        in_specs=[pl.BlockSpec((1, tm, oa.shape[2]), lambda b, i: (b, i, 0)),
                  pl.BlockSpec((1, tm, op.shape[2]), lambda b, i: (b, i, 0)),
                  pl.BlockSpec((1, tm, d), lambda b, i: (b, i, 0)), ms,
                  pl.BlockSpec((1, d), lambda b, i: (0, 0)),
                  pl.BlockSpec(wa.shape, lambda b, i: (0, 0)),
                  pl.BlockSpec(wp.shape, lambda b, i: (0, 0))],
        out_specs=pl.BlockSpec((1, tm, d), lambda b, i: (b, i, 0)),
        out_shape=jax.ShapeDtypeStruct((bx, t, d), F32),
        compiler_params=_cparams(("arbitrary", "arbitrary")),
    )(oa, op, x, gate, g.reshape(1, d), wa, wp)


def _final_body(f_ref, x_ref, gate_ref, g_ref, o_ref):
    f = jnp.concatenate([f_ref[0, :, c, :] for c in range(f_ref.shape[2])], axis=1)
    o_ref[0] = x_ref[0] + gate_ref[0] * _rms(f, g_ref[...])


def _final(f, x, gate, g, tm):
    bx, t, d = x.shape
    per_row = gate.shape[1] != 1
    ms = _mod_specs(t, tm, d, per_row)
    xs = pl.BlockSpec((1, tm, d), lambda b, i: (b, i, 0))
    fs = pl.BlockSpec((1, tm, d // LANE, LANE), lambda b, i: (b, i, 0, 0))
    return pl.pallas_call(
        _final_body,
        grid=(bx, t // tm),
        in_specs=[fs, xs, ms, pl.BlockSpec((1, d), lambda b, i: (0, 0))],
        out_specs=xs,
        out_shape=jax.ShapeDtypeStruct((bx, t, d), F32),
        compiler_params=_cparams(("arbitrary", "arbitrary")),
    )(f, x, gate, g.reshape(1, d))


def _topk_rows(s, k):
    n = s.shape[0]
    row = lax.broadcasted_iota(I32, s.shape, 0)
    krow = lax.broadcasted_iota(I32, (k, s.shape[1]), 0)
    vals = jnp.zeros((k, s.shape[1]), F32)
    idxs = jnp.zeros((k, s.shape[1]), I32)
    for it in range(k):
        mx = jnp.max(s, axis=0, keepdims=True)
        ix = jnp.min(jnp.where(s == mx, row, n), axis=0, keepdims=True)
        vals = jnp.where(krow == it, mx, vals)
        idxs = jnp.where(krow == it, ix, idxs)
        s = jnp.where(row == ix, REMOVED, s)
    return vals, idxs


def _pick_rows(table, ix):
    k = table.shape[0]
    out = jnp.zeros(ix.shape, table.dtype)
    for a in range(k):
        out = jnp.where(ix == a, table[a:a + 1, :], out)
    return out


def _peer_topk_body(q_ref, keys_ref, idx_ref, gate_ref, e_sc):
    kk = PEER_TOPK

    def head(h, carry):
        s1 = _dot_nt(keys_ref[2 * h], q_ref[2 * h])
        s2 = _dot_nt(keys_ref[2 * h + 1], q_ref[2 * h + 1])
        v1, i1 = _topk_rows(s1, kk)
        v2, i2 = _topk_rows(s2, kk)
        cand = jnp.concatenate([v1[a:a + 1, :] + v2 for a in range(kk)], axis=0)
        best, flat = _topk_rows(cand, kk)
        e = _pick_rows(i1, flat // kk) * PEER_KEYS + _pick_rows(i2, flat % kk)
        ex = jnp.exp(best - jnp.max(best, axis=0, keepdims=True))
        gate = ex / jnp.sum(ex, axis=0, keepdims=True)
        off = pl.multiple_of(h * kk, kk)
        e_sc[pl.ds(off, kk), :] = e
        gate_ref[pl.ds(off, kk), :] = gate
        return carry

    lax.fori_loop(0, PEER_HEADS, head, 0)
    idx_ref[...] = e_sc[...].T


def _peer_topk(q16, keys16, tt):
    nq, n, _ = q16.shape
    rows = PEER_HEADS * PEER_TOPK
    return pl.pallas_call(
        _peer_topk_body,
        grid=(n // tt,),
        in_specs=[pl.BlockSpec((nq, tt, LANE), lambda i: (0, i, 0)),
                  pl.BlockSpec(keys16.shape, lambda i: (0, 0, 0))],
        out_specs=[pl.BlockSpec((tt, rows), lambda i: (i, 0)),
                   pl.BlockSpec((rows, tt), lambda i: (0, i))],
        out_shape=[jax.ShapeDtypeStruct((n, rows), I32),
                   jax.ShapeDtypeStruct((rows, n), F32)],
        scratch_shapes=[pltpu.VMEM((rows, tt), I32)],
        compiler_params=_cparams(("arbitrary",)),
    )(q16, keys16)


PAD_ROWS = 4
HALF_ROWS = 4
_SLOT_OF_ROW = (6, 2, 4, 0, 7, 3, 5, 1)


def _expert_slab(tab_ref, e, late):
    off = pl.multiple_of(e * HALF_ROWS + (0 if late else PAD_ROWS), HALF_ROWS)
    w = tab_ref[pl.ds(off, 8), :]
    lo = pltpu.bitcast(w << 16, F32)
    hi = pltpu.bitcast(w & jnp.uint32(0xFFFF0000), F32)
    return lo, hi


def _peer_u_body(idx_ref, x_ref, gate_ref, tab_ref, act_ref, *, tt):
    rows = PEER_HEADS * PEER_TOPK
    ngrp = rows // 8
    lane = lax.broadcasted_iota(I32, (8, tt), 1)
    sub = lax.broadcasted_iota(I32, (8, LANE), 0)
    m1 = (sub & 2) != 0
    m2 = (sub & 1) != 0
    low = sub < HALF_ROWS

    def token(t, accs):
        x8 = x_ref[t]
        xr = pltpu.roll(x8, HALF_ROWS, 0)
        xa = (jnp.where(low, x8, 0.0), jnp.where(low, 0.0, xr))
        xb = (jnp.where(low, xr, 0.0), jnp.where(low, 0.0, x8))
        new = []
        for gidx in range(ngrp):
            prod = [None] * 8
            for r in range(8):
                slot = _SLOT_OF_ROW[r]
                late = slot % 2
                lo, hi = _expert_slab(tab_ref, idx_ref[t, gidx * 8 + r], late)
                prod[slot] = lo * xa[late] + hi * xb[late]
            c = [prod[2 * k] + prod[2 * k + 1] for k in range(4)]
            d = [c[k] + pltpu.roll(c[k], 2 if k % 2 == 0 else 6, 0) for k in range(4)]
            e0, e1 = jnp.where(m1, d[0], d[1]), jnp.where(m1, d[2], d[3])
            f0 = e0 + pltpu.roll(e0, 1, 0)
            f1 = e1 + pltpu.roll(e1, 7, 0)
            dcol = jnp.sum(jnp.where(m2, f0, f1), axis=1, keepdims=True)
            new.append(jnp.where(lane == t, dcol, accs[gidx]))
        return tuple(new)

    accs = lax.fori_loop(0, tt, token, tuple(jnp.zeros((8, tt), F32) for _ in range(ngrp)))
    d = jnp.concatenate(accs, axis=0)
    act_ref[...] = (jax.nn.gelu(d, approximate=True) * gate_ref[...]).T


def _peer_u(idx, x3, gate, tab, tt):
    n, rows = idx.shape
    body = functools.partial(_peer_u_body, tt=tt)
    return pl.pallas_call(
        body,
        grid=(n // tt,),
        in_specs=[pl.BlockSpec((tt, rows), lambda i: (i, 0), memory_space=pltpu.SMEM),
                  pl.BlockSpec((tt, 8, LANE), lambda i: (i, 0, 0)),
                  pl.BlockSpec((rows, tt), lambda i: (0, i)),
                  pl.BlockSpec(tab.shape, lambda i: (0, 0), pipeline_mode=pl.Buffered(1))],
        out_specs=pl.BlockSpec((tt, rows), lambda i: (i, 0)),
        out_shape=jax.ShapeDtypeStruct((n, rows), F32),
        compiler_params=_cparams(("arbitrary",)),
    )(idx, x3, gate, tab)


def _peer_v_body(idx_ref, act_ref, tab_ref, f_ref, *, tt):
    rows = PEER_HEADS * PEER_TOPK
    sub = lax.broadcasted_iota(I32, (8, LANE), 0)

    def token(t, carry):
        al = [jnp.zeros((8, LANE), F32) for _ in range(2)]
        ah = [jnp.zeros((8, LANE), F32) for _ in range(2)]
        for r in range(rows):
            lo, hi = _expert_slab(tab_ref, idx_ref[t, r], False)
            a = act_ref[t, r]
            al[r % 2] = al[r % 2] + a * lo
            ah[r % 2] = ah[r % 2] + a * hi
        f_ref[t] = jnp.where(sub < HALF_ROWS, al[0] + al[1], pltpu.roll(ah[0] + ah[1], HALF_ROWS, 0))
        return carry

    lax.fori_loop(0, tt, token, 0)


def _peer_v(idx, act, tab, tt):
    n, rows = idx.shape
    body = functools.partial(_peer_v_body, tt=tt)
    return pl.pallas_call(
        body,
        grid=(n // tt,),
        in_specs=[pl.BlockSpec((tt, rows), lambda i: (i, 0), memory_space=pltpu.SMEM),
                  pl.BlockSpec((tt, rows), lambda i: (i, 0), memory_space=pltpu.SMEM),
                  pl.BlockSpec(tab.shape, lambda i: (0, 0), pipeline_mode=pl.Buffered(1))],
        out_specs=pl.BlockSpec((tt, 8, LANE), lambda i: (i, 0, 0)),
        out_shape=jax.ShapeDtypeStruct((n, 8, LANE), F32),
        compiler_params=_cparams(("arbitrary",)),
    )(idx, act, tab)


def _pack_table(tab):
    n, d = tab.shape
    bits = lax.bitcast_convert_type(tab.astype(jnp.bfloat16), jnp.uint16).astype(U32)
    words = (bits[:, :d // 2] | (bits[:, d // 2:] << 16)).reshape(n * HALF_ROWS, LANE)
    pad = jnp.zeros((PAD_ROWS, LANE), U32)
    return jnp.concatenate([pad, words, pad], axis=0)


def _peer(x1, sc, sh, g, wq, keys16, tab_u, tab_v, tm):
    bx, t, d = x1.shape
    n = bx * t
    q16, h2 = _peerq(x1, sc, sh, g, wq, tm)
    tt = min(128, n)
    idx, gate = _peer_topk(q16.reshape(2 * PEER_HEADS, n, LANE), keys16, tt)
    act = _peer_u(idx, h2.reshape(n, d // LANE, LANE), gate, tab_u, tt)
    f = _peer_v(idx, act, tab_v, tt)
    return f.reshape(bx, t, d // LANE, LANE)


def _pack_w_in(w_in):
    d = w_in.shape[0]
    scale = HEAD_DIM ** -0.5
    cols = []
    for hd in range(N_HEADS):
        blk = w_in[:, hd * HEAD_DIM:(hd + 1) * HEAD_DIM] * scale
        z = jnp.zeros((d, HEAD_DIM), w_in.dtype)
        cols += [blk, z] if hd // GQA == 0 else [z, blk]
    o0 = N_HEADS * HEAD_DIM
    o1 = o0 + 6 * N_KV * HEAD_DIM
    o2 = o1 + 3 * N_HEADS
    kvw = N_KV * HEAD_DIM
    wa = jnp.concatenate([w_in[:, o0:o1], w_in[:, o2:]], axis=1)
    cols.append(w_in[:, o0 + 3 * kvw:o0 + 4 * kvw])
    cols.append(w_in[:, o0 + 5 * kvw:o0 + 6 * kvw])
    cols.append(w_in[:, o1:o2])
    cols.append(jnp.zeros((d, LANE - 3 * N_HEADS), w_in.dtype))
    wbt = jnp.concatenate(cols, axis=1).T
    return wa.astype(BF16), wbt.astype(BF16)


def _pack_w_out(w_out):
    d = w_out.shape[1]
    rows = []
    for hd in range(N_HEADS):
        blk = w_out[hd * HEAD_DIM:(hd + 1) * HEAD_DIM]
        z = jnp.zeros((HEAD_DIM, d), w_out.dtype)
        rows += [blk, z] if hd // GQA == 0 else [z, blk]
    wa = jnp.concatenate(rows, axis=0).astype(BF16)
    wp = w_out[N_HEADS * HEAD_DIM:].astype(BF16)
    return wa, wp


def _pack_compress(alpha, phi):
    a = jnp.concatenate([alpha[0], alpha[0], alpha[1], alpha[1]], axis=1)
    z = jnp.zeros((HEAD_DIM, HEAD_DIM), phi.dtype)
    blocks = [phi[0], phi[0], phi[1], phi[1]]
    p = jnp.concatenate([jnp.concatenate([blocks[r] if c == r else z for c in range(4)], axis=1)
                         for r in range(4)], axis=0)
    return a, p


def kernel(x_prompt, x_sample, cache_kv, cache_win, state_pool, page_table, c_prompt, c_sample,
           w_ada, b_ada, norm_g, w_in, w_out, cmp_alpha, cmp_phi, rel_bias, pool_w, pool_scale,
           peer_wq, peer_keys, peer_u, peer_v):
    depth = w_ada.shape[0]
    assert depth == 1 and x_sample.shape[1] == 1
    l = 0
    b, s, d = x_prompt.shape
    db = x_sample.shape[0]
    n_pool, page = cache_kv.shape[1], cache_kv.shape[2]
    n_pages = page_table.shape[1]
    past = n_pages * page
    wbuf = cache_win.shape[2]
    tq = WINDOW // 2

    w_in_a, w_in_bt = _pack_w_in(w_in[l])
    wa, wp = _pack_w_out(w_out[l])
    a_cmp, phi_cmp = _pack_compress(cmp_alpha[l], cmp_phi[l])
    wq = peer_wq[l].astype(BF16)
    keys16 = peer_keys[l].reshape(2 * PEER_HEADS, PEER_KEYS, LANE).astype(BF16)
    tab_u = _pack_table(peer_u[l])
    tab_v = _pack_table(peer_v[l])
    g = norm_g[l]

    c_all = jnp.concatenate([c_prompt, c_sample], axis=0)
    mod = _mod(c_all, w_ada[l], b_ada[l]).reshape(b + db, 6, d)
    mod_p = [mod[:b, k][:, None, :] for k in range(6)]
    mod_s = [mod[b:, k][None, :, :] for k in range(6)]

    tm = min(512, s)
    kv4, kk, win, pool_in, q_t, gates_t, v_t = _inproj(x_prompt, mod_p[1], mod_p[0], g[0], w_in_a, w_in_bt, tm, tq)
    kcvc = _compress(kv4, a_cmp, phi_cmp, min(1024, s))
    o_attn = _nsa_prompt(q_t, kk, v_t, kcvc.reshape(b, s // SEL_BLOCK, 512), gates_t, rel_bias, tq)
    prev0 = jnp.zeros((b, 16, POOL_DIM), F32)
    o_pool = _pool(pool_in, prev0, pool_w[l], pool_scale[l], min(1024, s), 0)
    x1 = _outproj(o_attn, o_pool, x_prompt, mod_p[2], g[1], wa, wp, tm)
    f = _peer(x1, mod_p[4], mod_p[3], g[2], wq, keys16, tab_u, tab_v, tm)
    y_prompt = _final(f, x1, mod_p[5], g[3], tm)

    kv_prompt = kv4.reshape(1, b, s, 4, N_KV, HEAD_DIM)
    win_prompt = win[:, s - min(WINDOW, s):].reshape(1, b, min(WINDOW, s), 2, N_KV, HEAD_DIM)
    pool_prompt = pool_in[:, s - POOL_STATE:][None]

    xs = x_sample.reshape(1, db, d)
    kv4_s, _, win_s, pool_s, q_t_s, gates_t_s = _inproj(xs, mod_s[1], mod_s[0], g[0], w_in_a, w_in_bt, db, None)
    cache_l = cache_kv
    wsum = _wsum_paged(cache_l, page_table, cmp_alpha[l])
    q8 = q_t_s[0].T.reshape(db, N_HEADS, LANE)
    kv4n = kv4_s.reshape(db, 1, 512)
    winn = win_s.reshape(db, 1, 256)
    gat = gates_t_s[0].T.reshape(db, 1, LANE)
    cw3 = cache_win[l].reshape(db, wbuf, 256)
    rbt = rel_bias.T
    ocw, sel_idx = _nsa_s1(q8, wsum.reshape(db, past // SEL_BLOCK, 512), cw3, winn, gat, rbt,
                           phi_cmp[:LANE, :LANE], phi_cmp[LANE:, LANE:], past)
    sel_flat = sel_idx[:, :N_KV, :N_SEL - 1].reshape(-1)
    o_attn_s = _nsa_s2(sel_flat, page_table, q8, cache_l, kv4n, gat, rbt, ocw, past)
    ext = jnp.concatenate([jnp.zeros((db, 1, POOL_DIM), F32), state_pool[l], pool_s.reshape(db, 1, POOL_DIM)], axis=1)
    o_pool_s = _pool(ext[:, 1:], jnp.zeros((db, 16, POOL_DIM), F32), pool_w[l], pool_scale[l], 16,
                     past - POOL_STATE)[:, POOL_STATE:]
    x1s = _outproj(o_attn_s.reshape(1, db, N_HEADS * LANE).astype(BF16), o_pool_s.reshape(1, db, POOL_DIM),
                   xs, mod_s[2], g[1], wa, wp, db)
    fs = _peer(x1s, mod_s[4], mod_s[3], g[2], wq, keys16, tab_u, tab_v, db)
    y_sample = _final(fs, x1s, mod_s[5], g[3], db).reshape(db, 1, d)

    kv_sample = kv4_s.reshape(1, db, 1, 4, N_KV, HEAD_DIM)
    win_sample = jnp.concatenate([cache_win[l][:, 1:], win_s.reshape(db, 1, 2, N_KV, HEAD_DIM)], axis=1)[None]
    pool_sample = ext[:, 2:][None]
    return (y_prompt, y_sample, kv_prompt, kv_sample, win_prompt, win_sample, pool_prompt, pool_sample)
```

```python
import functools
import math

import numpy as np
import jax
import jax.numpy as jnp
from jax import lax
from jax.experimental import pallas as pl
from jax.experimental.pallas import tpu as pltpu

F32, BF16, I32, U32 = jnp.float32, jnp.bfloat16, jnp.int32, jnp.uint32

D_MODEL = 1024
N_HEADS = 8
HEAD_DIM = 64
N_KV = 2
GQA = 4
CMP_BLOCK = 32
SEL_BLOCK = 64
N_SEL = 16
WINDOW = 512
N_BUCKETS = 32
REL_MAX_DIST = 128
POOL_WINDOWS = (2, 4, 8, 16)
POOL_DIM = 512
POOL_GROUP = 128
POOL_STATE = 15
PEER_HEADS = 8
PEER_KEYS = 128
PEER_TOPK = 16
EPS = 1e-6
LANE = 128
NEG = -1e30
BIG = 1e30
REMOVED = -3e38
VMEM_LIMIT = 56 * 1024 * 1024

A_KV4, A_WIN, A_POOL, A_END = 0, 512, 768, 1280
B_Q, B_V, B_GATE, B_END = 0, 1024, 1280, 1408


def _cparams(sem):
    return pltpu.CompilerParams(dimension_semantics=sem, vmem_limit_bytes=VMEM_LIMIT)


def _bucket_thresholds():
    n = np.arange(0, 4 * REL_MAX_DIST)
    exact = N_BUCKETS // 2
    ratio = np.log(np.maximum(n, exact).astype(np.float32) / np.float32(exact)) / np.float32(
        math.log(REL_MAX_DIST / exact))
    large = np.minimum(exact + (ratio * np.float32(N_BUCKETS - exact)).astype(np.int32), N_BUCKETS - 1)
    b = np.where(n < exact, n, large)
    return [int(np.argmax(b >= j)) for j in range(N_BUCKETS)]


_THR = _bucket_thresholds()
FAR_DIST = _THR[N_BUCKETS - 1]


def _bias_from_dist(d, rb):
    val = jnp.full(d.shape, rb(0), F32)
    for j in range(1, N_BUCKETS):
        if _THR[j] == _THR[j - 1] and j > 1:
            continue
        jj = j
        while jj + 1 < N_BUCKETS and _THR[jj + 1] == _THR[j]:
            jj += 1
        val = jnp.where(d >= _THR[j], rb(jj), val)
    return val


def _dot_nt(a, b):
    return lax.dot_general(a, b, (((1,), (1,)), ((), ())), preferred_element_type=F32)


def _mod_body(c_ref, w_ref, b_ref, o_ref):
    c = c_ref[...]
    a = (c * jax.nn.sigmoid(c)).astype(BF16)
    o_ref[...] = jnp.dot(a, w_ref[...].astype(BF16), preferred_element_type=F32) + b_ref[...]


def _mod(c, w_ada, b_ada):
    r, d = c.shape
    n = w_ada.shape[1]
    tn = 1536
    return pl.pallas_call(
        _mod_body,
        grid=(n // tn,),
        in_specs=[pl.BlockSpec((r, d), lambda j: (0, 0)),
                  pl.BlockSpec((d, tn), lambda j: (0, j)),
                  pl.BlockSpec((1, tn), lambda j: (0, j))],
        out_specs=pl.BlockSpec((r, tn), lambda j: (0, j)),
        out_shape=jax.ShapeDtypeStruct((r, n), F32),
        compiler_params=_cparams(("arbitrary",)),
    )(c, w_ada, b_ada.reshape(1, n))


def _norm_mod(x, g, sc, sh):
    ms = jnp.mean(x * x, axis=-1, keepdims=True)
    h = x * lax.rsqrt(ms + EPS) * g
    return h * (1.0 + sc) + sh


def _inproj_body(x_ref, sc_ref, sh_ref, g_ref, wa_ref, wb_ref, kv_ref, kk_ref, win_ref, pool_ref,
                 qt_ref, gt_ref, *vt_refs, tq):
    h = _norm_mod(x_ref[0], g_ref[...], sc_ref[0], sh_ref[0]).astype(BF16)
    u = jnp.dot(h, wa_ref[...], preferred_element_type=F32)
    kv_ref[0] = u[:, A_KV4:A_WIN]
    win_ref[0] = u[:, A_WIN:A_POOL]
    pool_ref[0] = u[:, A_POOL:A_END]
    kk_ref[0, :, 0:LANE] = u[:, A_KV4 + 2 * LANE:A_KV4 + 3 * LANE].astype(BF16)
    kk_ref[0, :, LANE:2 * LANE] = u[:, A_WIN:A_WIN + LANE].astype(BF16)
    ut = _dot_nt(wb_ref[...], h)
    qt_ref[0] = ut[B_Q:B_V].astype(BF16)
    gt_ref[0] = jax.nn.sigmoid(ut[B_GATE:B_END])
    for vt_ref in vt_refs:
        for c in range(ut.shape[1] // tq):
            vt_ref[0, c] = ut[B_V:B_GATE, c * tq:(c + 1) * tq].astype(BF16)


def _mod_specs(t, tm, d, per_row):
    if per_row:
        return pl.BlockSpec((1, tm, d), lambda b, i: (b, i, 0))
    return pl.BlockSpec((1, 1, d), lambda b, i: (b, 0, 0))


def _inproj(x, sc, sh, g, wa, wbt, tm, tq):
    bx, t, d = x.shape
    per_row = sc.shape[1] != 1
    ms = _mod_specs(t, tm, d, per_row)
    widths = (512, 256, 256, 512)
    dtypes = (F32, BF16, F32, F32)
    out_specs = [pl.BlockSpec((1, tm, wd), lambda b, i: (b, i, 0)) for wd in widths]
    out_shape = [jax.ShapeDtypeStruct((bx, t, wd), dt) for wd, dt in zip(widths, dtypes)]
    out_specs += [pl.BlockSpec((1, B_V - B_Q, tm), lambda b, i: (b, 0, i)),
                  pl.BlockSpec((1, B_END - B_GATE, tm), lambda b, i: (b, 0, i))]
    out_shape += [jax.ShapeDtypeStruct((bx, B_V - B_Q, t), BF16),
                  jax.ShapeDtypeStruct((bx, B_END - B_GATE, t), F32)]
    if tq is not None:
        out_specs.append(pl.BlockSpec((1, tm // tq, B_GATE - B_V, tq), lambda b, i: (b, i, 0, 0)))
        out_shape.append(jax.ShapeDtypeStruct((bx, t // tq, B_GATE - B_V, tq), BF16))
    return pl.pallas_call(
        functools.partial(_inproj_body, tq=tq),
        grid=(bx, t // tm),
        in_specs=[pl.BlockSpec((1, tm, d), lambda b, i: (b, i, 0)), ms, ms,
                  pl.BlockSpec((1, d), lambda b, i: (0, 0)),
                  pl.BlockSpec(wa.shape, lambda b, i: (0, 0)),
                  pl.BlockSpec(wbt.shape, lambda b, i: (0, 0))],
        out_specs=out_specs,
        out_shape=out_shape,
        compiler_params=_cparams(("arbitrary", "arbitrary")),
    )(x, sc, sh, g.reshape(1, d), wa, wbt)


def _peerq_body(x_ref, sc_ref, sh_ref, g_ref, w_ref, q_ref, h_ref):
    h = _norm_mod(x_ref[0], g_ref[...], sc_ref[0], sh_ref[0])
    for c in range(h.shape[1] // LANE):
        h_ref[0, :, c, :] = h[:, c * LANE:(c + 1) * LANE]
    u = jnp.dot(h.astype(BF16), w_ref[...], preferred_element_type=F32)
    for j in range(2 * PEER_HEADS):
        q_ref[j, 0] = u[:, j * LANE:(j + 1) * LANE].astype(BF16)


def _peerq(x, sc, sh, g, w, tm):
    bx, t, d = x.shape
    per_row = sc.shape[1] != 1
    ms = _mod_specs(t, tm, d, per_row)
    nq = 2 * PEER_HEADS
    return pl.pallas_call(
        _peerq_body,
        grid=(bx, t // tm),
        in_specs=[pl.BlockSpec((1, tm, d), lambda b, i: (b, i, 0)), ms, ms,
                  pl.BlockSpec((1, d), lambda b, i: (0, 0)),
                  pl.BlockSpec((d, nq * LANE), lambda b, i: (0, 0))],
        out_specs=[pl.BlockSpec((nq, 1, tm, LANE), lambda b, i: (0, b, i, 0)),
                   pl.BlockSpec((1, tm, d // LANE, LANE), lambda b, i: (b, i, 0, 0))],
        out_shape=[jax.ShapeDtypeStruct((nq, bx, t, LANE), BF16),
                   jax.ShapeDtypeStruct((bx, t, d // LANE, LANE), F32)],
        compiler_params=_cparams(("arbitrary", "arbitrary")),
    )(x, sc, sh, g.reshape(1, d), w)


def _compress_rows(rows, a, phi):
    n = rows.shape[0] // CMP_BLOCK
    w = jnp.sum(rows.reshape(n, CMP_BLOCK, rows.shape[1]) * a[None], axis=1)
    return jnp.dot(w, phi, preferred_element_type=F32, precision=lax.Precision.HIGHEST)


def _compress_body(x_ref, a_ref, phi_ref, o_ref):
    o_ref[0] = _compress_rows(x_ref[0], a_ref[...], phi_ref[...])


def _compress(kv4, a, phi, tm):
    b, s, _ = kv4.shape
    return pl.pallas_call(
        _compress_body,
        grid=(b, s // tm),
        in_specs=[pl.BlockSpec((1, tm, 256), lambda bi, i: (bi, i, 0)),
                  pl.BlockSpec((CMP_BLOCK, 256), lambda bi, i: (0, 0)),
                  pl.BlockSpec((256, 256), lambda bi, i: (0, 0))],
        out_specs=pl.BlockSpec((1, tm // CMP_BLOCK, 256), lambda bi, i: (bi, i, 0)),
        out_shape=jax.ShapeDtypeStruct((b, s // CMP_BLOCK, 256), F32),
        compiler_params=_cparams(("arbitrary", "arbitrary")),
    )(kv4, a, phi)


PAGES_PER_STEP = 4


def _wsum_paged_body(pt_ref, *refs):
    page_refs, a_ref, o_ref = refs[:PAGES_PER_STEP], refs[PAGES_PER_STEP], refs[PAGES_PER_STEP + 1]
    a = a_ref[...]
    page = a.shape[1]
    per = page // CMP_BLOCK
    nrow = PAGES_PER_STEP * per
    rowi = lax.broadcasted_iota(I32, (nrow, page), 0)
    posb = lax.broadcasted_iota(I32, (nrow, page), 1) // CMP_BLOCK
    acc = jnp.zeros((nrow, a.shape[0]), F32)
    for p, p_ref in enumerate(page_refs):
        z = p_ref[0, 0].reshape(a.shape) * a
        z_hi = z.astype(BF16)
        z_lo = (z - z_hi.astype(F32)).astype(BF16)
        member = jnp.where(rowi == p * per + posb, 1.0, 0.0).astype(BF16)
        acc = acc + _dot_nt(member, z_hi) + _dot_nt(member, z_lo)
    o_ref[0] = acc


def _wsum_paged(cache_t, page_table, alpha):
    db, n_pages = page_table.shape
    page = cache_t.shape[5]
    per = page // CMP_BLOCK
    at = jnp.tile(jnp.transpose(alpha, (0, 2, 1)), (1, 1, per))
    a_rows = jnp.broadcast_to(at[:, None], (2, N_KV, HEAD_DIM, page)).reshape(2 * N_KV * HEAD_DIM, page)

    def page_spec(r):
        return pl.BlockSpec((1, 1, 2, N_KV, HEAD_DIM, page),
                            lambda b, i, pt: (0, pt[b, PAGES_PER_STEP * i + r], 0, 0, 0, 0))

    grid_spec = pltpu.PrefetchScalarGridSpec(
        num_scalar_prefetch=1,
        grid=(db, n_pages // PAGES_PER_STEP),
        in_specs=[page_spec(r) for r in range(PAGES_PER_STEP)]
        + [pl.BlockSpec(a_rows.shape, lambda b, i, pt: (0, 0))],
        out_specs=pl.BlockSpec((1, PAGES_PER_STEP * per, a_rows.shape[0]), lambda b, i, pt: (b, i, 0)),
    )
    return pl.pallas_call(
        _wsum_paged_body,
        grid_spec=grid_spec,
        out_shape=jax.ShapeDtypeStruct((db, n_pages * per, a_rows.shape[0]), F32),
        compiler_params=_cparams(("arbitrary", "arbitrary")),
    )(page_table, *([cache_t] * PAGES_PER_STEP), a_rows)


def _select_blocks(score, n_pick):
    lane = lax.broadcasted_iota(I32, score.shape, 1)
    nl = score.shape[1]
    sel = jnp.zeros(score.shape, jnp.bool_)
    idxs = []
    for _ in range(n_pick):
        mx = jnp.max(score, axis=1, keepdims=True)
        idx = jnp.min(jnp.where(score == mx, lane, nl), axis=1, keepdims=True)
        hit = lane == idx
        sel = sel | hit
        score = jnp.where(hit, REMOVED, score)
        idxs.append(idx)
    return sel, idxs


def _flash_update(slot, qt, kc, vt, bias, mask, acc_sc, m_sc):
    s = jnp.dot(kc, qt, preferred_element_type=F32)
    if bias is not None:
        s = s + bias
    if mask is not None:
        s = jnp.where(mask, s, NEG)
    m_old = m_sc[slot]
    m_new = jnp.maximum(m_old, jnp.max(s, axis=0, keepdims=True))
    alpha = jnp.exp(m_old - m_new)
    p = jnp.exp(s - m_new)
    acc_sc[slot] = alpha * acc_sc[slot] + jnp.dot(vt, p.astype(BF16), preferred_element_type=F32)
    m_sc[slot] = m_new


def _select_rows(score, n_pick):
    row = lax.broadcasted_iota(I32, score.shape, 0)
    n = score.shape[0]
    sel = jnp.zeros(score.shape, jnp.bool_)
    for _ in range(n_pick):
        mx = jnp.max(score, axis=0, keepdims=True)
        idx = jnp.min(jnp.where(score == mx, row, n), axis=0, keepdims=True)
        hit = row == idx
        sel = sel | hit
        score = jnp.where(hit, REMOVED, score)
    return sel


def _nsa_body(rb_ref, q_ref, kk_ref, vt_ref, kc_ref, g_ref, o_ref,
              bdiag_sc, bcmp_sc, acc_sc, m_sc, osum_sc, *, seq, tq):
    i = pl.program_id(1)
    q0 = i * tq
    nb = seq // SEL_BLOCK
    per = tq // SEL_BLOCK
    ncopy = bcmp_sc.shape[2]

    @pl.when((pl.program_id(0) == 0) & (i == 0))
    def _build_bias_tiles():
        k = lax.broadcasted_iota(I32, (tq, tq), 0)
        t = lax.broadcasted_iota(I32, (tq, tq), 1)
        rr = lax.broadcasted_iota(I32, (2 * nb, tq), 0) - nb
        tc = lax.broadcasted_iota(I32, (2 * nb, tq), 1)
        for hd in range(N_HEADS):
            rb = lambda j, hd=hd: rb_ref[j, hd]
            kvh, cols = hd // GQA, slice((hd % GQA) * tq, (hd % GQA + 1) * tq)
            for dl in range(2):
                bdiag_sc[kvh, dl, :, cols] = _bias_from_dist(t - k + dl * tq, rb) - rb(N_BUCKETS - 1)
            for par in range(2):
                for cp in range(ncopy):
                    bcmp_sc[kvh, par, cp, :, cols] = _bias_from_dist(
                        tc - (SEL_BLOCK * (rr + cp * per) + (par + 1) * CMP_BLOCK - 1), rb)

    gw = GQA * tq
    kc2 = kc_ref[0]
    kce, kco = kc2[:, 0:128].astype(BF16), kc2[:, 256:384].astype(BF16)
    vcet, vcot = kc2[:, 128:256].T.astype(BF16), kc2[:, 384:512].T.astype(BF16)
    qpos = q0 + lax.broadcasted_iota(I32, (nb, tq), 1)
    jl = lax.broadcasted_iota(I32, (nb, tq), 0)
    cur = qpos // SEL_BLOCK
    forced = (jl == 0) | (jl == cur) | (jl == cur - 1)
    causal_blk = jl * SEL_BLOCK <= qpos
    qpos_g = q0 + lax.broadcasted_iota(I32, (nb, gw), 1) % tq
    jl_g = lax.broadcasted_iota(I32, (nb, gw), 0)
    valid_e = qpos_g >= SEL_BLOCK * jl_g + CMP_BLOCK - 1
    valid_o = qpos_g >= SEL_BLOCK * jl_g + 2 * CMP_BLOCK - 1
    cmp_copy = (ncopy - i % ncopy) % ncopy
    cmp_start = pl.multiple_of(nb - (i + cmp_copy) * per, 8)
    kk = lax.broadcasted_iota(I32, (tq, gw), 0)
    tt = lax.broadcasted_iota(I32, (tq, gw), 1) % tq
    causal = tt >= kk
    gates = g_ref[0]
    ek = lax.broadcasted_iota(I32, (tq, nb), 0) // SEL_BLOCK
    ej = lax.broadcasted_iota(I32, (tq, nb), 1)

    def reset_state():
        m_sc[...] = jnp.full(m_sc.shape, NEG, F32)
        acc_sc[...] = jnp.zeros(acc_sc.shape, F32)

    def qgroup(kvh):
        return jnp.concatenate([q_ref[0, hd * LANE:(hd + 1) * LANE, :]
                                for hd in range(kvh * GQA, (kvh + 1) * GQA)], axis=1)

    vrow = lax.broadcasted_iota(I32, (LANE, tq), 0) // HEAD_DIM

    def kv_chunk(c, branch, kvh):
        start = pl.multiple_of(c * tq, tq)
        v = vt_ref[0, c, branch * LANE:(branch + 1) * LANE, :]
        return (kk_ref[0, pl.ds(start, tq), branch * LANE:(branch + 1) * LANE],
                jnp.where(vrow == kvh, v, jnp.ones_like(v)))

    def finish(kvh, branch):
        lrow = (1 - kvh) * HEAD_DIM
        acc = acc_sc[0]
        o = acc / acc[lrow:lrow + 1, :]
        for g in range(GQA):
            hd = kvh * GQA + g
            r = branch * N_HEADS + hd
            osum_sc[hd] += gates[r:r + 1, :] * o[:, g * tq:(g + 1) * tq]

    for kvh in range(N_KV):
        qt = qgroup(kvh)
        s_e = jnp.dot(kce, qt, preferred_element_type=F32) + bcmp_sc[kvh, 0, cmp_copy, pl.ds(cmp_start, nb), :]
        s_o = jnp.dot(kco, qt, preferred_element_type=F32) + bcmp_sc[kvh, 1, cmp_copy, pl.ds(cmp_start, nb), :]
        s_e = jnp.where(valid_e, s_e, NEG)
        s_o = jnp.where(valid_o, s_o, NEG)
        mx = jnp.maximum(jnp.max(s_e, axis=0, keepdims=True), jnp.max(s_o, axis=0, keepdims=True))
        e_e = jnp.where(valid_e, jnp.exp(s_e - mx), 0.0)
        e_o = jnp.where(valid_o, jnp.exp(s_o - mx), 0.0)
        den = jnp.maximum(jnp.sum(e_e, axis=0, keepdims=True) + jnp.sum(e_o, axis=0, keepdims=True), 1e-30)
        p_e, p_o = e_e / den, e_o / den
        pe = p_e + p_o
        imp = (pe[:, 0:tq] + pe[:, tq:2 * tq]) + (pe[:, 2 * tq:3 * tq] + pe[:, 3 * tq:4 * tq])
        o_c = (jnp.dot(vcet, p_e.astype(BF16), preferred_element_type=F32)
               + jnp.dot(vcot, p_o.astype(BF16), preferred_element_type=F32))
        for g in range(GQA):
            hd = kvh * GQA + g
            osum_sc[hd] = gates[hd:hd + 1, :] * o_c[:, g * tq:(g + 1) * tq]

        score = jnp.where(forced, BIG, imp)
        score = jnp.where(causal_blk, score, NEG)
        selm = jnp.where(_select_rows(score, min(N_SEL, nb)), 1.0, 0.0).astype(BF16)
        selm = jnp.concatenate([selm] * GQA, axis=1)

        def sel_mask(c):
            expand = jnp.where(ej == ek + c * per, 1.0, 0.0).astype(BF16)
            return jnp.dot(expand, selm, preferred_element_type=F32) > 0.5

        reset_state()

        def far_chunk(c, carry):
            kc, vt = kv_chunk(c, 0, kvh)
            _flash_update(0, qgroup(kvh), kc, vt, None, sel_mask(c), acc_sc, m_sc)
            return carry

        lax.fori_loop(0, jnp.maximum(i - 1, 0), far_chunk, 0)

        @pl.when(i >= 1)
        def _prev_chunk():
            kc, vt = kv_chunk(i - 1, 0, kvh)
            _flash_update(0, qt, kc, vt, bdiag_sc[kvh, 1], sel_mask(i - 1), acc_sc, m_sc)

        kc, vt = kv_chunk(i, 0, kvh)
        _flash_update(0, qt, kc, vt, bdiag_sc[kvh, 0], sel_mask(i) & causal, acc_sc, m_sc)
        finish(kvh, 1)

        reset_state()

        @pl.when(i >= 2)
        def _win_far():
            kc, vt = kv_chunk(i - 2, 1, kvh)
            _flash_update(0, qt, kc, vt, None, kk > tt, acc_sc, m_sc)

        @pl.when(i >= 1)
        def _win_prev():
            kc, vt = kv_chunk(i - 1, 1, kvh)
            _flash_update(0, qt, kc, vt, bdiag_sc[kvh, 1], None, acc_sc, m_sc)

        kc, vt = kv_chunk(i, 1, kvh)
        _flash_update(0, qt, kc, vt, bdiag_sc[kvh, 0], causal, acc_sc, m_sc)
        finish(kvh, 2)

    for hd in range(N_HEADS):
        o_ref[0, :, hd * LANE:(hd + 1) * LANE] = osum_sc[hd].T.astype(BF16)


def _nsa_prompt(qt, kk, vt, kcvc2, gates_t, rel_bias, tq):
    b, _, s = qt.shape
    nb = s // SEL_BLOCK
    assert WINDOW == 2 * tq and tq + 1 >= FAR_DIST and s % tq == 0
    body = functools.partial(_nsa_body, seq=s, tq=tq)
    return pl.pallas_call(
        body,
        grid=(b, s // tq),
        in_specs=[pl.BlockSpec(memory_space=pltpu.SMEM),
                  pl.BlockSpec((1, N_HEADS * LANE, tq), lambda bi, i: (bi, 0, i)),
                  pl.BlockSpec((1, s, 2 * LANE), lambda bi, i: (bi, 0, 0)),
                  pl.BlockSpec((1, s // tq, 2 * LANE, tq), lambda bi, i: (bi, 0, 0, 0)),
                  pl.BlockSpec((1, nb, 512), lambda bi, i: (bi, 0, 0)),
                  pl.BlockSpec((1, LANE, tq), lambda bi, i: (bi, 0, i))],
        out_specs=pl.BlockSpec((1, tq, N_HEADS * LANE), lambda bi, i: (bi, i, 0)),
        out_shape=jax.ShapeDtypeStruct((b, s, N_HEADS * LANE), BF16),
        scratch_shapes=[pltpu.VMEM((N_KV, 2, tq, GQA * tq), F32),
                        pltpu.VMEM((N_KV, 2, max(1, 8 // (tq // SEL_BLOCK)), 2 * nb, GQA * tq), F32),
                        pltpu.VMEM((1, LANE, GQA * tq), F32),
                        pltpu.VMEM((1, 1, GQA * tq), F32),
                        pltpu.VMEM((N_HEADS, LANE, tq), F32)],
        compiler_params=_cparams(("arbitrary", "arbitrary")),
    )(rel_bias, qt, kk, vt, kcvc2, gates_t)


def _head_rows(x):
    sub = lax.broadcasted_iota(I32, (N_HEADS, LANE), 0)
    lane = lax.broadcasted_iota(I32, (N_HEADS, LANE), 1)
    return jnp.sum(jnp.where(sub == lane, jnp.broadcast_to(x, (N_HEADS, LANE)), 0.0), axis=1, keepdims=True)


def _nsa_s1_body(q_ref, kc_ref, cw_ref, wn_ref, g_ref, rbt_ref, phik_ref, phiv_ref, ocw_ref, idx_ref, *, past):
    nb = kc_ref.shape[1]
    wbuf = cw_ref.shape[1]
    q = q_ref[0]
    rbt = rbt_ref[...]
    rb = lambda j: rbt[:, j:j + 1]
    gates = g_ref[0]
    g_c = _head_rows(gates[:, 0:LANE])
    g_w = _head_rows(pltpu.roll(gates, LANE - 2 * N_HEADS, 1))

    ws = kc_ref[0]
    hi_dot = functools.partial(jnp.dot, preferred_element_type=F32, precision=lax.Precision.HIGHEST)
    phik, phiv = phik_ref[...], phiv_ref[...]
    kc2 = jnp.concatenate([hi_dot(ws[:, 0:128], phik), hi_dot(ws[:, 128:256], phiv),
                           hi_dot(ws[:, 256:384], phik), hi_dot(ws[:, 384:512], phiv)], axis=1)
    jl = lax.broadcasted_iota(I32, (N_HEADS, nb), 1)
    d_e = past - (SEL_BLOCK * jl + CMP_BLOCK - 1)
    d_o = past - (SEL_BLOCK * jl + 2 * CMP_BLOCK - 1)
    s_e = _dot_nt(q, kc2[:, 0:128].astype(BF16)) + _bias_from_dist(d_e, rb)
    s_o = _dot_nt(q, kc2[:, 256:384].astype(BF16)) + _bias_from_dist(d_o, rb)
    valid_e, valid_o = d_e >= 0, d_o >= 0
    s_e = jnp.where(valid_e, s_e, NEG)
    s_o = jnp.where(valid_o, s_o, NEG)
    mx = jnp.maximum(jnp.max(s_e, axis=1, keepdims=True), jnp.max(s_o, axis=1, keepdims=True))
    e_e = jnp.where(valid_e, jnp.exp(s_e - mx), 0.0)
    e_o = jnp.where(valid_o, jnp.exp(s_o - mx), 0.0)
    den = jnp.maximum(jnp.sum(e_e, axis=1, keepdims=True) + jnp.sum(e_o, axis=1, keepdims=True), 1e-30)
    p_e, p_o = e_e / den, e_o / den
    o_c = (jnp.dot(p_e.astype(BF16), kc2[:, 128:256].astype(BF16), preferred_element_type=F32)
           + jnp.dot(p_o.astype(BF16), kc2[:, 384:512].astype(BF16), preferred_element_type=F32))

    pe = p_e + p_o
    imp = jnp.concatenate([jnp.sum(pe[0:GQA], axis=0, keepdims=True),
                           jnp.sum(pe[GQA:2 * GQA], axis=0, keepdims=True)], axis=0)
    jl2 = lax.broadcasted_iota(I32, (N_KV, nb), 1)
    cur = past // SEL_BLOCK
    forced = (jl2 == 0) | (jl2 == cur) | (jl2 == cur - 1)
    score = jnp.where(forced, BIG, imp)
    score = jnp.where(jl2 * SEL_BLOCK <= past, score, NEG)
    _, idxs = _select_blocks(score, N_SEL - 1)
    lane = lax.broadcasted_iota(I32, (N_KV, LANE), 1)
    out_idx = jnp.zeros((N_KV, LANE), I32)
    for r, ix in enumerate(idxs):
        out_idx = jnp.where(lane == r, ix, out_idx)
    idx_ref[0] = jnp.concatenate([out_idx, jnp.zeros((8 - N_KV, LANE), I32)], axis=0)

    cw = cw_ref[0]
    il = lax.broadcasted_iota(I32, (N_HEADS, wbuf), 1)
    dq = wbuf - il
    s_w = _dot_nt(q, cw[:, 0:128].astype(BF16)) + _bias_from_dist(dq, rb)
    valid_w = dq < WINDOW
    s_w = jnp.where(valid_w, s_w, NEG)
    wn = wn_ref[0]
    qf = q.astype(F32)
    s_n = jnp.sum(qf * wn[:, 0:128].astype(BF16).astype(F32), axis=1, keepdims=True) + rb(0)
    mw = jnp.maximum(jnp.max(s_w, axis=1, keepdims=True), s_n)
    e_w = jnp.where(valid_w, jnp.exp(s_w - mw), 0.0)
    e_n = jnp.exp(s_n - mw)
    den_w = jnp.sum(e_w, axis=1, keepdims=True) + e_n
    o_w = (jnp.dot(e_w.astype(BF16), cw[:, 128:256].astype(BF16), preferred_element_type=F32)
           + e_n * wn[:, 128:256]) / den_w
    ocw_ref[0] = g_c * o_c + g_w * o_w


def _nsa_s1(q8, wsum2, cache_win3, winn, gates, rbt, phik, phiv, past):
    db = q8.shape[0]
    nb = wsum2.shape[1]
    wbuf = cache_win3.shape[1]
    body = functools.partial(_nsa_s1_body, past=past)
    return pl.pallas_call(
        body,
        grid=(db,),
        in_specs=[pl.BlockSpec((1, N_HEADS, LANE), lambda b: (b, 0, 0)),
                  pl.BlockSpec((1, nb, 512), lambda b: (b, 0, 0)),
                  pl.BlockSpec((1, wbuf, 256), lambda b: (b, 0, 0)),
                  pl.BlockSpec((1, 1, 256), lambda b: (b, 0, 0)),
                  pl.BlockSpec((1, 1, LANE), lambda b: (b, 0, 0)),
                  pl.BlockSpec((N_HEADS, N_BUCKETS), lambda b: (0, 0)),
                  pl.BlockSpec((LANE, LANE), lambda b: (0, 0)),
                  pl.BlockSpec((LANE, LANE), lambda b: (0, 0))],
        out_specs=[pl.BlockSpec((1, N_HEADS, LANE), lambda b: (b, 0, 0)),
                   pl.BlockSpec((1, 8, LANE), lambda b: (b, 0, 0))],
        out_shape=[jax.ShapeDtypeStruct((db, N_HEADS, LANE), F32),
                   jax.ShapeDtypeStruct((db, 8, LANE), I32)],
        compiler_params=_cparams(("arbitrary",)),
    )(q8, wsum2, cache_win3, winn, gates, rbt, phik, phiv)


def _nsa_s2_body(sel_ref, pt_ref, q_ref, blk0_ref, blk1_ref, kvn_ref, g_ref, rbt_ref, ocw_ref, o_ref,
                 acc_sc, m_sc, l_sc, *, past, npick):
    b, k = pl.program_id(0), pl.program_id(1)
    q = q_ref[0]
    rbt = rbt_ref[...]
    rb = lambda j: rbt[:, j:j + 1]
    kvn = kvn_ref[0]
    top = lax.broadcasted_iota(I32, (N_HEADS, 1), 0) < GQA

    @pl.when(k == 0)
    def _init():
        s_n = jnp.sum(q.astype(F32) * kvn[:, 256:384].astype(BF16).astype(F32), axis=1, keepdims=True) + rb(0)
        m_sc[...] = s_n
        l_sc[...] = jnp.ones(l_sc.shape, F32)
        acc_sc[...] = jnp.where(top, kvn[:, 384:384 + HEAD_DIM], kvn[:, 384 + HEAD_DIM:512])

    j0 = sel_ref[(b * N_KV + 0) * npick + k]
    j1 = sel_ref[(b * N_KV + 1) * npick + k]
    kt0 = blk0_ref[0, 0, 0, 0].astype(BF16)
    vt0 = blk0_ref[0, 0, 1, 0].astype(BF16)
    kt1 = blk1_ref[0, 0, 0, 0].astype(BF16)
    vt1 = blk1_ref[0, 0, 1, 0].astype(BF16)
    page = kt0.shape[1]
    bpp = page // SEL_BLOCK
    s = jnp.concatenate([jnp.dot(q[0:GQA, 0:HEAD_DIM], kt0, preferred_element_type=F32),
                         jnp.dot(q[GQA:, HEAD_DIM:], kt1, preferred_element_type=F32)], axis=0)
    pos = lax.broadcasted_iota(I32, (N_HEADS, page), 1)
    jrow = jnp.where(top, j0, j1)
    d = past - ((jrow // bpp) * page + pos)
    s = s + _bias_from_dist(d, rb)
    s = jnp.where((pos // SEL_BLOCK == jrow % bpp) & (d >= 0), s, NEG)
    m_old = m_sc[...]
    m_new = jnp.maximum(m_old, jnp.max(s, axis=1, keepdims=True))
    alpha = jnp.exp(m_old - m_new)
    p = jnp.exp(s - m_new).astype(BF16)
    pv = jnp.concatenate([_dot_nt(p[0:GQA], vt0), _dot_nt(p[GQA:], vt1)], axis=0)
    l_sc[...] = alpha * l_sc[...] + jnp.sum(p.astype(F32), axis=1, keepdims=True)
    acc_sc[...] = alpha * acc_sc[...] + pv
    m_sc[...] = m_new

    @pl.when(k == npick - 1)
    def _fin():
        g_s = _head_rows(pltpu.roll(g_ref[0], LANE - N_HEADS, 1))
        o = acc_sc[...] / l_sc[...]
        o_ref[0] = ocw_ref[0] + g_s * jnp.concatenate([o, o], axis=1)


def _nsa_s2(sel_flat, page_table, q8, cache_kv_l, kv4n, gates, rbt, ocw, past):
    db = q8.shape[0]
    npick = N_SEL - 1
    page = cache_kv_l.shape[5]
    bpp = page // SEL_BLOCK
    body = functools.partial(_nsa_s2_body, past=past, npick=npick)

    def blk_spec(h):
        def blk_map(b, k, sel, pt):
            j = sel[(b * N_KV + h) * npick + k]
            return (0, pt[b, j // bpp], 1, h, 0, 0)
        return pl.BlockSpec((1, 1, 2, 1, HEAD_DIM, page), blk_map)

    grid_spec = pltpu.PrefetchScalarGridSpec(
        num_scalar_prefetch=2,
        grid=(db, npick),
        in_specs=[pl.BlockSpec((1, N_HEADS, LANE), lambda b, k, sel, pt: (b, 0, 0)),
                  blk_spec(0), blk_spec(1),
                  pl.BlockSpec((1, 1, 512), lambda b, k, sel, pt: (b, 0, 0)),
                  pl.BlockSpec((1, 1, LANE), lambda b, k, sel, pt: (b, 0, 0)),
                  pl.BlockSpec((N_HEADS, N_BUCKETS), lambda b, k, sel, pt: (0, 0)),
                  pl.BlockSpec((1, N_HEADS, LANE), lambda b, k, sel, pt: (b, 0, 0))],
        out_specs=pl.BlockSpec((1, N_HEADS, LANE), lambda b, k, sel, pt: (b, 0, 0)),
        scratch_shapes=[pltpu.VMEM((N_HEADS, HEAD_DIM), F32),
                        pltpu.VMEM((N_HEADS, 1), F32),
                        pltpu.VMEM((N_HEADS, 1), F32)],
    )
    return pl.pallas_call(
        body,
        grid_spec=grid_spec,
        out_shape=jax.ShapeDtypeStruct((db, N_HEADS, LANE), F32),
        compiler_params=_cparams(("arbitrary", "arbitrary")),
    )(sel_flat, page_table, q8, cache_kv_l, cache_kv_l, kv4n, gates, rbt, ocw)


def _pool_body(x_ref, halo_ref, prev_ref, w_ref, sc_ref, o_ref, *, tm, pos_base):
    i = pl.program_id(1)
    halo = jnp.where(i == 0, prev_ref[0], halo_ref[0])
    cur = x_ref[0]
    ext = jnp.concatenate([halo, cur], axis=0)
    sums = {1: ext}
    w = 1
    while w < max(POOL_WINDOWS):
        sums[2 * w] = sums[w] + pltpu.roll(sums[w], w, 0)
        w *= 2
    pos = pos_base + i * tm + lax.broadcasted_iota(I32, (tm, 1), 0)
    outs = []
    for gi, wl in enumerate(POOL_WINDOWS):
        ch = slice(gi * POOL_GROUP, (gi + 1) * POOL_GROUP)
        tot = sums[wl][16:, ch]
        cnt = jnp.minimum(wl, pos + 1).astype(F32)
        d = tot / cnt - cur[:, ch]
        outs.append(jnp.dot(d.astype(BF16), w_ref[gi].astype(BF16), preferred_element_type=F32))
    o_ref[0] = jnp.concatenate(outs, axis=1) * sc_ref[...]


def _pool(p, prev16, pool_w, pool_scale, tm, pos_base):
    b, t, c = p.shape
    body = functools.partial(_pool_body, tm=tm, pos_base=pos_base)
    hb = tm // 16
    return pl.pallas_call(
        body,
        grid=(b, t // tm),
        in_specs=[pl.BlockSpec((1, tm, c), lambda bi, i: (bi, i, 0)),
                  pl.BlockSpec((1, 16, c), lambda bi, i: (bi, jnp.maximum(i * hb - 1, 0), 0)),
                  pl.BlockSpec((1, 16, c), lambda bi, i: (bi, 0, 0)),
                  pl.BlockSpec((len(POOL_WINDOWS), POOL_GROUP, POOL_GROUP), lambda bi, i: (0, 0, 0)),
                  pl.BlockSpec((1, c), lambda bi, i: (0, 0))],
        out_specs=pl.BlockSpec((1, tm, c), lambda bi, i: (bi, i, 0)),
        out_shape=jax.ShapeDtypeStruct((b, t, c), F32),
        compiler_params=_cparams(("arbitrary", "arbitrary")),
    )(p, p, prev16, pool_w, pool_scale.reshape(1, c))


def _rms(x, g):
    return x * lax.rsqrt(jnp.mean(x * x, axis=-1, keepdims=True) + EPS) * g


def _outproj_body(oa_ref, op_ref, x_ref, gate_ref, g_ref, wa_ref, wp_ref, o_ref):
    mix = (jnp.dot(oa_ref[0].astype(BF16), wa_ref[...], preferred_element_type=F32)
           + jnp.dot(op_ref[0].astype(BF16), wp_ref[...], preferred_element_type=F32))
    o_ref[0] = x_ref[0] + gate_ref[0] * _rms(mix, g_ref[...])


def _outproj(oa, op, x, gate, g, wa, wp, tm):
    bx, t, d = x.shape
    per_row = gate.shape[1] != 1
    ms = _mod_specs(t, tm, d, per_row)
    return pl.pallas_call(
        _outproj_body,
        grid=(bx, t // tm),
        in_specs=[pl.BlockSpec((1, tm, oa.shape[2]), lambda b, i: (b, i, 0)),
                  pl.BlockSpec((1, tm, op.shape[2]), lambda b, i: (b, i, 0)),
                  pl.BlockSpec((1, tm, d), lambda b, i: (b, i, 0)), ms,
                  pl.BlockSpec((1, d), lambda b, i: (0, 0)),
                  pl.BlockSpec(wa.shape, lambda b, i: (0, 0)),
                  pl.BlockSpec(wp.shape, lambda b, i: (0, 0))],
        out_specs=pl.BlockSpec((1, tm, d), lambda b, i: (b, i, 0)),
        out_shape=jax.ShapeDtypeStruct((bx, t, d), F32),
        compiler_params=_cparams(("arbitrary", "arbitrary")),
    )(oa, op, x, gate, g.reshape(1, d), wa, wp)


def _final_body(f_ref, x_ref, gate_ref, g_ref, o_ref):
    f = jnp.concatenate([f_ref[0, :, c, :] for c in range(f_ref.shape[2])], axis=1)
    o_ref[0] = x_ref[0] + gate_ref[0] * _rms(f, g_ref[...])


def _final(f, x, gate, g, tm):
    bx, t, d = x.shape
    per_row = gate.shape[1] != 1
    ms = _mod_specs(t, tm, d, per_row)
    xs = pl.BlockSpec((1, tm, d), lambda b, i: (b, i, 0))
    fs = pl.BlockSpec((1, tm, d // LANE, LANE), lambda b, i: (b, i, 0, 0))
    return pl.pallas_call(
        _final_body,
        grid=(bx, t // tm),
        in_specs=[fs, xs, ms, pl.BlockSpec((1, d), lambda b, i: (0, 0))],
        out_specs=xs,
        out_shape=jax.ShapeDtypeStruct((bx, t, d), F32),
        compiler_params=_cparams(("arbitrary", "arbitrary")),
    )(f, x, gate, g.reshape(1, d))


def _topk_rows(s, k, ids=None):
    row = lax.broadcasted_iota(I32, s.shape, 0) if ids is None else ids
    krow = lax.broadcasted_iota(I32, (k, s.shape[1]), 0)
    vals = jnp.zeros((k, s.shape[1]), F32)
    idxs = jnp.zeros((k, s.shape[1]), I32)
    for it in range(k):
        mx = jnp.max(s, axis=0, keepdims=True)
        ix = jnp.min(jnp.where(s == mx, row, jnp.int32(2 ** 30)), axis=0, keepdims=True)
        vals = jnp.where(krow == it, mx, vals)
        idxs = jnp.where(krow == it, ix, idxs)
        s = jnp.where(row == ix, REMOVED, s)
    return vals, idxs


def _pair_candidates(kk, t):
    r = lax.broadcasted_iota(I32, (kk + 7 * 8 + 8, t), 0)
    mid = (r - kk) // 8
    a = jnp.where(r < kk, 0, jnp.where(r < kk + 56, 1 + mid, r - (kk + 56) + 8))
    b = jnp.where(r < kk, r, jnp.where(r < kk + 56, r - kk - 8 * mid, 0))
    return a * kk + b, (a + 1) * (b + 1) <= kk


def _pick_rows(table, ix):
    k = table.shape[0]
    out = jnp.zeros(ix.shape, table.dtype)
    for a in range(k):
        out = jnp.where(ix == a, table[a:a + 1, :], out)
    return out


def _peer_topk_body(q_ref, keys_ref, idx_ref, gate_ref, e_sc):
    kk = PEER_TOPK
    assert kk == 16
    flat_id, reachable = _pair_candidates(kk, q_ref.shape[1])

    def head(h, carry):
        s1 = _dot_nt(keys_ref[2 * h], q_ref[2 * h])
        s2 = _dot_nt(keys_ref[2 * h + 1], q_ref[2 * h + 1])
        v1, i1 = _topk_rows(s1, kk)
        v2, i2 = _topk_rows(s2, kk)
        cand = jnp.concatenate([v1[0:1, :] + v2] + [v1[a:a + 1, :] + v2[0:8, :] for a in range(1, 8)]
                               + [v1[8:16, :] + v2[0:1, :]], axis=0)
        best, flat = _topk_rows(jnp.where(reachable, cand, REMOVED), kk, flat_id)
        e = _pick_rows(i1, flat // kk) * PEER_KEYS + _pick_rows(i2, flat % kk)
        ex = jnp.exp(best - jnp.max(best, axis=0, keepdims=True))
        gate = ex / jnp.sum(ex, axis=0, keepdims=True)
        off = pl.multiple_of(h * kk, kk)
        e_sc[pl.ds(off, kk), :] = e * HALF_ROWS + PAD_ROWS
        gate_ref[pl.ds(off, kk), :] = gate
        return carry

    lax.fori_loop(0, PEER_HEADS, head, 0)
    idx_ref[...] = e_sc[...].T


def _peer_topk(q16, keys16, tt):
    nq, n, _ = q16.shape
    rows = PEER_HEADS * PEER_TOPK
    return pl.pallas_call(
        _peer_topk_body,
        grid=(n // tt,),
        in_specs=[pl.BlockSpec((nq, tt, LANE), lambda i: (0, i, 0)),
                  pl.BlockSpec(keys16.shape, lambda i: (0, 0, 0))],
        out_specs=[pl.BlockSpec((tt, rows), lambda i: (i, 0)),
                   pl.BlockSpec((rows, tt), lambda i: (0, i))],
        out_shape=[jax.ShapeDtypeStruct((n, rows), I32),
                   jax.ShapeDtypeStruct((rows, n), F32)],
        scratch_shapes=[pltpu.VMEM((rows, tt), I32)],
        compiler_params=_cparams(("arbitrary",)),
    )(q16, keys16)


PAD_ROWS = 4
HALF_ROWS = 4
_SLOT_OF_ROW = (6, 2, 4, 0, 7, 3, 5, 1)


def _expert_slab(tab_ref, row, late):
    off = pl.multiple_of(row - (PAD_ROWS if late else 0), HALF_ROWS)
    w = tab_ref[pl.ds(off, 8), :]
    lo = pltpu.bitcast(w << 16, F32)
    hi = pltpu.bitcast(w & jnp.uint32(0xFFFF0000), F32)
    return lo, hi


def _peer_u_body(idx_ref, x_ref, gate_ref, tab_ref, act_ref, *, tt):
    rows = PEER_HEADS * PEER_TOPK
    ngrp = rows // 8
    lane = lax.broadcasted_iota(I32, (8, tt), 1)
    sub = lax.broadcasted_iota(I32, (8, LANE), 0)
    m1 = (sub & 2) != 0
    m2 = (sub & 1) != 0
    low = sub < HALF_ROWS

    def token(t, accs):
        x8 = x_ref[t]
        xr = pltpu.roll(x8, HALF_ROWS, 0)
        xa = (jnp.where(low, x8, 0.0), jnp.where(low, 0.0, xr))
        xb = (jnp.where(low, xr, 0.0), jnp.where(low, 0.0, x8))
        new = []
        for gidx in range(ngrp):
            prod = [None] * 8
            for r in range(8):
                slot = _SLOT_OF_ROW[r]
                late = slot % 2
                lo, hi = _expert_slab(tab_ref, idx_ref[t, gidx * 8 + r], late)
                prod[slot] = lo * xa[late] + hi * xb[late]
            c = [prod[2 * k] + prod[2 * k + 1] for k in range(4)]
            d = [c[k] + pltpu.roll(c[k], 2 if k % 2 == 0 else 6, 0) for k in range(4)]
            e0, e1 = jnp.where(m1, d[0], d[1]), jnp.where(m1, d[2], d[3])
            f0 = e0 + pltpu.roll(e0, 1, 0)
            f1 = e1 + pltpu.roll(e1, 7, 0)
            dcol = jnp.sum(jnp.where(m2, f0, f1), axis=1, keepdims=True)
            new.append(jnp.where(lane == t, dcol, accs[gidx]))
        return tuple(new)

    accs = lax.fori_loop(0, tt, token, tuple(jnp.zeros((8, tt), F32) for _ in range(ngrp)))
    d = jnp.concatenate(accs, axis=0)
    act_ref[...] = (jax.nn.gelu(d, approximate=True) * gate_ref[...]).T


def _peer_u(idx, x3, gate, tab, tt):
    n, rows = idx.shape
    body = functools.partial(_peer_u_body, tt=tt)
    return pl.pallas_call(
        body,
        grid=(n // tt,),
        in_specs=[pl.BlockSpec((tt, rows), lambda i: (i, 0), memory_space=pltpu.SMEM),
                  pl.BlockSpec((tt, 8, LANE), lambda i: (i, 0, 0)),
                  pl.BlockSpec((rows, tt), lambda i: (0, i)),
                  pl.BlockSpec(tab.shape, lambda i: (0, 0), pipeline_mode=pl.Buffered(1))],
        out_specs=pl.BlockSpec((tt, rows), lambda i: (i, 0)),
        out_shape=jax.ShapeDtypeStruct((n, rows), F32),
        compiler_params=_cparams(("arbitrary",)),
    )(idx, x3, gate, tab)


def _peer_v_body(idx_ref, act_ref, tab_ref, f_ref, *, tt):
    rows = PEER_HEADS * PEER_TOPK
    sub = lax.broadcasted_iota(I32, (8, LANE), 0)

    def token(t, carry):
        al = [jnp.zeros((8, LANE), F32) for _ in range(2)]
        ah = [jnp.zeros((8, LANE), F32) for _ in range(2)]
        for r in range(rows):
            lo, hi = _expert_slab(tab_ref, idx_ref[t, r], False)
            a = act_ref[t, r]
            al[r % 2] = al[r % 2] + a * lo
            ah[r % 2] = ah[r % 2] + a * hi
        f_ref[t] = jnp.where(sub < HALF_ROWS, al[0] + al[1], pltpu.roll(ah[0] + ah[1], HALF_ROWS, 0))
        return carry

    lax.fori_loop(0, tt, token, 0)


def _peer_v(idx, act, tab, tt):
    n, rows = idx.shape
    body = functools.partial(_peer_v_body, tt=tt)
    return pl.pallas_call(
        body,
        grid=(n // tt,),
        in_specs=[pl.BlockSpec((tt, rows), lambda i: (i, 0), memory_space=pltpu.SMEM),
                  pl.BlockSpec((tt, rows), lambda i: (i, 0), memory_space=pltpu.SMEM),
                  pl.BlockSpec(tab.shape, lambda i: (0, 0), pipeline_mode=pl.Buffered(1))],
        out_specs=pl.BlockSpec((tt, 8, LANE), lambda i: (i, 0, 0)),
        out_shape=jax.ShapeDtypeStruct((n, 8, LANE), F32),
        compiler_params=_cparams(("arbitrary",)),
    )(idx, act, tab)


def _pack_table(tab):
    n, d = tab.shape
    bits = lax.bitcast_convert_type(tab.astype(jnp.bfloat16), jnp.uint16).astype(U32)
    words = (bits[:, :d // 2] | (bits[:, d // 2:] << 16)).reshape(n * HALF_ROWS, LANE)
    pad = jnp.zeros((PAD_ROWS, LANE), U32)
    return jnp.concatenate([pad, words, pad], axis=0)


def _peer(x1, sc, sh, g, wq, keys16, tab_u, tab_v, tm):
    bx, t, d = x1.shape
    n = bx * t
    q16, h2 = _peerq(x1, sc, sh, g, wq, tm)
    tt = min(128, n)
    idx, gate = _peer_topk(q16.reshape(2 * PEER_HEADS, n, LANE), keys16, tt)
    act = _peer_u(idx, h2.reshape(n, d // LANE, LANE), gate, tab_u, tt)
    f = _peer_v(idx, act, tab_v, tt)
    return f.reshape(bx, t, d // LANE, LANE)


def _pack_w_in(w_in):
    d = w_in.shape[0]
    scale = HEAD_DIM ** -0.5
    cols = []
    for hd in range(N_HEADS):
        blk = w_in[:, hd * HEAD_DIM:(hd + 1) * HEAD_DIM] * scale
        z = jnp.zeros((d, HEAD_DIM), w_in.dtype)
        cols += [blk, z] if hd // GQA == 0 else [z, blk]
    o0 = N_HEADS * HEAD_DIM
    o1 = o0 + 6 * N_KV * HEAD_DIM
    o2 = o1 + 3 * N_HEADS
    kvw = N_KV * HEAD_DIM
    wa = jnp.concatenate([w_in[:, o0:o1], w_in[:, o2:]], axis=1)
    cols.append(w_in[:, o0 + 3 * kvw:o0 + 4 * kvw])
    cols.append(w_in[:, o0 + 5 * kvw:o0 + 6 * kvw])
    cols.append(w_in[:, o1:o2])
    cols.append(jnp.zeros((d, LANE - 3 * N_HEADS), w_in.dtype))
    wbt = jnp.concatenate(cols, axis=1).T
    return wa.astype(BF16), wbt.astype(BF16)


def _pack_w_out(w_out):
    d = w_out.shape[1]
    rows = []
    for hd in range(N_HEADS):
        blk = w_out[hd * HEAD_DIM:(hd + 1) * HEAD_DIM]
        z = jnp.zeros((HEAD_DIM, d), w_out.dtype)
        rows += [blk, z] if hd // GQA == 0 else [z, blk]
    wa = jnp.concatenate(rows, axis=0).astype(BF16)
    wp = w_out[N_HEADS * HEAD_DIM:].astype(BF16)
    return wa, wp


def _pack_compress(alpha, phi):
    a = jnp.concatenate([alpha[0], alpha[0], alpha[1], alpha[1]], axis=1)
    z = jnp.zeros((HEAD_DIM, HEAD_DIM), phi.dtype)
    blocks = [phi[0], phi[0], phi[1], phi[1]]
    p = jnp.concatenate([jnp.concatenate([blocks[r] if c == r else z for c in range(4)], axis=1)
                         for r in range(4)], axis=0)
    return a, p


def kernel(x_prompt, x_sample, cache_kv, cache_win, state_pool, page_table, c_prompt, c_sample,
           w_ada, b_ada, norm_g, w_in, w_out, cmp_alpha, cmp_phi, rel_bias, pool_w, pool_scale,
           peer_wq, peer_keys, peer_u, peer_v):
    depth = w_ada.shape[0]
    assert depth == 1 and x_sample.shape[1] == 1
    l = 0
    b, s, d = x_prompt.shape
    db = x_sample.shape[0]
    n_pool, page = cache_kv.shape[1], cache_kv.shape[2]
    n_pages = page_table.shape[1]
    past = n_pages * page
    wbuf = cache_win.shape[2]
    tq = WINDOW // 2

    w_in_a, w_in_bt = _pack_w_in(w_in[l])
    wa, wp = _pack_w_out(w_out[l])
    a_cmp, phi_cmp = _pack_compress(cmp_alpha[l], cmp_phi[l])
    wq = peer_wq[l].astype(BF16)
    keys16 = peer_keys[l].reshape(2 * PEER_HEADS, PEER_KEYS, LANE).astype(BF16)
    tab_u = _pack_table(peer_u[l])
    tab_v = _pack_table(peer_v[l])
    g = norm_g[l]

    c_all = jnp.concatenate([c_prompt, c_sample], axis=0)
    mod = _mod(c_all, w_ada[l], b_ada[l]).reshape(b + db, 6, d)
    mod_p = [mod[:b, k][:, None, :] for k in range(6)]
    mod_s = [mod[b:, k][None, :, :] for k in range(6)]

    tm = min(512, s)
    kv4, kk, win, pool_in, q_t, gates_t, v_t = _inproj(x_prompt, mod_p[1], mod_p[0], g[0], w_in_a, w_in_bt, tm, tq)
    kcvc = _compress(kv4, a_cmp, phi_cmp, min(1024, s))
    o_attn = _nsa_prompt(q_t, kk, v_t, kcvc.reshape(b, s // SEL_BLOCK, 512), gates_t, rel_bias, tq)
    prev0 = jnp.zeros((b, 16, POOL_DIM), F32)
    o_pool = _pool(pool_in, prev0, pool_w[l], pool_scale[l], min(1024, s), 0)
    x1 = _outproj(o_attn, o_pool, x_prompt, mod_p[2], g[1], wa, wp, tm)
    f = _peer(x1, mod_p[4], mod_p[3], g[2], wq, keys16, tab_u, tab_v, tm)
    y_prompt = _final(f, x1, mod_p[5], g[3], tm)

    kv_prompt = kv4.reshape(1, b, s, 4, N_KV, HEAD_DIM)
    win_prompt = win[:, s - min(WINDOW, s):].reshape(1, b, min(WINDOW, s), 2, N_KV, HEAD_DIM)
    pool_prompt = pool_in[:, s - POOL_STATE:][None]

    xs = x_sample.reshape(1, db, d)
    kv4_s, _, win_s, pool_s, q_t_s, gates_t_s = _inproj(xs, mod_s[1], mod_s[0], g[0], w_in_a, w_in_bt, db, None)
    cache_l = jnp.transpose(cache_kv, (0, 1, 3, 4, 5, 2))
    wsum = _wsum_paged(cache_l, page_table, cmp_alpha[l])
    q8 = q_t_s[0].T.reshape(db, N_HEADS, LANE)
    kv4n = kv4_s.reshape(db, 1, 512)
    winn = win_s.reshape(db, 1, 256)
    gat = gates_t_s[0].T.reshape(db, 1, LANE)
    cw3 = cache_win[l].reshape(db, wbuf, 256)
    rbt = rel_bias.T
    ocw, sel_idx = _nsa_s1(q8, wsum.reshape(db, past // SEL_BLOCK, 512), cw3, winn, gat, rbt,
                           phi_cmp[:LANE, :LANE], phi_cmp[LANE:, LANE:], past)
    sel_flat = sel_idx[:, :N_KV, :N_SEL - 1].reshape(-1)
    o_attn_s = _nsa_s2(sel_flat, page_table, q8, cache_l, kv4n, gat, rbt, ocw, past)
    ext = jnp.concatenate([jnp.zeros((db, 1, POOL_DIM), F32), state_pool[l], pool_s.reshape(db, 1, POOL_DIM)], axis=1)
    o_pool_s = _pool(ext[:, 1:], jnp.zeros((db, 16, POOL_DIM), F32), pool_w[l], pool_scale[l], 16,
                     past - POOL_STATE)[:, POOL_STATE:]
    x1s = _outproj(o_attn_s.reshape(1, db, N_HEADS * LANE).astype(BF16), o_pool_s.reshape(1, db, POOL_DIM),
                   xs, mod_s[2], g[1], wa, wp, db)
    fs = _peer(x1s, mod_s[4], mod_s[3], g[2], wq, keys16, tab_u, tab_v, db)
    y_sample = _final(fs, x1s, mod_s[5], g[3], db).reshape(db, 1, d)

    kv_sample = kv4_s.reshape(1, db, 1, 4, N_KV, HEAD_DIM)
    win_sample = jnp.concatenate([cache_win[l][:, 1:], win_s.reshape(db, 1, 2, N_KV, HEAD_DIM)], axis=1)[None]
    pool_sample = ext[:, 2:][None]
    return (y_prompt, y_sample, kv_prompt, kv_sample, win_prompt, win_sample, pool_prompt, pool_sample)
```

```python
import functools
import math

import numpy as np
import jax
import jax.numpy as jnp
from jax import lax
from jax.experimental import pallas as pl
from jax.experimental.pallas import tpu as pltpu

F32, BF16, I32, U32 = jnp.float32, jnp.bfloat16, jnp.int32, jnp.uint32

D_MODEL = 1024
N_HEADS = 8
HEAD_DIM = 64
N_KV = 2
GQA = 4
CMP_BLOCK = 32
SEL_BLOCK = 64
N_SEL = 16
WINDOW = 512
N_BUCKETS = 32
REL_MAX_DIST = 128
POOL_WINDOWS = (2, 4, 8, 16)
POOL_DIM = 512
POOL_GROUP = 128
POOL_STATE = 15
PEER_HEADS = 8
PEER_KEYS = 128
PEER_TOPK = 16
EPS = 1e-6
LANE = 128
NEG = -1e30
BIG = 1e30
REMOVED = -3e38
LOG2E = math.log2(math.e)
SEL_PENALTY = -NEG
VMEM_LIMIT = 56 * 1024 * 1024

A_KV4, A_WIN, A_POOL, A_END = 0, 512, 768, 1280
B_Q, B_V, B_GATE, B_END = 0, 1024, 1280, 1408


def _cparams(sem):
    return pltpu.CompilerParams(dimension_semantics=sem, vmem_limit_bytes=VMEM_LIMIT)


def _bucket_thresholds():
    n = np.arange(0, 4 * REL_MAX_DIST)
    exact = N_BUCKETS // 2
    ratio = np.log(np.maximum(n, exact).astype(np.float32) / np.float32(exact)) / np.float32(
        math.log(REL_MAX_DIST / exact))
    large = np.minimum(exact + (ratio * np.float32(N_BUCKETS - exact)).astype(np.int32), N_BUCKETS - 1)
    b = np.where(n < exact, n, large)
    return [int(np.argmax(b >= j)) for j in range(N_BUCKETS)]


_THR = _bucket_thresholds()
FAR_DIST = _THR[N_BUCKETS - 1]


def _bias_from_dist(d, rb):
    val = jnp.full(d.shape, rb(0), F32)
    for j in range(1, N_BUCKETS):
        if _THR[j] == _THR[j - 1] and j > 1:
            continue
        jj = j
        while jj + 1 < N_BUCKETS and _THR[jj + 1] == _THR[j]:
            jj += 1
        val = jnp.where(d >= _THR[j], rb(jj), val)
    return val


def _dot_nt(a, b):
    return lax.dot_general(a, b, (((1,), (1,)), ((), ())), preferred_element_type=F32)


def _mod_body(c_ref, w_ref, b_ref, o_ref):
    c = c_ref[...]
    a = (c * jax.nn.sigmoid(c)).astype(BF16)
    o_ref[...] = jnp.dot(a, w_ref[...].astype(BF16), preferred_element_type=F32) + b_ref[...]


def _mod(c, w_ada, b_ada):
    r, d = c.shape
    n = w_ada.shape[1]
    tn = 1536
    return pl.pallas_call(
        _mod_body,
        grid=(n // tn,),
        in_specs=[pl.BlockSpec((r, d), lambda j: (0, 0)),
                  pl.BlockSpec((d, tn), lambda j: (0, j)),
                  pl.BlockSpec((1, tn), lambda j: (0, j))],
        out_specs=pl.BlockSpec((r, tn), lambda j: (0, j)),
        out_shape=jax.ShapeDtypeStruct((r, n), F32),
        compiler_params=_cparams(("arbitrary",)),
    )(c, w_ada, b_ada.reshape(1, n))


def _norm_mod(x, g, sc, sh):
    ms = jnp.mean(x * x, axis=-1, keepdims=True)
    h = x * lax.rsqrt(ms + EPS) * g
    return h * (1.0 + sc) + sh


def _inproj_body(x_ref, sc_ref, sh_ref, g_ref, wa_ref, wb_ref, kv_ref, kk_ref, win_ref, pool_ref,
                 qt_ref, gt_ref, *vt_refs, tq):
    h = _norm_mod(x_ref[0], g_ref[...], sc_ref[0], sh_ref[0]).astype(BF16)
    u = jnp.dot(h, wa_ref[...], preferred_element_type=F32)
    kv_ref[0] = u[:, A_KV4:A_WIN]
    win_ref[0] = u[:, A_WIN:A_POOL]
    pool_ref[0] = u[:, A_POOL:A_END]
    kk_ref[0, :, 0:LANE] = u[:, A_KV4 + 2 * LANE:A_KV4 + 3 * LANE].astype(BF16)
    kk_ref[0, :, LANE:2 * LANE] = u[:, A_WIN:A_WIN + LANE].astype(BF16)
    ut = _dot_nt(wb_ref[...], h)
    qt_ref[0] = ut[B_Q:B_V].astype(BF16)
    gt_ref[0] = jax.nn.sigmoid(ut[B_GATE:B_END])
    for vt_ref in vt_refs:
        for c in range(ut.shape[1] // tq):
            vt_ref[0, c] = ut[B_V:B_GATE, c * tq:(c + 1) * tq].astype(BF16)


def _mod_specs(t, tm, d, per_row):
    if per_row:
        return pl.BlockSpec((1, tm, d), lambda b, i: (b, i, 0))
    return pl.BlockSpec((1, 1, d), lambda b, i: (b, 0, 0))


def _inproj(x, sc, sh, g, wa, wbt, tm, tq):
    bx, t, d = x.shape
    per_row = sc.shape[1] != 1
    ms = _mod_specs(t, tm, d, per_row)
    widths = (512, 256, 256, 512)
    dtypes = (F32, BF16, F32, F32)
    out_specs = [pl.BlockSpec((1, tm, wd), lambda b, i: (b, i, 0)) for wd in widths]
    out_shape = [jax.ShapeDtypeStruct((bx, t, wd), dt) for wd, dt in zip(widths, dtypes)]
    out_specs += [pl.BlockSpec((1, B_V - B_Q, tm), lambda b, i: (b, 0, i)),
                  pl.BlockSpec((1, B_END - B_GATE, tm), lambda b, i: (b, 0, i))]
    out_shape += [jax.ShapeDtypeStruct((bx, B_V - B_Q, t), BF16),
                  jax.ShapeDtypeStruct((bx, B_END - B_GATE, t), F32)]
    if tq is not None:
        out_specs.append(pl.BlockSpec((1, tm // tq, B_GATE - B_V, tq), lambda b, i: (b, i, 0, 0)))
        out_shape.append(jax.ShapeDtypeStruct((bx, t // tq, B_GATE - B_V, tq), BF16))
    return pl.pallas_call(
        functools.partial(_inproj_body, tq=tq),
        grid=(bx, t // tm),
        in_specs=[pl.BlockSpec((1, tm, d), lambda b, i: (b, i, 0)), ms, ms,
                  pl.BlockSpec((1, d), lambda b, i: (0, 0)),
                  pl.BlockSpec(wa.shape, lambda b, i: (0, 0)),
                  pl.BlockSpec(wbt.shape, lambda b, i: (0, 0))],
        out_specs=out_specs,
        out_shape=out_shape,
        compiler_params=_cparams(("arbitrary", "arbitrary")),
    )(x, sc, sh, g.reshape(1, d), wa, wbt)


def _peerq_body(x_ref, sc_ref, sh_ref, g_ref, w_ref, q_ref, h_ref):
    h = _norm_mod(x_ref[0], g_ref[...], sc_ref[0], sh_ref[0])
    for c in range(h.shape[1] // LANE):
        h_ref[0, :, c, :] = h[:, c * LANE:(c + 1) * LANE]
    u = jnp.dot(h.astype(BF16), w_ref[...], preferred_element_type=F32)
    for j in range(2 * PEER_HEADS):
        q_ref[j, 0] = u[:, j * LANE:(j + 1) * LANE].astype(BF16)


def _peerq(x, sc, sh, g, w, tm):
    bx, t, d = x.shape
    per_row = sc.shape[1] != 1
    ms = _mod_specs(t, tm, d, per_row)
    nq = 2 * PEER_HEADS
    return pl.pallas_call(
        _peerq_body,
        grid=(bx, t // tm),
        in_specs=[pl.BlockSpec((1, tm, d), lambda b, i: (b, i, 0)), ms, ms,
                  pl.BlockSpec((1, d), lambda b, i: (0, 0)),
                  pl.BlockSpec((d, nq * LANE), lambda b, i: (0, 0))],
        out_specs=[pl.BlockSpec((nq, 1, tm, LANE), lambda b, i: (0, b, i, 0)),
                   pl.BlockSpec((1, tm, d // LANE, LANE), lambda b, i: (b, i, 0, 0))],
        out_shape=[jax.ShapeDtypeStruct((nq, bx, t, LANE), BF16),
                   jax.ShapeDtypeStruct((bx, t, d // LANE, LANE), F32)],
        compiler_params=_cparams(("arbitrary", "arbitrary")),
    )(x, sc, sh, g.reshape(1, d), w)


def _compress_rows(rows, a, phi):
    n = rows.shape[0] // CMP_BLOCK
    w = jnp.sum(rows.reshape(n, CMP_BLOCK, rows.shape[1]) * a[None], axis=1)
    return jnp.dot(w, phi, preferred_element_type=F32, precision=lax.Precision.HIGHEST)


def _compress_body(x_ref, a_ref, phi_ref, o_ref):
    o_ref[0] = _compress_rows(x_ref[0], a_ref[...], phi_ref[...])


def _compress(kv4, a, phi, tm):
    b, s, _ = kv4.shape
    return pl.pallas_call(
        _compress_body,
        grid=(b, s // tm),
        in_specs=[pl.BlockSpec((1, tm, 256), lambda bi, i: (bi, i, 0)),
                  pl.BlockSpec((CMP_BLOCK, 256), lambda bi, i: (0, 0)),
                  pl.BlockSpec((256, 256), lambda bi, i: (0, 0))],
        out_specs=pl.BlockSpec((1, tm // CMP_BLOCK, 256), lambda bi, i: (bi, i, 0)),
        out_shape=jax.ShapeDtypeStruct((b, s // CMP_BLOCK, 256), F32),
        compiler_params=_cparams(("arbitrary", "arbitrary")),
    )(kv4, a, phi)


PAGES_PER_STEP = 4


def _wsum_paged_body(pt_ref, *refs):
    page_refs, a_ref, o_ref = refs[:PAGES_PER_STEP], refs[PAGES_PER_STEP], refs[PAGES_PER_STEP + 1]
    a = a_ref[...]
    page = a.shape[1]
    per = page // CMP_BLOCK
    nrow = PAGES_PER_STEP * per
    rowi = lax.broadcasted_iota(I32, (nrow, page), 0)
    posb = lax.broadcasted_iota(I32, (nrow, page), 1) // CMP_BLOCK
    acc = jnp.zeros((nrow, a.shape[0]), F32)
    for p, p_ref in enumerate(page_refs):
        z = p_ref[0, 0].reshape(a.shape) * a
        z_hi = z.astype(BF16)
        z_lo = (z - z_hi.astype(F32)).astype(BF16)
        member = jnp.where(rowi == p * per + posb, 1.0, 0.0).astype(BF16)
        acc = acc + _dot_nt(member, z_hi) + _dot_nt(member, z_lo)
    o_ref[0] = acc


def _wsum_paged(cache_t, page_table, alpha):
    db, n_pages = page_table.shape
    page = cache_t.shape[5]
    per = page // CMP_BLOCK
    at = jnp.tile(jnp.transpose(alpha, (0, 2, 1)), (1, 1, per))
    a_rows = jnp.broadcast_to(at[:, None], (2, N_KV, HEAD_DIM, page)).reshape(2 * N_KV * HEAD_DIM, page)

    def page_spec(r):
        return pl.BlockSpec((1, 1, 2, N_KV, HEAD_DIM, page),
                            lambda b, i, pt: (0, pt[b, PAGES_PER_STEP * i + r], 0, 0, 0, 0))

    grid_spec = pltpu.PrefetchScalarGridSpec(
        num_scalar_prefetch=1,
        grid=(db, n_pages // PAGES_PER_STEP),
        in_specs=[page_spec(r) for r in range(PAGES_PER_STEP)]
        + [pl.BlockSpec(a_rows.shape, lambda b, i, pt: (0, 0))],
        out_specs=pl.BlockSpec((1, PAGES_PER_STEP * per, a_rows.shape[0]), lambda b, i, pt: (b, i, 0)),
    )
    return pl.pallas_call(
        _wsum_paged_body,
        grid_spec=grid_spec,
        out_shape=jax.ShapeDtypeStruct((db, n_pages * per, a_rows.shape[0]), F32),
        compiler_params=_cparams(("arbitrary", "arbitrary")),
    )(page_table, *([cache_t] * PAGES_PER_STEP), a_rows)


def _select_blocks(score, n_pick):
    lane = lax.broadcasted_iota(I32, score.shape, 1)
    nl = score.shape[1]
    sel = jnp.zeros(score.shape, jnp.bool_)
    idxs = []
    for _ in range(n_pick):
        mx = jnp.max(score, axis=1, keepdims=True)
        idx = jnp.min(jnp.where(score == mx, lane, nl), axis=1, keepdims=True)
        hit = lane == idx
        sel = sel | hit
        score = jnp.where(hit, REMOVED, score)
        idxs.append(idx)
    return sel, idxs


def _flash_update(slot, qt, kc, vt, bias, mask, acc_sc, m_sc):
    s = jnp.dot(kc, qt, preferred_element_type=F32)
    if bias is not None:
        s = s + bias
    if mask is not None:
        s = jnp.where(mask, s, NEG)
    m_old = m_sc[slot]
    m_new = jnp.maximum(m_old, jnp.max(s, axis=0, keepdims=True))
    alpha = jnp.exp2(m_old - m_new)
    p = jnp.exp2(s - m_new)
    acc_sc[slot] = alpha * acc_sc[slot] + jnp.dot(vt, p.astype(BF16), preferred_element_type=F32)
    m_sc[slot] = m_new


def _select_rows(score, n_pick):
    row = lax.broadcasted_iota(I32, score.shape, 0)
    n = score.shape[0]
    sel = jnp.zeros(score.shape, jnp.bool_)
    for _ in range(n_pick):
        mx = jnp.max(score, axis=0, keepdims=True)
        idx = jnp.min(jnp.where(score == mx, row, n), axis=0, keepdims=True)
        hit = row == idx
        sel = sel | hit
        score = jnp.where(hit, REMOVED, score)
    return sel


def _nsa_body(rb_ref, q_ref, kk_ref, vt_ref, kc_ref, g_ref, o_ref,
              bdiag_sc, bcmp_sc, acc_sc, m_sc, osum_sc, *, seq, tq):
    i = pl.program_id(1)
    q0 = i * tq
    nb = seq // SEL_BLOCK
    per = tq // SEL_BLOCK
    ncopy = bcmp_sc.shape[2]

    @pl.when((pl.program_id(0) == 0) & (i == 0))
    def _build_bias_tiles():
        k = lax.broadcasted_iota(I32, (tq, tq), 0)
        t = lax.broadcasted_iota(I32, (tq, tq), 1)
        rr = lax.broadcasted_iota(I32, (2 * nb, tq), 0) - nb
        tc = lax.broadcasted_iota(I32, (2 * nb, tq), 1)
        for hd in range(N_HEADS):
            rb = lambda j, hd=hd: rb_ref[j, hd]
            kvh, cols = hd // GQA, slice((hd % GQA) * tq, (hd % GQA + 1) * tq)
            for dl in range(2):
                bdiag_sc[kvh, dl, :, cols] = LOG2E * (_bias_from_dist(t - k + dl * tq, rb) - rb(N_BUCKETS - 1))
            for par in range(2):
                for cp in range(ncopy):
                    bcmp_sc[kvh, par, cp, :, cols] = LOG2E * _bias_from_dist(
                        tc - (SEL_BLOCK * (rr + cp * per) + (par + 1) * CMP_BLOCK - 1), rb)

    gw = GQA * tq
    kc2 = kc_ref[0]
    kce, kco = kc2[:, 0:128].astype(BF16), kc2[:, 256:384].astype(BF16)
    vcet, vcot = kc2[:, 128:256].T.astype(BF16), kc2[:, 384:512].T.astype(BF16)
    qpos = q0 + lax.broadcasted_iota(I32, (nb, tq), 1)
    jl = lax.broadcasted_iota(I32, (nb, tq), 0)
    cur = qpos // SEL_BLOCK
    forced = (jl == 0) | (jl == cur) | (jl == cur - 1)
    causal_blk = jl * SEL_BLOCK <= qpos
    qpos_g = q0 + lax.broadcasted_iota(I32, (nb, gw), 1) % tq
    jl_g = lax.broadcasted_iota(I32, (nb, gw), 0)
    valid_e = qpos_g >= SEL_BLOCK * jl_g + CMP_BLOCK - 1
    valid_o = qpos_g >= SEL_BLOCK * jl_g + 2 * CMP_BLOCK - 1
    cmp_copy = (ncopy - i % ncopy) % ncopy
    cmp_start = pl.multiple_of(nb - (i + cmp_copy) * per, 8)
    kk = lax.broadcasted_iota(I32, (tq, gw), 0)
    tt = lax.broadcasted_iota(I32, (tq, gw), 1) % tq
    causal = tt >= kk
    gates = g_ref[0]
    ek = lax.broadcasted_iota(I32, (tq, nb), 0) // SEL_BLOCK
    ej = lax.broadcasted_iota(I32, (tq, nb), 1)

    def reset_state():
        m_sc[...] = jnp.full(m_sc.shape, NEG, F32)
        acc_sc[...] = jnp.zeros(acc_sc.shape, F32)

    def qgroup(kvh):
        return jnp.concatenate([q_ref[0, hd * LANE:(hd + 1) * LANE, :]
                                for hd in range(kvh * GQA, (kvh + 1) * GQA)], axis=1)

    vrow = lax.broadcasted_iota(I32, (LANE, tq), 0) // HEAD_DIM
    vrow2 = lax.broadcasted_iota(I32, (LANE, 2 * tq), 0) // HEAD_DIM
    ek2 = lax.broadcasted_iota(I32, (2 * tq, nb), 0) // SEL_BLOCK
    ej2 = lax.broadcasted_iota(I32, (2 * tq, nb), 1)

    def kv_chunk(c, branch, kvh):
        start = pl.multiple_of(c * tq, tq)
        v = vt_ref[0, c, branch * LANE:(branch + 1) * LANE, :]
        return (kk_ref[0, pl.ds(start, tq), branch * LANE:(branch + 1) * LANE],
                jnp.where(vrow == kvh, v, jnp.ones_like(v)))

    def finish(kvh, branch):
        lrow = (1 - kvh) * HEAD_DIM
        acc = acc_sc[0]
        o = acc / acc[lrow:lrow + 1, :]
        for g in range(GQA):
            hd = kvh * GQA + g
            r = branch * N_HEADS + hd
            osum_sc[hd] += gates[r:r + 1, :] * o[:, g * tq:(g + 1) * tq]

    scores = []
    for kvh in range(N_KV):
        qt = qgroup(kvh)
        s_e = jnp.dot(kce, qt, preferred_element_type=F32) + bcmp_sc[kvh, 0, cmp_copy, pl.ds(cmp_start, nb), :]
        s_o = jnp.dot(kco, qt, preferred_element_type=F32) + bcmp_sc[kvh, 1, cmp_copy, pl.ds(cmp_start, nb), :]
        s_e = jnp.where(valid_e, s_e, NEG)
        s_o = jnp.where(valid_o, s_o, NEG)
        mx = jnp.maximum(jnp.max(s_e, axis=0, keepdims=True), jnp.max(s_o, axis=0, keepdims=True))
        e_e = jnp.where(valid_e, jnp.exp2(s_e - mx), 0.0)
        e_o = jnp.where(valid_o, jnp.exp2(s_o - mx), 0.0)
        den = jnp.maximum(jnp.sum(e_e, axis=0, keepdims=True) + jnp.sum(e_o, axis=0, keepdims=True), 1e-30)
        p_e, p_o = e_e / den, e_o / den
        pe = p_e + p_o
        imp = (pe[:, 0:tq] + pe[:, tq:2 * tq]) + (pe[:, 2 * tq:3 * tq] + pe[:, 3 * tq:4 * tq])
        o_c = (jnp.dot(vcet, p_e.astype(BF16), preferred_element_type=F32)
               + jnp.dot(vcot, p_o.astype(BF16), preferred_element_type=F32))
        for g in range(GQA):
            hd = kvh * GQA + g
            osum_sc[hd] = gates[hd:hd + 1, :] * o_c[:, g * tq:(g + 1) * tq]
        score = jnp.where(forced, BIG, imp)
        scores.append(jnp.where(causal_blk, score, NEG))

    sel_all = _select_rows(jnp.concatenate(scores, axis=1), min(N_SEL, nb))

    for kvh in range(N_KV):
        pen = jnp.where(sel_all[:, kvh * tq:(kvh + 1) * tq], 0.0, -SEL_PENALTY).astype(BF16)
        qt = qgroup(kvh)
        qaug = jnp.concatenate([qt, jnp.concatenate([pen] * GQA, axis=1)], axis=0)

        def kaug(c, kc):
            expand = jnp.where(ej == ek + c * per, 1.0, 0.0).astype(BF16)
            return jnp.concatenate([kc, expand], axis=1)

        reset_state()

        def far_pair(c2, carry):
            c = 2 * c2
            start = pl.multiple_of(c * tq, 2 * tq)
            kc = kk_ref[0, pl.ds(start, 2 * tq), 0:LANE]
            v = jnp.concatenate([vt_ref[0, c, 0:LANE, :], vt_ref[0, c + 1, 0:LANE, :]], axis=1)
            vt = jnp.where(vrow2 == kvh, v, jnp.ones_like(v))
            expand = jnp.where(ej2 == ek2 + c * per, 1.0, 0.0).astype(BF16)
            _flash_update(0, qaug, jnp.concatenate([kc, expand], axis=1), vt, None, None, acc_sc, m_sc)
            return carry

        n_far = jnp.maximum(i - 1, 0)
        lax.fori_loop(0, n_far // 2, far_pair, 0)

        @pl.when(n_far % 2 == 1)
        def _far_single():
            kc, vt = kv_chunk(n_far - 1, 0, kvh)
            _flash_update(0, qaug, kaug(n_far - 1, kc), vt, None, None, acc_sc, m_sc)

        @pl.when(i >= 1)
        def _prev_chunk():
            kc, vt = kv_chunk(i - 1, 0, kvh)
            _flash_update(0, qaug, kaug(i - 1, kc), vt, bdiag_sc[kvh, 1], None, acc_sc, m_sc)

        kc, vt = kv_chunk(i, 0, kvh)
        _flash_update(0, qaug, kaug(i, kc), vt, bdiag_sc[kvh, 0], causal, acc_sc, m_sc)
        finish(kvh, 1)

        reset_state()

        @pl.when(i >= 2)
        def _win_far():
            kc, vt = kv_chunk(i - 2, 1, kvh)
            _flash_update(0, qt, kc, vt, None, kk > tt, acc_sc, m_sc)

        @pl.when(i >= 1)
        def _win_prev():
            kc, vt = kv_chunk(i - 1, 1, kvh)
            _flash_update(0, qt, kc, vt, bdiag_sc[kvh, 1], None, acc_sc, m_sc)

        kc, vt = kv_chunk(i, 1, kvh)
        _flash_update(0, qt, kc, vt, bdiag_sc[kvh, 0], causal, acc_sc, m_sc)
        finish(kvh, 2)

    for hd in range(N_HEADS):
        o_ref[0, :, hd * LANE:(hd + 1) * LANE] = osum_sc[hd].T.astype(BF16)


def _nsa_prompt(qt, kk, vt, kcvc2, gates_t, rel_bias, tq):
    b, _, s = qt.shape
    nb = s // SEL_BLOCK
    assert WINDOW == 2 * tq and tq + 1 >= FAR_DIST and s % tq == 0
    body = functools.partial(_nsa_body, seq=s, tq=tq)
    return pl.pallas_call(
        body,
        grid=(b, s // tq),
        in_specs=[pl.BlockSpec(memory_space=pltpu.SMEM),
                  pl.BlockSpec((1, N_HEADS * LANE, tq), lambda bi, i: (bi, 0, i)),
                  pl.BlockSpec((1, s, 2 * LANE), lambda bi, i: (bi, 0, 0)),
                  pl.BlockSpec((1, s // tq, 2 * LANE, tq), lambda bi, i: (bi, 0, 0, 0)),
                  pl.BlockSpec((1, nb, 512), lambda bi, i: (bi, 0, 0)),
                  pl.BlockSpec((1, LANE, tq), lambda bi, i: (bi, 0, i))],
        out_specs=pl.BlockSpec((1, tq, N_HEADS * LANE), lambda bi, i: (bi, i, 0)),
        out_shape=jax.ShapeDtypeStruct((b, s, N_HEADS * LANE), BF16),
        scratch_shapes=[pltpu.VMEM((N_KV, 2, tq, GQA * tq), F32),
                        pltpu.VMEM((N_KV, 2, max(1, 8 // (tq // SEL_BLOCK)), 2 * nb, GQA * tq), F32),
                        pltpu.VMEM((1, LANE, GQA * tq), F32),
                        pltpu.VMEM((1, 1, GQA * tq), F32),
                        pltpu.VMEM((N_HEADS, LANE, tq), F32)],
        compiler_params=_cparams(("arbitrary", "arbitrary")),
    )(rel_bias, qt, kk, vt, kcvc2, gates_t)


def _head_rows(x):
    sub = lax.broadcasted_iota(I32, (N_HEADS, LANE), 0)
    lane = lax.broadcasted_iota(I32, (N_HEADS, LANE), 1)
    return jnp.sum(jnp.where(sub == lane, jnp.broadcast_to(x, (N_HEADS, LANE)), 0.0), axis=1, keepdims=True)


def _nsa_s1_body(q_ref, kc_ref, cw_ref, wn_ref, g_ref, rbt_ref, phik_ref, phiv_ref, ocw_ref, idx_ref, *, past):
    nb = kc_ref.shape[1]
    wbuf = cw_ref.shape[1]
    q = q_ref[0]
    rbt = rbt_ref[...]
    rb = lambda j: rbt[:, j:j + 1]
    gates = g_ref[0]
    g_c = _head_rows(gates[:, 0:LANE])
    g_w = _head_rows(pltpu.roll(gates, LANE - 2 * N_HEADS, 1))

    ws = kc_ref[0]
    hi_dot = functools.partial(jnp.dot, preferred_element_type=F32, precision=lax.Precision.HIGHEST)
    phik, phiv = phik_ref[...], phiv_ref[...]
    kc2 = jnp.concatenate([hi_dot(ws[:, 0:128], phik), hi_dot(ws[:, 128:256], phiv),
                           hi_dot(ws[:, 256:384], phik), hi_dot(ws[:, 384:512], phiv)], axis=1)
    jl = lax.broadcasted_iota(I32, (N_HEADS, nb), 1)
    d_e = past - (SEL_BLOCK * jl + CMP_BLOCK - 1)
    d_o = past - (SEL_BLOCK * jl + 2 * CMP_BLOCK - 1)
    s_e = _dot_nt(q, kc2[:, 0:128].astype(BF16)) + _bias_from_dist(d_e, rb)
    s_o = _dot_nt(q, kc2[:, 256:384].astype(BF16)) + _bias_from_dist(d_o, rb)
    valid_e, valid_o = d_e >= 0, d_o >= 0
    s_e = jnp.where(valid_e, s_e, NEG)
    s_o = jnp.where(valid_o, s_o, NEG)
    mx = jnp.maximum(jnp.max(s_e, axis=1, keepdims=True), jnp.max(s_o, axis=1, keepdims=True))
    e_e = jnp.where(valid_e, jnp.exp(s_e - mx), 0.0)
    e_o = jnp.where(valid_o, jnp.exp(s_o - mx), 0.0)
    den = jnp.maximum(jnp.sum(e_e, axis=1, keepdims=True) + jnp.sum(e_o, axis=1, keepdims=True), 1e-30)
    p_e, p_o = e_e / den, e_o / den
    o_c = (jnp.dot(p_e.astype(BF16), kc2[:, 128:256].astype(BF16), preferred_element_type=F32)
           + jnp.dot(p_o.astype(BF16), kc2[:, 384:512].astype(BF16), preferred_element_type=F32))

    pe = p_e + p_o
    imp = jnp.concatenate([jnp.sum(pe[0:GQA], axis=0, keepdims=True),
                           jnp.sum(pe[GQA:2 * GQA], axis=0, keepdims=True)], axis=0)
    jl2 = lax.broadcasted_iota(I32, (N_KV, nb), 1)
    cur = past // SEL_BLOCK
    forced = (jl2 == 0) | (jl2 == cur) | (jl2 == cur - 1)
    score = jnp.where(forced, BIG, imp)
    score = jnp.where(jl2 * SEL_BLOCK <= past, score, NEG)
    _, idxs = _select_blocks(score, N_SEL - 1)
    lane = lax.broadcasted_iota(I32, (N_KV, LANE), 1)
    out_idx = jnp.zeros((N_KV, LANE), I32)
    for r, ix in enumerate(idxs):
        out_idx = jnp.where(lane == r, ix, out_idx)
    idx_ref[0] = jnp.concatenate([out_idx, jnp.zeros((8 - N_KV, LANE), I32)], axis=0)

    cw = cw_ref[0]
    il = lax.broadcasted_iota(I32, (N_HEADS, wbuf), 1)
    dq = wbuf - il
    s_w = _dot_nt(q, cw[:, 0:128].astype(BF16)) + _bias_from_dist(dq, rb)
    valid_w = dq < WINDOW
    s_w = jnp.where(valid_w, s_w, NEG)
    wn = wn_ref[0]
    qf = q.astype(F32)
    s_n = jnp.sum(qf * wn[:, 0:128].astype(BF16).astype(F32), axis=1, keepdims=True) + rb(0)
    mw = jnp.maximum(jnp.max(s_w, axis=1, keepdims=True), s_n)
    e_w = jnp.where(valid_w, jnp.exp(s_w - mw), 0.0)
    e_n = jnp.exp(s_n - mw)
    den_w = jnp.sum(e_w, axis=1, keepdims=True) + e_n
    o_w = (jnp.dot(e_w.astype(BF16), cw[:, 128:256].astype(BF16), preferred_element_type=F32)
           + e_n * wn[:, 128:256]) / den_w
    ocw_ref[0] = g_c * o_c + g_w * o_w


def _nsa_s1(q8, wsum2, cache_win3, winn, gates, rbt, phik, phiv, past):
    db = q8.shape[0]
    nb = wsum2.shape[1]
    wbuf = cache_win3.shape[1]
    body = functools.partial(_nsa_s1_body, past=past)
    return pl.pallas_call(
        body,
        grid=(db,),
        in_specs=[pl.BlockSpec((1, N_HEADS, LANE), lambda b: (b, 0, 0)),
                  pl.BlockSpec((1, nb, 512), lambda b: (b, 0, 0)),
                  pl.BlockSpec((1, wbuf, 256), lambda b: (b, 0, 0)),
                  pl.BlockSpec((1, 1, 256), lambda b: (b, 0, 0)),
                  pl.BlockSpec((1, 1, LANE), lambda b: (b, 0, 0)),
                  pl.BlockSpec((N_HEADS, N_BUCKETS), lambda b: (0, 0)),
                  pl.BlockSpec((LANE, LANE), lambda b: (0, 0)),
                  pl.BlockSpec((LANE, LANE), lambda b: (0, 0))],
        out_specs=[pl.BlockSpec((1, N_HEADS, LANE), lambda b: (b, 0, 0)),
                   pl.BlockSpec((1, 8, LANE), lambda b: (b, 0, 0))],
        out_shape=[jax.ShapeDtypeStruct((db, N_HEADS, LANE), F32),
                   jax.ShapeDtypeStruct((db, 8, LANE), I32)],
        compiler_params=_cparams(("arbitrary",)),
    )(q8, wsum2, cache_win3, winn, gates, rbt, phik, phiv)


def _nsa_s2_body(sel_ref, pt_ref, q_ref, blk0_ref, blk1_ref, kvn_ref, g_ref, rbt_ref, ocw_ref, o_ref,
                 acc_sc, m_sc, l_sc, *, past, npick):
    b, k = pl.program_id(0), pl.program_id(1)
    q = q_ref[0]
    rbt = rbt_ref[...]
    rb = lambda j: rbt[:, j:j + 1]
    kvn = kvn_ref[0]
    top = lax.broadcasted_iota(I32, (N_HEADS, 1), 0) < GQA

    @pl.when(k == 0)
    def _init():
        s_n = jnp.sum(q.astype(F32) * kvn[:, 256:384].astype(BF16).astype(F32), axis=1, keepdims=True) + rb(0)
        m_sc[...] = s_n
        l_sc[...] = jnp.ones(l_sc.shape, F32)
        acc_sc[...] = jnp.where(top, kvn[:, 384:384 + HEAD_DIM], kvn[:, 384 + HEAD_DIM:512])

    j0 = sel_ref[(b * N_KV + 0) * npick + k]
    j1 = sel_ref[(b * N_KV + 1) * npick + k]
    kt0 = blk0_ref[0, 0, 0, 0].astype(BF16)
    vt0 = blk0_ref[0, 0, 1, 0].astype(BF16)
    kt1 = blk1_ref[0, 0, 0, 0].astype(BF16)
    vt1 = blk1_ref[0, 0, 1, 0].astype(BF16)
    page = kt0.shape[1]
    bpp = page // SEL_BLOCK
    s = jnp.concatenate([jnp.dot(q[0:GQA, 0:HEAD_DIM], kt0, preferred_element_type=F32),
                         jnp.dot(q[GQA:, HEAD_DIM:], kt1, preferred_element_type=F32)], axis=0)
    pos = lax.broadcasted_iota(I32, (N_HEADS, page), 1)
    jrow = jnp.where(top, j0, j1)
    d = past - ((jrow // bpp) * page + pos)
    s = s + _bias_from_dist(d, rb)
    s = jnp.where((pos // SEL_BLOCK == jrow % bpp) & (d >= 0), s, NEG)
    m_old = m_sc[...]
    m_new = jnp.maximum(m_old, jnp.max(s, axis=1, keepdims=True))
    alpha = jnp.exp(m_old - m_new)
    p = jnp.exp(s - m_new).astype(BF16)
    pv = jnp.concatenate([_dot_nt(p[0:GQA], vt0), _dot_nt(p[GQA:], vt1)], axis=0)
    l_sc[...] = alpha * l_sc[...] + jnp.sum(p.astype(F32), axis=1, keepdims=True)
    acc_sc[...] = alpha * acc_sc[...] + pv
    m_sc[...] = m_new

    @pl.when(k == npick - 1)
    def _fin():
        g_s = _head_rows(pltpu.roll(g_ref[0], LANE - N_HEADS, 1))
        o = acc_sc[...] / l_sc[...]
        o_ref[0] = ocw_ref[0] + g_s * jnp.concatenate([o, o], axis=1)


def _nsa_s2(sel_flat, page_table, q8, cache_kv_l, kv4n, gates, rbt, ocw, past):
    db = q8.shape[0]
    npick = N_SEL - 1
    page = cache_kv_l.shape[5]
    bpp = page // SEL_BLOCK
    body = functools.partial(_nsa_s2_body, past=past, npick=npick)

    def blk_spec(h):
        def blk_map(b, k, sel, pt):
            j = sel[(b * N_KV + h) * npick + k]
            return (0, pt[b, j // bpp], 1, h, 0, 0)
        return pl.BlockSpec((1, 1, 2, 1, HEAD_DIM, page), blk_map)

    grid_spec = pltpu.PrefetchScalarGridSpec(
        num_scalar_prefetch=2,
        grid=(db, npick),
        in_specs=[pl.BlockSpec((1, N_HEADS, LANE), lambda b, k, sel, pt: (b, 0, 0)),
                  blk_spec(0), blk_spec(1),
                  pl.BlockSpec((1, 1, 512), lambda b, k, sel, pt: (b, 0, 0)),
                  pl.BlockSpec((1, 1, LANE), lambda b, k, sel, pt: (b, 0, 0)),
                  pl.BlockSpec((N_HEADS, N_BUCKETS), lambda b, k, sel, pt: (0, 0)),
                  pl.BlockSpec((1, N_HEADS, LANE), lambda b, k, sel, pt: (b, 0, 0))],
        out_specs=pl.BlockSpec((1, N_HEADS, LANE), lambda b, k, sel, pt: (b, 0, 0)),
        scratch_shapes=[pltpu.VMEM((N_HEADS, HEAD_DIM), F32),
                        pltpu.VMEM((N_HEADS, 1), F32),
                        pltpu.VMEM((N_HEADS, 1), F32)],
    )
    return pl.pallas_call(
        body,
        grid_spec=grid_spec,
        out_shape=jax.ShapeDtypeStruct((db, N_HEADS, LANE), F32),
        compiler_params=_cparams(("arbitrary", "arbitrary")),
    )(sel_flat, page_table, q8, cache_kv_l, cache_kv_l, kv4n, gates, rbt, ocw)


def _pool_body(x_ref, halo_ref, prev_ref, w_ref, sc_ref, o_ref, *, tm, pos_base):
    i = pl.program_id(1)
    halo = jnp.where(i == 0, prev_ref[0], halo_ref[0])
    cur = x_ref[0]
    ext = jnp.concatenate([halo, cur], axis=0)
    sums = {1: ext}
    w = 1
    while w < max(POOL_WINDOWS):
        sums[2 * w] = sums[w] + pltpu.roll(sums[w], w, 0)
        w *= 2
    pos = pos_base + i * tm + lax.broadcasted_iota(I32, (tm, 1), 0)
    outs = []
    for gi, wl in enumerate(POOL_WINDOWS):
        ch = slice(gi * POOL_GROUP, (gi + 1) * POOL_GROUP)
        tot = sums[wl][16:, ch]
        cnt = jnp.minimum(wl, pos + 1).astype(F32)
        d = tot / cnt - cur[:, ch]
        outs.append(jnp.dot(d.astype(BF16), w_ref[gi].astype(BF16), preferred_element_type=F32))
    o_ref[0] = jnp.concatenate(outs, axis=1) * sc_ref[...]


def _pool(p, prev16, pool_w, pool_scale, tm, pos_base):
    b, t, c = p.shape
    body = functools.partial(_pool_body, tm=tm, pos_base=pos_base)
    hb = tm // 16
    return pl.pallas_call(
        body,
        grid=(b, t // tm),
        in_specs=[pl.BlockSpec((1, tm, c), lambda bi, i: (bi, i, 0)),
                  pl.BlockSpec((1, 16, c), lambda bi, i: (bi, jnp.maximum(i * hb - 1, 0), 0)),
                  pl.BlockSpec((1, 16, c), lambda bi, i: (bi, 0, 0)),
                  pl.BlockSpec((len(POOL_WINDOWS), POOL_GROUP, POOL_GROUP), lambda bi, i: (0, 0, 0)),
                  pl.BlockSpec((1, c), lambda bi, i: (0, 0))],
        out_specs=pl.BlockSpec((1, tm, c), lambda bi, i: (bi, i, 0)),
        out_shape=jax.ShapeDtypeStruct((b, t, c), F32),
        compiler_params=_cparams(("arbitrary", "arbitrary")),
    )(p, p, prev16, pool_w, pool_scale.reshape(1, c))


def _rms(x, g):
    return x * lax.rsqrt(jnp.mean(x * x, axis=-1, keepdims=True) + EPS) * g


def _outproj_body(oa_ref, op_ref, x_ref, gate_ref, g_ref, wa_ref, wp_ref, o_ref):
    mix = (jnp.dot(oa_ref[0].astype(BF16), wa_ref[...], preferred_element_type=F32)
           + jnp.dot(op_ref[0].astype(BF16), wp_ref[...], preferred_element_type=F32))
    o_ref[0] = x_ref[0] + gate_ref[0] * _rms(mix, g_ref[...])


def _outproj(oa, op, x, gate, g, wa, wp, tm):
    bx, t, d = x.shape
    per_row = gate.shape[1] != 1
    ms = _mod_specs(t, tm, d, per_row)
    return pl.pallas_call(
        _outproj_body,
        grid=(bx, t // tm),
        in_specs=[pl.BlockSpec((1, tm, oa.shape[2]), lambda b, i: (b, i, 0)),
                  pl.BlockSpec((1, tm, op.shape[2]), lambda b, i: (b, i, 0)),
                  pl.BlockSpec((1, tm, d), lambda b, i: (b, i, 0)), ms,
                  pl.BlockSpec((1, d), lambda b, i: (0, 0)),
                  pl.BlockSpec(wa.shape, lambda b, i: (0, 0)),
                  pl.BlockSpec(wp.shape, lambda b, i: (0, 0))],
        out_specs=pl.BlockSpec((1, tm, d), lambda b, i: (b, i, 0)),
        out_shape=jax.ShapeDtypeStruct((bx, t, d), F32),
        compiler_params=_cparams(("arbitrary", "arbitrary")),
    )(oa, op, x, gate, g.reshape(1, d), wa, wp)


def _final_body(f_ref, x_ref, gate_ref, g_ref, o_ref):
    f = jnp.concatenate([f_ref[0, :, c, :] for c in range(f_ref.shape[2])], axis=1)
    o_ref[0] = x_ref[0] + gate_ref[0] * _rms(f, g_ref[...])


def _final(f, x, gate, g, tm):
    bx, t, d = x.shape
    per_row = gate.shape[1] != 1
    ms = _mod_specs(t, tm, d, per_row)
    xs = pl.BlockSpec((1, tm, d), lambda b, i: (b, i, 0))
    fs = pl.BlockSpec((1, tm, d // LANE, LANE), lambda b, i: (b, i, 0, 0))
    return pl.pallas_call(
        _final_body,
        grid=(bx, t // tm),
        in_specs=[fs, xs, ms, pl.BlockSpec((1, d), lambda b, i: (0, 0))],
        out_specs=xs,
        out_shape=jax.ShapeDtypeStruct((bx, t, d), F32),
        compiler_params=_cparams(("arbitrary", "arbitrary")),
    )(f, x, gate, g.reshape(1, d))


def _topk_rows(s, k, ids=None):
    row = lax.broadcasted_iota(I32, s.shape, 0) if ids is None else ids
    krow = lax.broadcasted_iota(I32, (k, s.shape[1]), 0)
    vals = jnp.zeros((k, s.shape[1]), F32)
    idxs = jnp.zeros((k, s.shape[1]), I32)
    for it in range(k):
        mx = jnp.max(s, axis=0, keepdims=True)
        ix = jnp.min(jnp.where(s == mx, row, jnp.int32(2 ** 30)), axis=0, keepdims=True)
        vals = jnp.where(krow == it, mx, vals)
        idxs = jnp.where(krow == it, ix, idxs)
        s = jnp.where(row == ix, REMOVED, s)
    return vals, idxs


def _pair_candidates(kk, t):
    r = lax.broadcasted_iota(I32, (kk + 7 * 8 + 8, t), 0)
    mid = (r - kk) // 8
    a = jnp.where(r < kk, 0, jnp.where(r < kk + 56, 1 + mid, r - (kk + 56) + 8))
    b = jnp.where(r < kk, r, jnp.where(r < kk + 56, r - kk - 8 * mid, 0))
    return a * kk + b, (a + 1) * (b + 1) <= kk


def _pick_rows(table, ix):
    k = table.shape[0]
    out = jnp.zeros(ix.shape, table.dtype)
    for a in range(k):
        out = jnp.where(ix == a, table[a:a + 1, :], out)
    return out


def _peer_topk_body(q_ref, keys_ref, idx_ref, gate_ref, e_sc):
    kk = PEER_TOPK
    assert kk == 16
    flat_id, reachable = _pair_candidates(kk, q_ref.shape[1])

    def head(h, carry):
        s1 = _dot_nt(keys_ref[2 * h], q_ref[2 * h])
        s2 = _dot_nt(keys_ref[2 * h + 1], q_ref[2 * h + 1])
        v1, i1 = _topk_rows(s1, kk)
        v2, i2 = _topk_rows(s2, kk)
        cand = jnp.concatenate([v1[0:1, :] + v2] + [v1[a:a + 1, :] + v2[0:8, :] for a in range(1, 8)]
                               + [v1[8:16, :] + v2[0:1, :]], axis=0)
        best, flat = _topk_rows(jnp.where(reachable, cand, REMOVED), kk, flat_id)
        e = _pick_rows(i1, flat // kk) * PEER_KEYS + _pick_rows(i2, flat % kk)
        ex = jnp.exp(best - jnp.max(best, axis=0, keepdims=True))
        gate = ex / jnp.sum(ex, axis=0, keepdims=True)
        off = pl.multiple_of(h * kk, kk)
        e_sc[pl.ds(off, kk), :] = e * HALF_ROWS + PAD_ROWS
        gate_ref[pl.ds(off, kk), :] = gate
        return carry

    def head_pair(j, carry):
        head(2 * j, carry)
        return head(2 * j + 1, carry)

    lax.fori_loop(0, PEER_HEADS // 2, head_pair, 0)
    idx_ref[...] = e_sc[...].T


def _peer_topk(q16, keys16, tt):
    nq, n, _ = q16.shape
    rows = PEER_HEADS * PEER_TOPK
    return pl.pallas_call(
        _peer_topk_body,
        grid=(n // tt,),
        in_specs=[pl.BlockSpec((nq, tt, LANE), lambda i: (0, i, 0)),
                  pl.BlockSpec(keys16.shape, lambda i: (0, 0, 0))],
        out_specs=[pl.BlockSpec((tt, rows), lambda i: (i, 0)),
                   pl.BlockSpec((rows, tt), lambda i: (0, i))],
        out_shape=[jax.ShapeDtypeStruct((n, rows), I32),
                   jax.ShapeDtypeStruct((rows, n), F32)],
        scratch_shapes=[pltpu.VMEM((rows, tt), I32)],
        compiler_params=_cparams(("arbitrary",)),
    )(q16, keys16)


PAD_ROWS = 4
HALF_ROWS = 4
_SLOT_OF_ROW = (6, 2, 4, 0, 7, 3, 5, 1)


def _expert_slab(tab_ref, row, late):
    off = pl.multiple_of(row - (PAD_ROWS if late else 0), HALF_ROWS)
    w = tab_ref[pl.ds(off, 8), :]
    lo = pltpu.bitcast(w << 16, F32)
    hi = pltpu.bitcast(w & jnp.uint32(0xFFFF0000), F32)
    return lo, hi


def _peer_u_body(idx_ref, x_ref, gate_ref, tab_ref, act_ref, *, tt):
    rows = PEER_HEADS * PEER_TOPK
    ngrp = rows // 8
    lane = lax.broadcasted_iota(I32, (8, tt), 1)
    sub = lax.broadcasted_iota(I32, (8, LANE), 0)
    m1 = (sub & 2) != 0
    m2 = (sub & 1) != 0
    low = sub < HALF_ROWS

    def token(t, accs):
        x8 = x_ref[t]
        xr = pltpu.roll(x8, HALF_ROWS, 0)
        xa = (jnp.where(low, x8, 0.0), jnp.where(low, 0.0, xr))
        xb = (jnp.where(low, xr, 0.0), jnp.where(low, 0.0, x8))
        new = []
        for gidx in range(ngrp):
            prod = [None] * 8
            for r in range(8):
                slot = _SLOT_OF_ROW[r]
                late = slot % 2
                lo, hi = _expert_slab(tab_ref, idx_ref[t, gidx * 8 + r], late)
                prod[slot] = lo * xa[late] + hi * xb[late]
            c = [prod[2 * k] + prod[2 * k + 1] for k in range(4)]
            d = [c[k] + pltpu.roll(c[k], 2 if k % 2 == 0 else 6, 0) for k in range(4)]
            e0, e1 = jnp.where(m1, d[0], d[1]), jnp.where(m1, d[2], d[3])
            f0 = e0 + pltpu.roll(e0, 1, 0)
            f1 = e1 + pltpu.roll(e1, 7, 0)
            dcol = jnp.sum(jnp.where(m2, f0, f1), axis=1, keepdims=True)
            new.append(jnp.where(lane == t, dcol, accs[gidx]))
        return tuple(new)

    accs = lax.fori_loop(0, tt, token, tuple(jnp.zeros((8, tt), F32) for _ in range(ngrp)))
    d = jnp.concatenate(accs, axis=0)
    act_ref[...] = (jax.nn.gelu(d, approximate=True) * gate_ref[...]).T


def _peer_u(idx, x3, gate, tab, tt):
    n, rows = idx.shape
    body = functools.partial(_peer_u_body, tt=tt)
    return pl.pallas_call(
        body,
        grid=(n // tt,),
        in_specs=[pl.BlockSpec((tt, rows), lambda i: (i, 0), memory_space=pltpu.SMEM),
                  pl.BlockSpec((tt, 8, LANE), lambda i: (i, 0, 0)),
                  pl.BlockSpec((rows, tt), lambda i: (0, i)),
                  pl.BlockSpec(tab.shape, lambda i: (0, 0), pipeline_mode=pl.Buffered(1))],
        out_specs=pl.BlockSpec((tt, rows), lambda i: (i, 0)),
        out_shape=jax.ShapeDtypeStruct((n, rows), F32),
        compiler_params=_cparams(("arbitrary",)),
    )(idx, x3, gate, tab)


def _peer_v_body(idx_ref, act_ref, tab_ref, f_ref, *, tt):
    rows = PEER_HEADS * PEER_TOPK
    sub = lax.broadcasted_iota(I32, (8, LANE), 0)

    def token(t, carry):
        al = [jnp.zeros((8, LANE), F32) for _ in range(2)]
        ah = [jnp.zeros((8, LANE), F32) for _ in range(2)]
        for r in range(rows):
            lo, hi = _expert_slab(tab_ref, idx_ref[t, r], False)
            a = act_ref[t, r]
            al[r % 2] = al[r % 2] + a * lo
            ah[r % 2] = ah[r % 2] + a * hi
        f_ref[t] = jnp.where(sub < HALF_ROWS, al[0] + al[1], pltpu.roll(ah[0] + ah[1], HALF_ROWS, 0))
        return carry

    lax.fori_loop(0, tt, token, 0)


def _peer_v(idx, act, tab, tt):
    n, rows = idx.shape
    body = functools.partial(_peer_v_body, tt=tt)
    return pl.pallas_call(
        body,
        grid=(n // tt,),
        in_specs=[pl.BlockSpec((tt, rows), lambda i: (i, 0), memory_space=pltpu.SMEM),
                  pl.BlockSpec((tt, rows), lambda i: (i, 0), memory_space=pltpu.SMEM),
                  pl.BlockSpec(tab.shape, lambda i: (0, 0), pipeline_mode=pl.Buffered(1))],
        out_specs=pl.BlockSpec((tt, 8, LANE), lambda i: (i, 0, 0)),
        out_shape=jax.ShapeDtypeStruct((n, 8, LANE), F32),
        compiler_params=_cparams(("arbitrary",)),
    )(idx, act, tab)


def _pack_table(tab):
    n, d = tab.shape
    bits = lax.bitcast_convert_type(tab.astype(jnp.bfloat16), jnp.uint16).astype(U32)
    words = (bits[:, :d // 2] | (bits[:, d // 2:] << 16)).reshape(n * HALF_ROWS, LANE)
    pad = jnp.zeros((PAD_ROWS, LANE), U32)
    return jnp.concatenate([pad, words, pad], axis=0)


def _peer(x1, sc, sh, g, wq, keys16, tab_u, tab_v, tm):
    bx, t, d = x1.shape
    n = bx * t
    q16, h2 = _peerq(x1, sc, sh, g, wq, tm)
    tt = min(128, n)
    idx, gate = _peer_topk(q16.reshape(2 * PEER_HEADS, n, LANE), keys16, tt)
    act = _peer_u(idx, h2.reshape(n, d // LANE, LANE), gate, tab_u, tt)
    f = _peer_v(idx, act, tab_v, tt)
    return f.reshape(bx, t, d // LANE, LANE)


def _pack_w_in(w_in, q_scale):
    d = w_in.shape[0]
    scale = q_scale
    cols = []
    for hd in range(N_HEADS):
        blk = w_in[:, hd * HEAD_DIM:(hd + 1) * HEAD_DIM] * scale
        z = jnp.zeros((d, HEAD_DIM), w_in.dtype)
        cols += [blk, z] if hd // GQA == 0 else [z, blk]
    o0 = N_HEADS * HEAD_DIM
    o1 = o0 + 6 * N_KV * HEAD_DIM
    o2 = o1 + 3 * N_HEADS
    kvw = N_KV * HEAD_DIM
    wa = jnp.concatenate([w_in[:, o0:o1], w_in[:, o2:]], axis=1)
    cols.append(w_in[:, o0 + 3 * kvw:o0 + 4 * kvw])
    cols.append(w_in[:, o0 + 5 * kvw:o0 + 6 * kvw])
    cols.append(w_in[:, o1:o2])
    cols.append(jnp.zeros((d, LANE - 3 * N_HEADS), w_in.dtype))
    wbt = jnp.concatenate(cols, axis=1).T
    return wa.astype(BF16), wbt.astype(BF16)


def _pack_w_out(w_out):
    d = w_out.shape[1]
    rows = []
    for hd in range(N_HEADS):
        blk = w_out[hd * HEAD_DIM:(hd + 1) * HEAD_DIM]
        z = jnp.zeros((HEAD_DIM, d), w_out.dtype)
        rows += [blk, z] if hd // GQA == 0 else [z, blk]
    wa = jnp.concatenate(rows, axis=0).astype(BF16)
    wp = w_out[N_HEADS * HEAD_DIM:].astype(BF16)
    return wa, wp


def _pack_compress(alpha, phi):
    a = jnp.concatenate([alpha[0], alpha[0], alpha[1], alpha[1]], axis=1)
    z = jnp.zeros((HEAD_DIM, HEAD_DIM), phi.dtype)
    blocks = [phi[0], phi[0], phi[1], phi[1]]
    p = jnp.concatenate([jnp.concatenate([blocks[r] if c == r else z for c in range(4)], axis=1)
                         for r in range(4)], axis=0)
    return a, p


def kernel(x_prompt, x_sample, cache_kv, cache_win, state_pool, page_table, c_prompt, c_sample,
           w_ada, b_ada, norm_g, w_in, w_out, cmp_alpha, cmp_phi, rel_bias, pool_w, pool_scale,
           peer_wq, peer_keys, peer_u, peer_v):
    depth = w_ada.shape[0]
    assert depth == 1 and x_sample.shape[1] == 1
    l = 0
    b, s, d = x_prompt.shape
    db = x_sample.shape[0]
    n_pool, page = cache_kv.shape[1], cache_kv.shape[2]
    n_pages = page_table.shape[1]
    past = n_pages * page
    wbuf = cache_win.shape[2]
    tq = WINDOW // 2

    w_in_a, w_in_bt = _pack_w_in(w_in[l], HEAD_DIM ** -0.5 * LOG2E)
    _, w_in_bt_s = _pack_w_in(w_in[l], HEAD_DIM ** -0.5)
    wa, wp = _pack_w_out(w_out[l])
    a_cmp, phi_cmp = _pack_compress(cmp_alpha[l], cmp_phi[l])
    wq = peer_wq[l].astype(BF16)
    keys16 = peer_keys[l].reshape(2 * PEER_HEADS, PEER_KEYS, LANE).astype(BF16)
    tab_u = _pack_table(peer_u[l])
    tab_v = _pack_table(peer_v[l])
    g = norm_g[l]

    c_all = jnp.concatenate([c_prompt, c_sample], axis=0)
    mod = _mod(c_all, w_ada[l], b_ada[l]).reshape(b + db, 6, d)
    mod_p = [mod[:b, k][:, None, :] for k in range(6)]
    mod_s = [mod[b:, k][None, :, :] for k in range(6)]

    tm = min(512, s)
    kv4, kk, win, pool_in, q_t, gates_t, v_t = _inproj(x_prompt, mod_p[1], mod_p[0], g[0], w_in_a, w_in_bt, tm, tq)
    kcvc = _compress(kv4, a_cmp, phi_cmp, min(1024, s))
    o_attn = _nsa_prompt(q_t, kk, v_t, kcvc.reshape(b, s // SEL_BLOCK, 512), gates_t, rel_bias, tq)
    prev0 = jnp.zeros((b, 16, POOL_DIM), F32)
    o_pool = _pool(pool_in, prev0, pool_w[l], pool_scale[l], min(1024, s), 0)
    x1 = _outproj(o_attn, o_pool, x_prompt, mod_p[2], g[1], wa, wp, tm)
    f = _peer(x1, mod_p[4], mod_p[3], g[2], wq, keys16, tab_u, tab_v, tm)
    y_prompt = _final(f, x1, mod_p[5], g[3], tm)

    kv_prompt = kv4.reshape(1, b, s, 4, N_KV, HEAD_DIM)
    win_prompt = win[:, s - min(WINDOW, s):].reshape(1, b, min(WINDOW, s), 2, N_KV, HEAD_DIM)
    pool_prompt = pool_in[:, s - POOL_STATE:][None]

    xs = x_sample.reshape(1, db, d)
    kv4_s, _, win_s, pool_s, q_t_s, gates_t_s = _inproj(xs, mod_s[1], mod_s[0], g[0], w_in_a, w_in_bt_s, db, None)
    cache_l = jnp.transpose(cache_kv, (0, 1, 3, 4, 5, 2))
    wsum = _wsum_paged(cache_l, page_table, cmp_alpha[l])
    q8 = q_t_s[0].T.reshape(db, N_HEADS, LANE)
    kv4n = kv4_s.reshape(db, 1, 512)
    winn = win_s.reshape(db, 1, 256)
    gat = gates_t_s[0].T.reshape(db, 1, LANE)
    cw3 = cache_win[l].reshape(db, wbuf, 256)
    rbt = rel_bias.T
    ocw, sel_idx = _nsa_s1(q8, wsum.reshape(db, past // SEL_BLOCK, 512), cw3, winn, gat, rbt,
                           phi_cmp[:LANE, :LANE], phi_cmp[LANE:, LANE:], past)
    sel_flat = sel_idx[:, :N_KV, :N_SEL - 1].reshape(-1)
    o_attn_s = _nsa_s2(sel_flat, page_table, q8, cache_l, kv4n, gat, rbt, ocw, past)
    ext = jnp.concatenate([jnp.zeros((db, 1, POOL_DIM), F32), state_pool[l], pool_s.reshape(db, 1, POOL_DIM)], axis=1)
    o_pool_s = _pool(ext[:, 1:], jnp.zeros((db, 16, POOL_DIM), F32), pool_w[l], pool_scale[l], 16,
                     past - POOL_STATE)[:, POOL_STATE:]
    x1s = _outproj(o_attn_s.reshape(1, db, N_HEADS * LANE).astype(BF16), o_pool_s.reshape(1, db, POOL_DIM),
                   xs, mod_s[2], g[1], wa, wp, db)
    fs = _peer(x1s, mod_s[4], mod_s[3], g[2], wq, keys16, tab_u, tab_v, db)
    y_sample = _final(fs, x1s, mod_s[5], g[3], db).reshape(db, 1, d)

    kv_sample = kv4_s.reshape(1, db, 1, 4, N_KV, HEAD_DIM)
    win_sample = jnp.concatenate([cache_win[l][:, 1:], win_s.reshape(db, 1, 2, N_KV, HEAD_DIM)], axis=1)[None]
    pool_sample = ext[:, 2:][None]
    return (y_prompt, y_sample, kv_prompt, kv_sample, win_prompt, win_sample, pool_prompt, pool_sample)
```

```python
import functools
import math

import numpy as np
import jax
import jax.numpy as jnp
from jax import lax
from jax.experimental import pallas as pl
from jax.experimental.pallas import tpu as pltpu

F32, BF16, I32, U32 = jnp.float32, jnp.bfloat16, jnp.int32, jnp.uint32

D_MODEL = 1024
N_HEADS = 8
HEAD_DIM = 64
N_KV = 2
GQA = 4
CMP_BLOCK = 32
SEL_BLOCK = 64
N_SEL = 16
WINDOW = 512
N_BUCKETS = 32
REL_MAX_DIST = 128
POOL_WINDOWS = (2, 4, 8, 16)
POOL_DIM = 512
POOL_GROUP = 128
POOL_STATE = 15
PEER_HEADS = 8
PEER_KEYS = 128
PEER_TOPK = 16
EPS = 1e-6
LANE = 128
NEG = -1e30
BIG = 1e30
REMOVED = -3e38
LOG2E = math.log2(math.e)
SEL_PENALTY = -NEG
VMEM_LIMIT = 56 * 1024 * 1024

A_KV4, A_WIN, A_POOL, A_END = 0, 512, 768, 1280
B_Q, B_V, B_GATE, B_END = 0, 1024, 1280, 1408


def _cparams(sem):
    return pltpu.CompilerParams(dimension_semantics=sem, vmem_limit_bytes=VMEM_LIMIT)


def _bucket_thresholds():
    n = np.arange(0, 4 * REL_MAX_DIST)
    exact = N_BUCKETS // 2
    ratio = np.log(np.maximum(n, exact).astype(np.float32) / np.float32(exact)) / np.float32(
        math.log(REL_MAX_DIST / exact))
    large = np.minimum(exact + (ratio * np.float32(N_BUCKETS - exact)).astype(np.int32), N_BUCKETS - 1)
    b = np.where(n < exact, n, large)
    return [int(np.argmax(b >= j)) for j in range(N_BUCKETS)]


_THR = _bucket_thresholds()
FAR_DIST = _THR[N_BUCKETS - 1]


def _bias_from_dist(d, rb):
    val = jnp.full(d.shape, rb(0), F32)
    for j in range(1, N_BUCKETS):
        if _THR[j] == _THR[j - 1] and j > 1:
            continue
        jj = j
        while jj + 1 < N_BUCKETS and _THR[jj + 1] == _THR[j]:
            jj += 1
        val = jnp.where(d >= _THR[j], rb(jj), val)
    return val


def _dot_nt(a, b):
    return lax.dot_general(a, b, (((1,), (1,)), ((), ())), preferred_element_type=F32)


def _mod_body(c_ref, w_ref, b_ref, o_ref):
    c = c_ref[...]
    a = (c * jax.nn.sigmoid(c)).astype(BF16)
    o_ref[...] = jnp.dot(a, w_ref[...].astype(BF16), preferred_element_type=F32) + b_ref[...]


def _mod(c, w_ada, b_ada):
    r, d = c.shape
    n = w_ada.shape[1]
    tn = 1536
    return pl.pallas_call(
        _mod_body,
        grid=(n // tn,),
        in_specs=[pl.BlockSpec((r, d), lambda j: (0, 0)),
                  pl.BlockSpec((d, tn), lambda j: (0, j)),
                  pl.BlockSpec((1, tn), lambda j: (0, j))],
        out_specs=pl.BlockSpec((r, tn), lambda j: (0, j)),
        out_shape=jax.ShapeDtypeStruct((r, n), F32),
        compiler_params=_cparams(("arbitrary",)),
    )(c, w_ada, b_ada.reshape(1, n))


def _norm_mod(x, g, sc, sh):
    ms = jnp.mean(x * x, axis=-1, keepdims=True)
    h = x * lax.rsqrt(ms + EPS) * g
    return h * (1.0 + sc) + sh


def _inproj_body(x_ref, sc_ref, sh_ref, g_ref, wa_ref, wb_ref, kv_ref, kk_ref, win_ref, pool_ref,
                 qt_ref, gt_ref, *vt_refs, tq):
    h = _norm_mod(x_ref[0], g_ref[...], sc_ref[0], sh_ref[0]).astype(BF16)
    u = jnp.dot(h, wa_ref[...], preferred_element_type=F32)
    kv_ref[0] = u[:, A_KV4:A_WIN]
    win_ref[0] = u[:, A_WIN:A_POOL]
    pool_ref[0] = u[:, A_POOL:A_END]
    kk_ref[0, :, 0:LANE] = u[:, A_KV4 + 2 * LANE:A_KV4 + 3 * LANE].astype(BF16)
    kk_ref[0, :, LANE:2 * LANE] = u[:, A_WIN:A_WIN + LANE].astype(BF16)
    ut = _dot_nt(wb_ref[...], h)
    qt_ref[0] = ut[B_Q:B_V].astype(BF16)
    gt_ref[0] = jax.nn.sigmoid(ut[B_GATE:B_END])
    for vt_ref in vt_refs:
        for c in range(ut.shape[1] // tq):
            vt_ref[0, c] = ut[B_V:B_GATE, c * tq:(c + 1) * tq].astype(BF16)


def _mod_specs(t, tm, d, per_row):
    if per_row:
        return pl.BlockSpec((1, tm, d), lambda b, i: (b, i, 0))
    return pl.BlockSpec((1, 1, d), lambda b, i: (b, 0, 0))


def _inproj(x, sc, sh, g, wa, wbt, tm, tq):
    bx, t, d = x.shape
    per_row = sc.shape[1] != 1
    ms = _mod_specs(t, tm, d, per_row)
    widths = (512, 256, 256, 512)
    dtypes = (F32, BF16, F32, F32)
    out_specs = [pl.BlockSpec((1, tm, wd), lambda b, i: (b, i, 0)) for wd in widths]
    out_shape = [jax.ShapeDtypeStruct((bx, t, wd), dt) for wd, dt in zip(widths, dtypes)]
    out_specs += [pl.BlockSpec((1, B_V - B_Q, tm), lambda b, i: (b, 0, i)),
                  pl.BlockSpec((1, B_END - B_GATE, tm), lambda b, i: (b, 0, i))]
    out_shape += [jax.ShapeDtypeStruct((bx, B_V - B_Q, t), BF16),
                  jax.ShapeDtypeStruct((bx, B_END - B_GATE, t), F32)]
    if tq is not None:
        out_specs.append(pl.BlockSpec((1, tm // tq, B_GATE - B_V, tq), lambda b, i: (b, i, 0, 0)))
        out_shape.append(jax.ShapeDtypeStruct((bx, t // tq, B_GATE - B_V, tq), BF16))
    return pl.pallas_call(
        functools.partial(_inproj_body, tq=tq),
        grid=(bx, t // tm),
        in_specs=[pl.BlockSpec((1, tm, d), lambda b, i: (b, i, 0)), ms, ms,
                  pl.BlockSpec((1, d), lambda b, i: (0, 0)),
                  pl.BlockSpec(wa.shape, lambda b, i: (0, 0)),
                  pl.BlockSpec(wbt.shape, lambda b, i: (0, 0))],
        out_specs=out_specs,
        out_shape=out_shape,
        compiler_params=_cparams(("arbitrary", "arbitrary")),
    )(x, sc, sh, g.reshape(1, d), wa, wbt)


def _peerq_body(x_ref, sc_ref, sh_ref, g_ref, w_ref, q_ref, h_ref):
    h = _norm_mod(x_ref[0], g_ref[...], sc_ref[0], sh_ref[0])
    for c in range(h.shape[1] // LANE):
        h_ref[0, :, c, :] = h[:, c * LANE:(c + 1) * LANE]
    u = jnp.dot(h.astype(BF16), w_ref[...], preferred_element_type=F32)
    for j in range(2 * PEER_HEADS):
        q_ref[j, 0] = u[:, j * LANE:(j + 1) * LANE].astype(BF16)


def _peerq(x, sc, sh, g, w, tm):
    bx, t, d = x.shape
    per_row = sc.shape[1] != 1
    ms = _mod_specs(t, tm, d, per_row)
    nq = 2 * PEER_HEADS
    return pl.pallas_call(
        _peerq_body,
        grid=(bx, t // tm),
        in_specs=[pl.BlockSpec((1, tm, d), lambda b, i: (b, i, 0)), ms, ms,
                  pl.BlockSpec((1, d), lambda b, i: (0, 0)),
                  pl.BlockSpec((d, nq * LANE), lambda b, i: (0, 0))],
        out_specs=[pl.BlockSpec((nq, 1, tm, LANE), lambda b, i: (0, b, i, 0)),
                   pl.BlockSpec((1, tm, d // LANE, LANE), lambda b, i: (b, i, 0, 0))],
        out_shape=[jax.ShapeDtypeStruct((nq, bx, t, LANE), BF16),
                   jax.ShapeDtypeStruct((bx, t, d // LANE, LANE), F32)],
        compiler_params=_cparams(("arbitrary", "arbitrary")),
    )(x, sc, sh, g.reshape(1, d), w)


def _compress_rows(rows, a, phi):
    n = rows.shape[0] // CMP_BLOCK
    w = jnp.sum(rows.reshape(n, CMP_BLOCK, rows.shape[1]) * a[None], axis=1)
    return jnp.dot(w, phi, preferred_element_type=F32, precision=lax.Precision.HIGHEST)


def _compress_body(x_ref, a_ref, phi_ref, o_ref):
    o_ref[0] = _compress_rows(x_ref[0], a_ref[...], phi_ref[...])


def _compress(kv4, a, phi, tm):
    b, s, _ = kv4.shape
    return pl.pallas_call(
        _compress_body,
        grid=(b, s // tm),
        in_specs=[pl.BlockSpec((1, tm, 256), lambda bi, i: (bi, i, 0)),
                  pl.BlockSpec((CMP_BLOCK, 256), lambda bi, i: (0, 0)),
                  pl.BlockSpec((256, 256), lambda bi, i: (0, 0))],
        out_specs=pl.BlockSpec((1, tm // CMP_BLOCK, 256), lambda bi, i: (bi, i, 0)),
        out_shape=jax.ShapeDtypeStruct((b, s // CMP_BLOCK, 256), F32),
        compiler_params=_cparams(("arbitrary", "arbitrary")),
    )(kv4, a, phi)


PAGES_PER_STEP = 4


def _wsum_paged_body(pt_ref, *refs):
    page_refs, a_ref, o_ref = refs[:PAGES_PER_STEP], refs[PAGES_PER_STEP], refs[PAGES_PER_STEP + 1]
    a = a_ref[...]
    page = a.shape[1]
    per = page // CMP_BLOCK
    nrow = PAGES_PER_STEP * per
    rowi = lax.broadcasted_iota(I32, (nrow, page), 0)
    posb = lax.broadcasted_iota(I32, (nrow, page), 1) // CMP_BLOCK
    acc = jnp.zeros((nrow, a.shape[0]), F32)
    for p, p_ref in enumerate(page_refs):
        z = p_ref[0, 0].reshape(a.shape) * a
        z_hi = z.astype(BF16)
        z_lo = (z - z_hi.astype(F32)).astype(BF16)
        member = jnp.where(rowi == p * per + posb, 1.0, 0.0).astype(BF16)
        acc = acc + _dot_nt(member, z_hi) + _dot_nt(member, z_lo)
    o_ref[0] = acc


def _wsum_paged(cache_t, page_table, alpha):
    db, n_pages = page_table.shape
    page = cache_t.shape[5]
    per = page // CMP_BLOCK
    at = jnp.tile(jnp.transpose(alpha, (0, 2, 1)), (1, 1, per))
    a_rows = jnp.broadcast_to(at[:, None], (2, N_KV, HEAD_DIM, page)).reshape(2 * N_KV * HEAD_DIM, page)

    def page_spec(r):
        return pl.BlockSpec((1, 1, 2, N_KV, HEAD_DIM, page),
                            lambda b, i, pt: (0, pt[b, PAGES_PER_STEP * i + r], 0, 0, 0, 0))

    grid_spec = pltpu.PrefetchScalarGridSpec(
        num_scalar_prefetch=1,
        grid=(db, n_pages // PAGES_PER_STEP),
        in_specs=[page_spec(r) for r in range(PAGES_PER_STEP)]
        + [pl.BlockSpec(a_rows.shape, lambda b, i, pt: (0, 0))],
        out_specs=pl.BlockSpec((1, PAGES_PER_STEP * per, a_rows.shape[0]), lambda b, i, pt: (b, i, 0)),
    )
    return pl.pallas_call(
        _wsum_paged_body,
        grid_spec=grid_spec,
        out_shape=jax.ShapeDtypeStruct((db, n_pages * per, a_rows.shape[0]), F32),
        compiler_params=_cparams(("arbitrary", "arbitrary")),
    )(page_table, *([cache_t] * PAGES_PER_STEP), a_rows)


def _select_blocks(score, n_pick):
    lane = lax.broadcasted_iota(I32, score.shape, 1)
    nl = score.shape[1]
    sel = jnp.zeros(score.shape, jnp.bool_)
    idxs = []
    for _ in range(n_pick):
        mx = jnp.max(score, axis=1, keepdims=True)
        idx = jnp.min(jnp.where(score == mx, lane, nl), axis=1, keepdims=True)
        hit = lane == idx
        sel = sel | hit
        score = jnp.where(hit, REMOVED, score)
        idxs.append(idx)
    return sel, idxs


def _flash_update(slot, qt, kc, vt, bias, mask, acc_sc, m_sc):
    s = jnp.dot(kc, qt, preferred_element_type=F32)
    if bias is not None:
        s = s + bias
    if mask is not None:
        s = jnp.where(mask, s, NEG)
    m_old = m_sc[slot]
    m_new = jnp.maximum(m_old, jnp.max(s, axis=0, keepdims=True))
    alpha = jnp.exp2(m_old - m_new)
    p = jnp.exp2(s - m_new)
    acc_sc[slot] = alpha * acc_sc[slot] + jnp.dot(vt, p.astype(BF16), preferred_element_type=F32)
    m_sc[slot] = m_new


def _select_rows(score, n_pick):
    row = lax.broadcasted_iota(I32, score.shape, 0)
    n = score.shape[0]
    sel = jnp.zeros(score.shape, jnp.bool_)
    for _ in range(n_pick):
        mx = jnp.max(score, axis=0, keepdims=True)
        idx = jnp.min(jnp.where(score == mx, row, n), axis=0, keepdims=True)
        hit = row == idx
        sel = sel | hit
        score = jnp.where(hit, REMOVED, score)
    return sel


def _nsa_body(rb_ref, q_ref, kk_ref, vt_ref, kc_ref, g_ref, o_ref,
              bdiag_sc, bcmp_sc, acc_sc, m_sc, osum_sc, *, seq, tq):
    i = pl.program_id(1)
    q0 = i * tq
    nb = seq // SEL_BLOCK
    per = tq // SEL_BLOCK
    ncopy = bcmp_sc.shape[2]

    @pl.when((pl.program_id(0) == 0) & (i == 0))
    def _build_bias_tiles():
        k = lax.broadcasted_iota(I32, (tq, tq), 0)
        t = lax.broadcasted_iota(I32, (tq, tq), 1)
        rr = lax.broadcasted_iota(I32, (2 * nb, tq), 0) - nb
        tc = lax.broadcasted_iota(I32, (2 * nb, tq), 1)
        for hd in range(N_HEADS):
            rb = lambda j, hd=hd: rb_ref[j, hd]
            kvh, cols = hd // GQA, slice((hd % GQA) * tq, (hd % GQA + 1) * tq)
            for dl in range(2):
                bdiag_sc[kvh, dl, :, cols] = LOG2E * (_bias_from_dist(t - k + dl * tq, rb) - rb(N_BUCKETS - 1))
            for par in range(2):
                for cp in range(ncopy):
                    bcmp_sc[kvh, par, cp, :, cols] = LOG2E * _bias_from_dist(
                        tc - (SEL_BLOCK * (rr + cp * per) + (par + 1) * CMP_BLOCK - 1), rb)

    gw = GQA * tq
    kc2 = kc_ref[0]
    kce, kco = kc2[:, 0:128].astype(BF16), kc2[:, 256:384].astype(BF16)
    vcet, vcot = kc2[:, 128:256].T.astype(BF16), kc2[:, 384:512].T.astype(BF16)
    qpos = q0 + lax.broadcasted_iota(I32, (nb, tq), 1)
    jl = lax.broadcasted_iota(I32, (nb, tq), 0)
    cur = qpos // SEL_BLOCK
    forced = (jl == 0) | (jl == cur) | (jl == cur - 1)
    causal_blk = jl * SEL_BLOCK <= qpos
    qpos_g = q0 + lax.broadcasted_iota(I32, (nb, gw), 1) % tq
    jl_g = lax.broadcasted_iota(I32, (nb, gw), 0)
    valid_e = qpos_g >= SEL_BLOCK * jl_g + CMP_BLOCK - 1
    valid_o = qpos_g >= SEL_BLOCK * jl_g + 2 * CMP_BLOCK - 1
    cmp_copy = (ncopy - i % ncopy) % ncopy
    cmp_start = pl.multiple_of(nb - (i + cmp_copy) * per, 8)
    kk = lax.broadcasted_iota(I32, (tq, gw), 0)
    tt = lax.broadcasted_iota(I32, (tq, gw), 1) % tq
    causal = tt >= kk
    gates = g_ref[0]
    ek = lax.broadcasted_iota(I32, (tq, nb), 0) // SEL_BLOCK
    ej = lax.broadcasted_iota(I32, (tq, nb), 1)

    def reset_state():
        m_sc[...] = jnp.full(m_sc.shape, NEG, F32)
        acc_sc[...] = jnp.zeros(acc_sc.shape, F32)

    def qgroup(kvh):
        return jnp.concatenate([q_ref[0, hd * LANE:(hd + 1) * LANE, :]
                                for hd in range(kvh * GQA, (kvh + 1) * GQA)], axis=1)

    vrow = lax.broadcasted_iota(I32, (LANE, tq), 0) // HEAD_DIM
    vrow2 = lax.broadcasted_iota(I32, (LANE, 2 * tq), 0) // HEAD_DIM
    ek2 = lax.broadcasted_iota(I32, (2 * tq, nb), 0) // SEL_BLOCK
    ej2 = lax.broadcasted_iota(I32, (2 * tq, nb), 1)

    def kv_chunk(c, branch, kvh):
        start = pl.multiple_of(c * tq, tq)
        v = vt_ref[0, c, branch * LANE:(branch + 1) * LANE, :]
        return (kk_ref[0, pl.ds(start, tq), branch * LANE:(branch + 1) * LANE],
                jnp.where(vrow == kvh, v, jnp.ones_like(v)))

    def finish(kvh, branch):
        lrow = (1 - kvh) * HEAD_DIM
        acc = acc_sc[0]
        o = acc / acc[lrow:lrow + 1, :]
        for g in range(GQA):
            hd = kvh * GQA + g
            r = branch * N_HEADS + hd
            osum_sc[hd] += gates[r:r + 1, :] * o[:, g * tq:(g + 1) * tq]

    scores = []
    for kvh in range(N_KV):
        qt = qgroup(kvh)
        s_e = jnp.dot(kce, qt, preferred_element_type=F32) + bcmp_sc[kvh, 0, cmp_copy, pl.ds(cmp_start, nb), :]
        s_o = jnp.dot(kco, qt, preferred_element_type=F32) + bcmp_sc[kvh, 1, cmp_copy, pl.ds(cmp_start, nb), :]
        s_e = jnp.where(valid_e, s_e, NEG)
        s_o = jnp.where(valid_o, s_o, NEG)
        mx = jnp.maximum(jnp.max(s_e, axis=0, keepdims=True), jnp.max(s_o, axis=0, keepdims=True))
        e_e = jnp.where(valid_e, jnp.exp2(s_e - mx), 0.0)
        e_o = jnp.where(valid_o, jnp.exp2(s_o - mx), 0.0)
        den = jnp.maximum(jnp.sum(e_e, axis=0, keepdims=True) + jnp.sum(e_o, axis=0, keepdims=True), 1e-30)
        p_e, p_o = e_e / den, e_o / den
        pe = p_e + p_o
        imp = (pe[:, 0:tq] + pe[:, tq:2 * tq]) + (pe[:, 2 * tq:3 * tq] + pe[:, 3 * tq:4 * tq])
        o_c = (jnp.dot(vcet, p_e.astype(BF16), preferred_element_type=F32)
               + jnp.dot(vcot, p_o.astype(BF16), preferred_element_type=F32))
        for g in range(GQA):
            hd = kvh * GQA + g
            osum_sc[hd] = gates[hd:hd + 1, :] * o_c[:, g * tq:(g + 1) * tq]
        score = jnp.where(forced, BIG, imp)
        scores.append(jnp.where(causal_blk, score, NEG))

    sel_all = _select_rows(jnp.concatenate(scores, axis=1), min(N_SEL, nb))

    for kvh in range(N_KV):
        pen = jnp.where(sel_all[:, kvh * tq:(kvh + 1) * tq], 0.0, -SEL_PENALTY).astype(BF16)
        qt = qgroup(kvh)
        qaug = jnp.concatenate([qt, jnp.concatenate([pen] * GQA, axis=1)], axis=0)

        def kaug(c, kc):
            expand = jnp.where(ej == ek + c * per, 1.0, 0.0).astype(BF16)
            return jnp.concatenate([kc, expand], axis=1)

        reset_state()

        def far_pair(c2, carry):
            c = 2 * c2
            start = pl.multiple_of(c * tq, 2 * tq)
            kc = kk_ref[0, pl.ds(start, 2 * tq), 0:LANE]
            v = jnp.concatenate([vt_ref[0, c, 0:LANE, :], vt_ref[0, c + 1, 0:LANE, :]], axis=1)
            vt = jnp.where(vrow2 == kvh, v, jnp.ones_like(v))
            expand = jnp.where(ej2 == ek2 + c * per, 1.0, 0.0).astype(BF16)
            _flash_update(0, qaug, jnp.concatenate([kc, expand], axis=1), vt, None, None, acc_sc, m_sc)
            return carry

        n_far = jnp.maximum(i - 1, 0)
        lax.fori_loop(0, n_far // 2, far_pair, 0)

        @pl.when(n_far % 2 == 1)
        def _far_single():
            kc, vt = kv_chunk(n_far - 1, 0, kvh)
            _flash_update(0, qaug, kaug(n_far - 1, kc), vt, None, None, acc_sc, m_sc)

        @pl.when(i >= 1)
        def _prev_chunk():
            kc, vt = kv_chunk(i - 1, 0, kvh)
            _flash_update(0, qaug, kaug(i - 1, kc), vt, bdiag_sc[kvh, 1], None, acc_sc, m_sc)

        kc, vt = kv_chunk(i, 0, kvh)
        _flash_update(0, qaug, kaug(i, kc), vt, bdiag_sc[kvh, 0], causal, acc_sc, m_sc)
        finish(kvh, 1)

        reset_state()

        @pl.when(i >= 2)
        def _win_far():
            kc, vt = kv_chunk(i - 2, 1, kvh)
            _flash_update(0, qt, kc, vt, None, kk > tt, acc_sc, m_sc)

        @pl.when(i >= 1)
        def _win_prev():
            kc, vt = kv_chunk(i - 1, 1, kvh)
            _flash_update(0, qt, kc, vt, bdiag_sc[kvh, 1], None, acc_sc, m_sc)

        kc, vt = kv_chunk(i, 1, kvh)
        _flash_update(0, qt, kc, vt, bdiag_sc[kvh, 0], causal, acc_sc, m_sc)
        finish(kvh, 2)

    for hd in range(N_HEADS):
        o_ref[0, :, hd * LANE:(hd + 1) * LANE] = osum_sc[hd].T.astype(BF16)


def _nsa_prompt(qt, kk, vt, kcvc2, gates_t, rel_bias, tq):
    b, _, s = qt.shape
    nb = s // SEL_BLOCK
    assert WINDOW == 2 * tq and tq + 1 >= FAR_DIST and s % tq == 0
    body = functools.partial(_nsa_body, seq=s, tq=tq)
    return pl.pallas_call(
        body,
        grid=(b, s // tq),
        in_specs=[pl.BlockSpec(memory_space=pltpu.SMEM),
                  pl.BlockSpec((1, N_HEADS * LANE, tq), lambda bi, i: (bi, 0, i)),
                  pl.BlockSpec((1, s, 2 * LANE), lambda bi, i: (bi, 0, 0)),
                  pl.BlockSpec((1, s // tq, 2 * LANE, tq), lambda bi, i: (bi, 0, 0, 0)),
                  pl.BlockSpec((1, nb, 512), lambda bi, i: (bi, 0, 0)),
                  pl.BlockSpec((1, LANE, tq), lambda bi, i: (bi, 0, i))],
        out_specs=pl.BlockSpec((1, tq, N_HEADS * LANE), lambda bi, i: (bi, i, 0)),
        out_shape=jax.ShapeDtypeStruct((b, s, N_HEADS * LANE), BF16),
        scratch_shapes=[pltpu.VMEM((N_KV, 2, tq, GQA * tq), F32),
                        pltpu.VMEM((N_KV, 2, max(1, 8 // (tq // SEL_BLOCK)), 2 * nb, GQA * tq), F32),
                        pltpu.VMEM((1, LANE, GQA * tq), F32),
                        pltpu.VMEM((1, 1, GQA * tq), F32),
                        pltpu.VMEM((N_HEADS, LANE, tq), F32)],
        compiler_params=_cparams(("arbitrary", "arbitrary")),
    )(rel_bias, qt, kk, vt, kcvc2, gates_t)


def _head_rows(x):
    sub = lax.broadcasted_iota(I32, (N_HEADS, LANE), 0)
    lane = lax.broadcasted_iota(I32, (N_HEADS, LANE), 1)
    return jnp.sum(jnp.where(sub == lane, jnp.broadcast_to(x, (N_HEADS, LANE)), 0.0), axis=1, keepdims=True)


def _nsa_s1_body(q_ref, kc_ref, cw_ref, wn_ref, g_ref, rbt_ref, phik_ref, phiv_ref, ocw_ref, idx_ref, *, past):
    nb = kc_ref.shape[1]
    wbuf = cw_ref.shape[1]
    q = q_ref[0]
    rbt = rbt_ref[...]
    rb = lambda j: rbt[:, j:j + 1]
    gates = g_ref[0]
    g_c = _head_rows(gates[:, 0:LANE])
    g_w = _head_rows(pltpu.roll(gates, LANE - 2 * N_HEADS, 1))

    ws = kc_ref[0]
    hi_dot = functools.partial(jnp.dot, preferred_element_type=F32, precision=lax.Precision.HIGHEST)
    phik, phiv = phik_ref[...], phiv_ref[...]
    kc2 = jnp.concatenate([hi_dot(ws[:, 0:128], phik), hi_dot(ws[:, 128:256], phiv),
                           hi_dot(ws[:, 256:384], phik), hi_dot(ws[:, 384:512], phiv)], axis=1)
    jl = lax.broadcasted_iota(I32, (N_HEADS, nb), 1)
    d_e = past - (SEL_BLOCK * jl + CMP_BLOCK - 1)
    d_o = past - (SEL_BLOCK * jl + 2 * CMP_BLOCK - 1)
    s_e = _dot_nt(q, kc2[:, 0:128].astype(BF16)) + _bias_from_dist(d_e, rb)
    s_o = _dot_nt(q, kc2[:, 256:384].astype(BF16)) + _bias_from_dist(d_o, rb)
    valid_e, valid_o = d_e >= 0, d_o >= 0
    s_e = jnp.where(valid_e, s_e, NEG)
    s_o = jnp.where(valid_o, s_o, NEG)
    mx = jnp.maximum(jnp.max(s_e, axis=1, keepdims=True), jnp.max(s_o, axis=1, keepdims=True))
    e_e = jnp.where(valid_e, jnp.exp(s_e - mx), 0.0)
    e_o = jnp.where(valid_o, jnp.exp(s_o - mx), 0.0)
    den = jnp.maximum(jnp.sum(e_e, axis=1, keepdims=True) + jnp.sum(e_o, axis=1, keepdims=True), 1e-30)
    p_e, p_o = e_e / den, e_o / den
    o_c = (jnp.dot(p_e.astype(BF16), kc2[:, 128:256].astype(BF16), preferred_element_type=F32)
           + jnp.dot(p_o.astype(BF16), kc2[:, 384:512].astype(BF16), preferred_element_type=F32))

    pe = p_e + p_o
    imp = jnp.concatenate([jnp.sum(pe[0:GQA], axis=0, keepdims=True),
                           jnp.sum(pe[GQA:2 * GQA], axis=0, keepdims=True)], axis=0)
    jl2 = lax.broadcasted_iota(I32, (N_KV, nb), 1)
    cur = past // SEL_BLOCK
    forced = (jl2 == 0) | (jl2 == cur) | (jl2 == cur - 1)
    score = jnp.where(forced, BIG, imp)
    score = jnp.where(jl2 * SEL_BLOCK <= past, score, NEG)
    _, idxs = _select_blocks(score, N_SEL - 1)
    lane = lax.broadcasted_iota(I32, (N_KV, LANE), 1)
    out_idx = jnp.zeros((N_KV, LANE), I32)
    for r, ix in enumerate(idxs):
        out_idx = jnp.where(lane == r, ix, out_idx)
    idx_ref[0] = jnp.concatenate([out_idx, jnp.zeros((8 - N_KV, LANE), I32)], axis=0)

    cw = cw_ref[0]
    il = lax.broadcasted_iota(I32, (N_HEADS, wbuf), 1)
    dq = wbuf - il
    s_w = _dot_nt(q, cw[:, 0:128].astype(BF16)) + _bias_from_dist(dq, rb)
    valid_w = dq < WINDOW
    s_w = jnp.where(valid_w, s_w, NEG)
    wn = wn_ref[0]
    qf = q.astype(F32)
    s_n = jnp.sum(qf * wn[:, 0:128].astype(BF16).astype(F32), axis=1, keepdims=True) + rb(0)
    mw = jnp.maximum(jnp.max(s_w, axis=1, keepdims=True), s_n)
    e_w = jnp.where(valid_w, jnp.exp(s_w - mw), 0.0)
    e_n = jnp.exp(s_n - mw)
    den_w = jnp.sum(e_w, axis=1, keepdims=True) + e_n
    o_w = (jnp.dot(e_w.astype(BF16), cw[:, 128:256].astype(BF16), preferred_element_type=F32)
           + e_n * wn[:, 128:256]) / den_w
    ocw_ref[0] = g_c * o_c + g_w * o_w


def _nsa_s1(q8, wsum2, cache_win3, winn, gates, rbt, phik, phiv, past):
    db = q8.shape[0]
    nb = wsum2.shape[1]
    wbuf = cache_win3.shape[1]
    body = functools.partial(_nsa_s1_body, past=past)
    return pl.pallas_call(
        body,
        grid=(db,),
        in_specs=[pl.BlockSpec((1, N_HEADS, LANE), lambda b: (b, 0, 0)),
                  pl.BlockSpec((1, nb, 512), lambda b: (b, 0, 0)),
                  pl.BlockSpec((1, wbuf, 256), lambda b: (b, 0, 0)),
                  pl.BlockSpec((1, 1, 256), lambda b: (b, 0, 0)),
                  pl.BlockSpec((1, 1, LANE), lambda b: (b, 0, 0)),
                  pl.BlockSpec((N_HEADS, N_BUCKETS), lambda b: (0, 0)),
                  pl.BlockSpec((LANE, LANE), lambda b: (0, 0)),
                  pl.BlockSpec((LANE, LANE), lambda b: (0, 0))],
        out_specs=[pl.BlockSpec((1, N_HEADS, LANE), lambda b: (b, 0, 0)),
                   pl.BlockSpec((1, 8, LANE), lambda b: (b, 0, 0))],
        out_shape=[jax.ShapeDtypeStruct((db, N_HEADS, LANE), F32),
                   jax.ShapeDtypeStruct((db, 8, LANE), I32)],
        compiler_params=_cparams(("arbitrary",)),
    )(q8, wsum2, cache_win3, winn, gates, rbt, phik, phiv)


def _nsa_s2_body(sel_ref, pt_ref, q_ref, blk0_ref, blk1_ref, kvn_ref, g_ref, rbt_ref, ocw_ref, o_ref,
                 acc_sc, m_sc, l_sc, *, past, npick):
    b, k = pl.program_id(0), pl.program_id(1)
    q = q_ref[0]
    rbt = rbt_ref[...]
    rb = lambda j: rbt[:, j:j + 1]
    kvn = kvn_ref[0]
    top = lax.broadcasted_iota(I32, (N_HEADS, 1), 0) < GQA

    @pl.when(k == 0)
    def _init():
        s_n = jnp.sum(q.astype(F32) * kvn[:, 256:384].astype(BF16).astype(F32), axis=1, keepdims=True) + rb(0)
        m_sc[...] = s_n
        l_sc[...] = jnp.ones(l_sc.shape, F32)
        acc_sc[...] = jnp.where(top, kvn[:, 384:384 + HEAD_DIM], kvn[:, 384 + HEAD_DIM:512])

    j0 = sel_ref[(b * N_KV + 0) * npick + k]
    j1 = sel_ref[(b * N_KV + 1) * npick + k]
    kt0 = blk0_ref[0, 0, 0, 0].astype(BF16)
    vt0 = blk0_ref[0, 0, 1, 0].astype(BF16)
    kt1 = blk1_ref[0, 0, 0, 0].astype(BF16)
    vt1 = blk1_ref[0, 0, 1, 0].astype(BF16)
    page = kt0.shape[1]
    bpp = page // SEL_BLOCK
    s = jnp.concatenate([jnp.dot(q[0:GQA, 0:HEAD_DIM], kt0, preferred_element_type=F32),
                         jnp.dot(q[GQA:, HEAD_DIM:], kt1, preferred_element_type=F32)], axis=0)
    pos = lax.broadcasted_iota(I32, (N_HEADS, page), 1)
    jrow = jnp.where(top, j0, j1)
    d = past - ((jrow // bpp) * page + pos)
    s = s + _bias_from_dist(d, rb)
    s = jnp.where((pos // SEL_BLOCK == jrow % bpp) & (d >= 0), s, NEG)
    m_old = m_sc[...]
    m_new = jnp.maximum(m_old, jnp.max(s, axis=1, keepdims=True))
    alpha = jnp.exp(m_old - m_new)
    p = jnp.exp(s - m_new).astype(BF16)
    pv = jnp.concatenate([_dot_nt(p[0:GQA], vt0), _dot_nt(p[GQA:], vt1)], axis=0)
    l_sc[...] = alpha * l_sc[...] + jnp.sum(p.astype(F32), axis=1, keepdims=True)
    acc_sc[...] = alpha * acc_sc[...] + pv
    m_sc[...] = m_new

    @pl.when(k == npick - 1)
    def _fin():
        g_s = _head_rows(pltpu.roll(g_ref[0], LANE - N_HEADS, 1))
        o = acc_sc[...] / l_sc[...]
        o_ref[0] = ocw_ref[0] + g_s * jnp.concatenate([o, o], axis=1)


def _nsa_s2(sel_flat, page_table, q8, cache_kv_l, kv4n, gates, rbt, ocw, past):
    db = q8.shape[0]
    npick = N_SEL - 1
    page = cache_kv_l.shape[5]
    bpp = page // SEL_BLOCK
    body = functools.partial(_nsa_s2_body, past=past, npick=npick)

    def blk_spec(h):
        def blk_map(b, k, sel, pt):
            j = sel[(b * N_KV + h) * npick + k]
            return (0, pt[b, j // bpp], 1, h, 0, 0)
        return pl.BlockSpec((1, 1, 2, 1, HEAD_DIM, page), blk_map)

    grid_spec = pltpu.PrefetchScalarGridSpec(
        num_scalar_prefetch=2,
        grid=(db, npick),
        in_specs=[pl.BlockSpec((1, N_HEADS, LANE), lambda b, k, sel, pt: (b, 0, 0)),
                  blk_spec(0), blk_spec(1),
                  pl.BlockSpec((1, 1, 512), lambda b, k, sel, pt: (b, 0, 0)),
                  pl.BlockSpec((1, 1, LANE), lambda b, k, sel, pt: (b, 0, 0)),
                  pl.BlockSpec((N_HEADS, N_BUCKETS), lambda b, k, sel, pt: (0, 0)),
                  pl.BlockSpec((1, N_HEADS, LANE), lambda b, k, sel, pt: (b, 0, 0))],
        out_specs=pl.BlockSpec((1, N_HEADS, LANE), lambda b, k, sel, pt: (b, 0, 0)),
        scratch_shapes=[pltpu.VMEM((N_HEADS, HEAD_DIM), F32),
                        pltpu.VMEM((N_HEADS, 1), F32),
                        pltpu.VMEM((N_HEADS, 1), F32)],
    )
    return pl.pallas_call(
        body,
        grid_spec=grid_spec,
        out_shape=jax.ShapeDtypeStruct((db, N_HEADS, LANE), F32),
        compiler_params=_cparams(("arbitrary", "arbitrary")),
    )(sel_flat, page_table, q8, cache_kv_l, cache_kv_l, kv4n, gates, rbt, ocw)


def _pool_body(x_ref, halo_ref, prev_ref, w_ref, sc_ref, o_ref, *, tm, pos_base):
    i = pl.program_id(1)
    halo = jnp.where(i == 0, prev_ref[0], halo_ref[0])
    cur = x_ref[0]
    ext = jnp.concatenate([halo, cur], axis=0)
    sums = {1: ext}
    w = 1
    while w < max(POOL_WINDOWS):
        sums[2 * w] = sums[w] + pltpu.roll(sums[w], w, 0)
        w *= 2
    pos = pos_base + i * tm + lax.broadcasted_iota(I32, (tm, 1), 0)
    outs = []
    for gi, wl in enumerate(POOL_WINDOWS):
        ch = slice(gi * POOL_GROUP, (gi + 1) * POOL_GROUP)
        tot = sums[wl][16:, ch]
        cnt = jnp.minimum(wl, pos + 1).astype(F32)
        d = tot / cnt - cur[:, ch]
        outs.append(jnp.dot(d.astype(BF16), w_ref[gi].astype(BF16), preferred_element_type=F32))
    o_ref[0] = jnp.concatenate(outs, axis=1) * sc_ref[...]


def _pool(p, prev16, pool_w, pool_scale, tm, pos_base):
    b, t, c = p.shape
    body = functools.partial(_pool_body, tm=tm, pos_base=pos_base)
    hb = tm // 16
    return pl.pallas_call(
        body,
        grid=(b, t // tm),
        in_specs=[pl.BlockSpec((1, tm, c), lambda bi, i: (bi, i, 0)),
                  pl.BlockSpec((1, 16, c), lambda bi, i: (bi, jnp.maximum(i * hb - 1, 0), 0)),
                  pl.BlockSpec((1, 16, c), lambda bi, i: (bi, 0, 0)),
                  pl.BlockSpec((len(POOL_WINDOWS), POOL_GROUP, POOL_GROUP), lambda bi, i: (0, 0, 0)),
                  pl.BlockSpec((1, c), lambda bi, i: (0, 0))],
        out_specs=pl.BlockSpec((1, tm, c), lambda bi, i: (bi, i, 0)),
        out_shape=jax.ShapeDtypeStruct((b, t, c), F32),
        compiler_params=_cparams(("arbitrary", "arbitrary")),
    )(p, p, prev16, pool_w, pool_scale.reshape(1, c))


def _rms(x, g):
    return x * lax.rsqrt(jnp.mean(x * x, axis=-1, keepdims=True) + EPS) * g


def _outproj_body(oa_ref, op_ref, x_ref, gate_ref, g_ref, wa_ref, wp_ref, o_ref):
    mix = (jnp.dot(oa_ref[0].astype(BF16), wa_ref[...], preferred_element_type=F32)
           + jnp.dot(op_ref[0].astype(BF16), wp_ref[...], preferred_element_type=F32))
    o_ref[0] = x_ref[0] + gate_ref[0] * _rms(mix, g_ref[...])


def _outproj(oa, op, x, gate, g, wa, wp, tm):
    bx, t, d = x.shape
    per_row = gate.shape[1] != 1
    ms = _mod_specs(t, tm, d, per_row)
    return pl.pallas_call(
        _outproj_body,
        grid=(bx, t // tm),
        in_specs=[pl.BlockSpec((1, tm, oa.shape[2]), lambda b, i: (b, i, 0)),
                  pl.BlockSpec((1, tm, op.shape[2]), lambda b, i: (b, i, 0)),
                  pl.BlockSpec((1, tm, d), lambda b, i: (b, i, 0)), ms,
                  pl.BlockSpec((1, d), lambda b, i: (0, 0)),
                  pl.BlockSpec(wa.shape, lambda b, i: (0, 0)),
                  pl.BlockSpec(wp.shape, lambda b, i: (0, 0))],
        out_specs=pl.BlockSpec((1, tm, d), lambda b, i: (b, i, 0)),
        out_shape=jax.ShapeDtypeStruct((bx, t, d), F32),
        compiler_params=_cparams(("arbitrary", "arbitrary")),
    )(oa, op, x, gate, g.reshape(1, d), wa, wp)


def _final_body(f_ref, x_ref, gate_ref, g_ref, o_ref):
    f = jnp.concatenate([f_ref[0, :, c, :] for c in range(f_ref.shape[2])], axis=1)
    o_ref[0] = x_ref[0] + gate_ref[0] * _rms(f, g_ref[...])


def _final(f, x, gate, g, tm):
    bx, t, d = x.shape
    per_row = gate.shape[1] != 1
    ms = _mod_specs(t, tm, d, per_row)
    xs = pl.BlockSpec((1, tm, d), lambda b, i: (b, i, 0))
    fs = pl.BlockSpec((1, tm, d // LANE, LANE), lambda b, i: (b, i, 0, 0))
    return pl.pallas_call(
        _final_body,
        grid=(bx, t // tm),
        in_specs=[fs, xs, ms, pl.BlockSpec((1, d), lambda b, i: (0, 0))],
        out_specs=xs,
        out_shape=jax.ShapeDtypeStruct((bx, t, d), F32),
        compiler_params=_cparams(("arbitrary", "arbitrary")),
    )(f, x, gate, g.reshape(1, d))


def _topk_rows(s, k, ids=None):
    row = lax.broadcasted_iota(I32, s.shape, 0) if ids is None else ids
    krow = lax.broadcasted_iota(I32, (k, s.shape[1]), 0)
    vals = jnp.zeros((k, s.shape[1]), F32)
    idxs = jnp.zeros((k, s.shape[1]), I32)
    for it in range(k):
        mx = jnp.max(s, axis=0, keepdims=True)
        ix = jnp.min(jnp.where(s == mx, row, jnp.int32(2 ** 30)), axis=0, keepdims=True)
        vals = jnp.where(krow == it, mx, vals)
        idxs = jnp.where(krow == it, ix, idxs)
        s = jnp.where(row == ix, REMOVED, s)
    return vals, idxs


def _pair_candidates(kk, t):
    r = lax.broadcasted_iota(I32, (kk + 7 * 8 + 8, t), 0)
    mid = (r - kk) // 8
    a = jnp.where(r < kk, 0, jnp.where(r < kk + 56, 1 + mid, r - (kk + 56) + 8))
    b = jnp.where(r < kk, r, jnp.where(r < kk + 56, r - kk - 8 * mid, 0))
    return a * kk + b, (a + 1) * (b + 1) <= kk


def _pick_rows(table, ix):
    k = table.shape[0]
    out = jnp.zeros(ix.shape, table.dtype)
    for a in range(k):
        out = jnp.where(ix == a, table[a:a + 1, :], out)
    return out


def _peer_topk_body(q_ref, keys_ref, idx_ref, gate_ref, e_sc):
    kk = PEER_TOPK
    assert kk == 16
    flat_id, reachable = _pair_candidates(kk, q_ref.shape[1])

    def head(h, carry):
        s1 = _dot_nt(keys_ref[2 * h], q_ref[2 * h])
        s2 = _dot_nt(keys_ref[2 * h + 1], q_ref[2 * h + 1])
        v1, i1 = _topk_rows(s1, kk)
        v2, i2 = _topk_rows(s2, kk)
        cand = jnp.concatenate([v1[0:1, :] + v2] + [v1[a:a + 1, :] + v2[0:8, :] for a in range(1, 8)]
                               + [v1[8:16, :] + v2[0:1, :]], axis=0)
        best, flat = _topk_rows(jnp.where(reachable, cand, REMOVED), kk, flat_id)
        e = _pick_rows(i1, flat // kk) * PEER_KEYS + _pick_rows(i2, flat % kk)
        ex = jnp.exp(best - jnp.max(best, axis=0, keepdims=True))
        gate = ex / jnp.sum(ex, axis=0, keepdims=True)
        off = pl.multiple_of(h * kk, kk)
        e_sc[pl.ds(off, kk), :] = e * HALF_ROWS + PAD_ROWS
        gate_ref[pl.ds(off, kk), :] = gate
        return carry

    def head_group(j, carry):
        for r in range(HEADS_PER_BODY):
            head(HEADS_PER_BODY * j + r, carry)
        return carry

    lax.fori_loop(0, PEER_HEADS // HEADS_PER_BODY, head_group, 0)
    idx_ref[...] = e_sc[...].T


def _peer_topk(q16, keys16, tt):
    nq, n, _ = q16.shape
    rows = PEER_HEADS * PEER_TOPK
    return pl.pallas_call(
        _peer_topk_body,
        grid=(n // tt,),
        in_specs=[pl.BlockSpec((nq, tt, LANE), lambda i: (0, i, 0)),
                  pl.BlockSpec(keys16.shape, lambda i: (0, 0, 0))],
        out_specs=[pl.BlockSpec((tt, rows), lambda i: (i, 0)),
                   pl.BlockSpec((rows, tt), lambda i: (0, i))],
        out_shape=[jax.ShapeDtypeStruct((n, rows), I32),
                   jax.ShapeDtypeStruct((rows, n), F32)],
        scratch_shapes=[pltpu.VMEM((rows, tt), I32)],
        compiler_params=_cparams(("arbitrary",)),
    )(q16, keys16)


PAD_ROWS = 4
HALF_ROWS = 4
TOKENS_PER_BODY = 4
HEADS_PER_BODY = 4
_SLOT_OF_ROW = (6, 2, 4, 0, 7, 3, 5, 1)


def _expert_slab(tab_ref, row, late):
    off = pl.multiple_of(row - (PAD_ROWS if late else 0), HALF_ROWS)
    w = tab_ref[pl.ds(off, 8), :]
    lo = pltpu.bitcast(w << 16, F32)
    hi = pltpu.bitcast(w & jnp.uint32(0xFFFF0000), F32)
    return lo, hi


def _peer_u_body(idx_ref, x_ref, gate_ref, tab_ref, act_ref, *, tt):
    rows = PEER_HEADS * PEER_TOPK
    ngrp = rows // 8
    lane = lax.broadcasted_iota(I32, (8, tt), 1)
    sub = lax.broadcasted_iota(I32, (8, LANE), 0)
    m1 = (sub & 2) != 0
    m2 = (sub & 1) != 0
    low = sub < HALF_ROWS

    def token(t, accs):
        x8 = x_ref[t]
        xr = pltpu.roll(x8, HALF_ROWS, 0)
        xa = (jnp.where(low, x8, 0.0), jnp.where(low, 0.0, xr))
        xb = (jnp.where(low, xr, 0.0), jnp.where(low, 0.0, x8))
        new = []
        for gidx in range(ngrp):
            prod = [None] * 8
            for r in range(8):
                slot = _SLOT_OF_ROW[r]
                late = slot % 2
                lo, hi = _expert_slab(tab_ref, idx_ref[t, gidx * 8 + r], late)
                prod[slot] = lo * xa[late] + hi * xb[late]
            c = [prod[2 * k] + prod[2 * k + 1] for k in range(4)]
            d = [c[k] + pltpu.roll(c[k], 2 if k % 2 == 0 else 6, 0) for k in range(4)]
            e0, e1 = jnp.where(m1, d[0], d[1]), jnp.where(m1, d[2], d[3])
            f0 = e0 + pltpu.roll(e0, 1, 0)
            f1 = e1 + pltpu.roll(e1, 7, 0)
            dcol = jnp.sum(jnp.where(m2, f0, f1), axis=1, keepdims=True)
            new.append(jnp.where(lane == t, dcol, accs[gidx]))
        return tuple(new)

    def token_group(j, accs):
        for r in range(TOKENS_PER_BODY):
            accs = token(TOKENS_PER_BODY * j + r, accs)
        return accs

    accs = lax.fori_loop(0, tt // TOKENS_PER_BODY, token_group,
                         tuple(jnp.zeros((8, tt), F32) for _ in range(ngrp)))
    d = jnp.concatenate(accs, axis=0)
    act_ref[...] = (jax.nn.gelu(d, approximate=True) * gate_ref[...]).T


def _peer_u(idx, x3, gate, tab, tt):
    n, rows = idx.shape
    body = functools.partial(_peer_u_body, tt=tt)
    return pl.pallas_call(
        body,
        grid=(n // tt,),
        in_specs=[pl.BlockSpec((tt, rows), lambda i: (i, 0), memory_space=pltpu.SMEM),
                  pl.BlockSpec((tt, 8, LANE), lambda i: (i, 0, 0)),
                  pl.BlockSpec((rows, tt), lambda i: (0, i)),
                  pl.BlockSpec(tab.shape, lambda i: (0, 0), pipeline_mode=pl.Buffered(1))],
        out_specs=pl.BlockSpec((tt, rows), lambda i: (i, 0)),
        out_shape=jax.ShapeDtypeStruct((n, rows), F32),
        compiler_params=_cparams(("arbitrary",)),
    )(idx, x3, gate, tab)


def _peer_v_body(idx_ref, act_ref, tab_ref, f_ref, *, tt):
    rows = PEER_HEADS * PEER_TOPK
    sub = lax.broadcasted_iota(I32, (8, LANE), 0)

    def token(t, carry):
        al = [jnp.zeros((8, LANE), F32) for _ in range(2)]
        ah = [jnp.zeros((8, LANE), F32) for _ in range(2)]
        for r in range(rows):
            lo, hi = _expert_slab(tab_ref, idx_ref[t, r], False)
            a = act_ref[t, r]
            al[r % 2] = al[r % 2] + a * lo
            ah[r % 2] = ah[r % 2] + a * hi
        f_ref[t] = jnp.where(sub < HALF_ROWS, al[0] + al[1], pltpu.roll(ah[0] + ah[1], HALF_ROWS, 0))
        return carry

    def token_group(j, carry):
        for r in range(TOKENS_PER_BODY):
            token(TOKENS_PER_BODY * j + r, carry)
        return carry

    lax.fori_loop(0, tt // TOKENS_PER_BODY, token_group, 0)


def _peer_v(idx, act, tab, tt):
    n, rows = idx.shape
    body = functools.partial(_peer_v_body, tt=tt)
    return pl.pallas_call(
        body,
        grid=(n // tt,),
        in_specs=[pl.BlockSpec((tt, rows), lambda i: (i, 0), memory_space=pltpu.SMEM),
                  pl.BlockSpec((tt, rows), lambda i: (i, 0), memory_space=pltpu.SMEM),
                  pl.BlockSpec(tab.shape, lambda i: (0, 0), pipeline_mode=pl.Buffered(1))],
        out_specs=pl.BlockSpec((tt, 8, LANE), lambda i: (i, 0, 0)),
        out_shape=jax.ShapeDtypeStruct((n, 8, LANE), F32),
        compiler_params=_cparams(("arbitrary",)),
    )(idx, act, tab)


def _pack_table(tab):
    n, d = tab.shape
    bits = lax.bitcast_convert_type(tab.astype(jnp.bfloat16), jnp.uint16).astype(U32)
    words = (bits[:, :d // 2] | (bits[:, d // 2:] << 16)).reshape(n * HALF_ROWS, LANE)
    pad = jnp.zeros((PAD_ROWS, LANE), U32)
    return jnp.concatenate([pad, words, pad], axis=0)


def _peer(x1, sc, sh, g, wq, keys16, tab_u, tab_v, tm):
    bx, t, d = x1.shape
    n = bx * t
    q16, h2 = _peerq(x1, sc, sh, g, wq, tm)
    tt = min(128, n)
    idx, gate = _peer_topk(q16.reshape(2 * PEER_HEADS, n, LANE), keys16, tt)
    act = _peer_u(idx, h2.reshape(n, d // LANE, LANE), gate, tab_u, tt)
    f = _peer_v(idx, act, tab_v, tt)
    return f.reshape(bx, t, d // LANE, LANE)


def _pack_w_in(w_in, q_scale):
    d = w_in.shape[0]
    scale = q_scale
    cols = []
    for hd in range(N_HEADS):
        blk = w_in[:, hd * HEAD_DIM:(hd + 1) * HEAD_DIM] * scale
        z = jnp.zeros((d, HEAD_DIM), w_in.dtype)
        cols += [blk, z] if hd // GQA == 0 else [z, blk]
    o0 = N_HEADS * HEAD_DIM
    o1 = o0 + 6 * N_KV * HEAD_DIM
    o2 = o1 + 3 * N_HEADS
    kvw = N_KV * HEAD_DIM
    wa = jnp.concatenate([w_in[:, o0:o1], w_in[:, o2:]], axis=1)
    cols.append(w_in[:, o0 + 3 * kvw:o0 + 4 * kvw])
    cols.append(w_in[:, o0 + 5 * kvw:o0 + 6 * kvw])
    cols.append(w_in[:, o1:o2])
    cols.append(jnp.zeros((d, LANE - 3 * N_HEADS), w_in.dtype))
    wbt = jnp.concatenate(cols, axis=1).T
    return wa.astype(BF16), wbt.astype(BF16)


def _pack_w_out(w_out):
    d = w_out.shape[1]
    rows = []
    for hd in range(N_HEADS):
        blk = w_out[hd * HEAD_DIM:(hd + 1) * HEAD_DIM]
        z = jnp.zeros((HEAD_DIM, d), w_out.dtype)
        rows += [blk, z] if hd // GQA == 0 else [z, blk]
    wa = jnp.concatenate(rows, axis=0).astype(BF16)
    wp = w_out[N_HEADS * HEAD_DIM:].astype(BF16)
    return wa, wp


def _pack_compress(alpha, phi):
    a = jnp.concatenate([alpha[0], alpha[0], alpha[1], alpha[1]], axis=1)
    z = jnp.zeros((HEAD_DIM, HEAD_DIM), phi.dtype)
    blocks = [phi[0], phi[0], phi[1], phi[1]]
    p = jnp.concatenate([jnp.concatenate([blocks[r] if c == r else z for c in range(4)], axis=1)
                         for r in range(4)], axis=0)
    return a, p


def kernel(x_prompt, x_sample, cache_kv, cache_win, state_pool, page_table, c_prompt, c_sample,
           w_ada, b_ada, norm_g, w_in, w_out, cmp_alpha, cmp_phi, rel_bias, pool_w, pool_scale,
           peer_wq, peer_keys, peer_u, peer_v):
    depth = w_ada.shape[0]
    assert depth == 1 and x_sample.shape[1] == 1
    l = 0
    b, s, d = x_prompt.shape
    db = x_sample.shape[0]
    n_pool, page = cache_kv.shape[1], cache_kv.shape[2]
    n_pages = page_table.shape[1]
    past = n_pages * page
    wbuf = cache_win.shape[2]
    tq = WINDOW // 2

    w_in_a, w_in_bt = _pack_w_in(w_in[l], HEAD_DIM ** -0.5 * LOG2E)
    _, w_in_bt_s = _pack_w_in(w_in[l], HEAD_DIM ** -0.5)
    wa, wp = _pack_w_out(w_out[l])
    a_cmp, phi_cmp = _pack_compress(cmp_alpha[l], cmp_phi[l])
    wq = peer_wq[l].astype(BF16)
    keys16 = peer_keys[l].reshape(2 * PEER_HEADS, PEER_KEYS, LANE).astype(BF16)
    tab_u = _pack_table(peer_u[l])
    tab_v = _pack_table(peer_v[l])
    g = norm_g[l]

    c_all = jnp.concatenate([c_prompt, c_sample], axis=0)
    mod = _mod(c_all, w_ada[l], b_ada[l]).reshape(b + db, 6, d)
    mod_p = [mod[:b, k][:, None, :] for k in range(6)]
    mod_s = [mod[b:, k][None, :, :] for k in range(6)]

    tm = min(512, s)
    kv4, kk, win, pool_in, q_t, gates_t, v_t = _inproj(x_prompt, mod_p[1], mod_p[0], g[0], w_in_a, w_in_bt, tm, tq)
    kcvc = _compress(kv4, a_cmp, phi_cmp, min(1024, s))
    o_attn = _nsa_prompt(q_t, kk, v_t, kcvc.reshape(b, s // SEL_BLOCK, 512), gates_t, rel_bias, tq)
    prev0 = jnp.zeros((b, 16, POOL_DIM), F32)
    o_pool = _pool(pool_in, prev0, pool_w[l], pool_scale[l], min(1024, s), 0)
    x1 = _outproj(o_attn, o_pool, x_prompt, mod_p[2], g[1], wa, wp, tm)
    f = _peer(x1, mod_p[4], mod_p[3], g[2], wq, keys16, tab_u, tab_v, tm)
    y_prompt = _final(f, x1, mod_p[5], g[3], tm)

    kv_prompt = kv4.reshape(1, b, s, 4, N_KV, HEAD_DIM)
    win_prompt = win[:, s - min(WINDOW, s):].reshape(1, b, min(WINDOW, s), 2, N_KV, HEAD_DIM)
    pool_prompt = pool_in[:, s - POOL_STATE:][None]

    xs = x_sample.reshape(1, db, d)
    kv4_s, _, win_s, pool_s, q_t_s, gates_t_s = _inproj(xs, mod_s[1], mod_s[0], g[0], w_in_a, w_in_bt_s, db, None)
    cache_l = jnp.transpose(cache_kv, (0, 1, 3, 4, 5, 2))
    wsum = _wsum_paged(cache_l, page_table, cmp_alpha[l])
    q8 = q_t_s[0].T.reshape(db, N_HEADS, LANE)
    kv4n = kv4_s.reshape(db, 1, 512)
    winn = win_s.reshape(db, 1, 256)
    gat = gates_t_s[0].T.reshape(db, 1, LANE)
    cw3 = cache_win[l].reshape(db, wbuf, 256)
    rbt = rel_bias.T
    ocw, sel_idx = _nsa_s1(q8, wsum.reshape(db, past // SEL_BLOCK, 512), cw3, winn, gat, rbt,
                           phi_cmp[:LANE, :LANE], phi_cmp[LANE:, LANE:], past)
    sel_flat = sel_idx[:, :N_KV, :N_SEL - 1].reshape(-1)
    o_attn_s = _nsa_s2(sel_flat, page_table, q8, cache_l, kv4n, gat, rbt, ocw, past)
    ext = jnp.concatenate([jnp.zeros((db, 1, POOL_DIM), F32), state_pool[l], pool_s.reshape(db, 1, POOL_DIM)], axis=1)
    o_pool_s = _pool(ext[:, 1:], jnp.zeros((db, 16, POOL_DIM), F32), pool_w[l], pool_scale[l], 16,
                     past - POOL_STATE)[:, POOL_STATE:]
    x1s = _outproj(o_attn_s.reshape(1, db, N_HEADS * LANE).astype(BF16), o_pool_s.reshape(1, db, POOL_DIM),
                   xs, mod_s[2], g[1], wa, wp, db)
    fs = _peer(x1s, mod_s[4], mod_s[3], g[2], wq, keys16, tab_u, tab_v, db)
    y_sample = _final(fs, x1s, mod_s[5], g[3], db).reshape(db, 1, d)

    kv_sample = kv4_s.reshape(1, db, 1, 4, N_KV, HEAD_DIM)
    win_sample = jnp.concatenate([cache_win[l][:, 1:], win_s.reshape(db, 1, 2, N_KV, HEAD_DIM)], axis=1)[None]
    pool_sample = ext[:, 2:][None]
    return (y_prompt, y_sample, kv_prompt, kv_sample, win_prompt, win_sample, pool_prompt, pool_sample)
```

```python
import functools
import math

import numpy as np
import jax
import jax.numpy as jnp
from jax import lax
from jax.experimental import pallas as pl
from jax.experimental.pallas import tpu as pltpu

F32, BF16, I32, U32 = jnp.float32, jnp.bfloat16, jnp.int32, jnp.uint32

D_MODEL = 1024
N_HEADS = 8
HEAD_DIM = 64
N_KV = 2
GQA = 4
CMP_BLOCK = 32
SEL_BLOCK = 64
N_SEL = 16
WINDOW = 512
N_BUCKETS = 32
REL_MAX_DIST = 128
POOL_WINDOWS = (2, 4, 8, 16)
POOL_DIM = 512
POOL_GROUP = 128
POOL_STATE = 15
PEER_HEADS = 8
PEER_KEYS = 128
PEER_TOPK = 16
EPS = 1e-6
LANE = 128
NEG = -1e30
BIG = 1e30
REMOVED = -3e38
LOG2E = math.log2(math.e)
SEL_PENALTY = -NEG
VMEM_LIMIT = 56 * 1024 * 1024

A_KV4, A_WIN, A_POOL, A_END = 0, 512, 768, 1280
B_Q, B_V, B_GATE, B_END = 0, 1024, 1280, 1408


def _cparams(sem):
    return pltpu.CompilerParams(dimension_semantics=sem, vmem_limit_bytes=VMEM_LIMIT)


def _bucket_thresholds():
    n = np.arange(0, 4 * REL_MAX_DIST)
    exact = N_BUCKETS // 2
    ratio = np.log(np.maximum(n, exact).astype(np.float32) / np.float32(exact)) / np.float32(
        math.log(REL_MAX_DIST / exact))
    large = np.minimum(exact + (ratio * np.float32(N_BUCKETS - exact)).astype(np.int32), N_BUCKETS - 1)
    b = np.where(n < exact, n, large)
    return [int(np.argmax(b >= j)) for j in range(N_BUCKETS)]


_THR = _bucket_thresholds()
FAR_DIST = _THR[N_BUCKETS - 1]


def _bias_from_dist(d, rb):
    val = jnp.full(d.shape, rb(0), F32)
    for j in range(1, N_BUCKETS):
        if _THR[j] == _THR[j - 1] and j > 1:
            continue
        jj = j
        while jj + 1 < N_BUCKETS and _THR[jj + 1] == _THR[j]:
            jj += 1
        val = jnp.where(d >= _THR[j], rb(jj), val)
    return val


def _dot_nt(a, b):
    return lax.dot_general(a, b, (((1,), (1,)), ((), ())), preferred_element_type=F32)


def _mod_body(c_ref, w_ref, b_ref, o_ref):
    c = c_ref[...]
    a = (c * jax.nn.sigmoid(c)).astype(BF16)
    o_ref[...] = jnp.dot(a, w_ref[...].astype(BF16), preferred_element_type=F32) + b_ref[...]


def _mod(c, w_ada, b_ada):
    r, d = c.shape
    n = w_ada.shape[1]
    tn = 1536
    return pl.pallas_call(
        _mod_body,
        grid=(n // tn,),
        in_specs=[pl.BlockSpec((r, d), lambda j: (0, 0)),
                  pl.BlockSpec((d, tn), lambda j: (0, j)),
                  pl.BlockSpec((1, tn), lambda j: (0, j))],
        out_specs=pl.BlockSpec((r, tn), lambda j: (0, j)),
        out_shape=jax.ShapeDtypeStruct((r, n), F32),
        compiler_params=_cparams(("arbitrary",)),
    )(c, w_ada, b_ada.reshape(1, n))


def _norm_mod(x, g, sc, sh):
    ms = jnp.mean(x * x, axis=-1, keepdims=True)
    h = x * lax.rsqrt(ms + EPS) * g
    return h * (1.0 + sc) + sh


def _inproj_body(x_ref, sc_ref, sh_ref, g_ref, wa_ref, wb_ref, kv_ref, kk_ref, win_ref, pool_ref,
                 qt_ref, gt_ref, *vt_refs, tq):
    h = _norm_mod(x_ref[0], g_ref[...], sc_ref[0], sh_ref[0]).astype(BF16)
    u = jnp.dot(h, wa_ref[...], preferred_element_type=F32)
    kv_ref[0] = u[:, A_KV4:A_WIN]
    win_ref[0] = u[:, A_WIN:A_POOL]
    pool_ref[0] = u[:, A_POOL:A_END]
    kk_ref[0, :, 0:LANE] = u[:, A_KV4 + 2 * LANE:A_KV4 + 3 * LANE].astype(BF16)
    kk_ref[0, :, LANE:2 * LANE] = u[:, A_WIN:A_WIN + LANE].astype(BF16)
    ut = _dot_nt(wb_ref[...], h)
    qt_ref[0] = ut[B_Q:B_V].astype(BF16)
    gt_ref[0] = jax.nn.sigmoid(ut[B_GATE:B_END])
    for vt_ref in vt_refs:
        for c in range(ut.shape[1] // tq):
            vt_ref[0, c] = ut[B_V:B_GATE, c * tq:(c + 1) * tq].astype(BF16)


def _mod_specs(t, tm, d, per_row):
    if per_row:
        return pl.BlockSpec((1, tm, d), lambda b, i: (b, i, 0))
    return pl.BlockSpec((1, 1, d), lambda b, i: (b, 0, 0))


def _inproj(x, sc, sh, g, wa, wbt, tm, tq):
    bx, t, d = x.shape
    per_row = sc.shape[1] != 1
    ms = _mod_specs(t, tm, d, per_row)
    widths = (512, 256, 256, 512)
    dtypes = (F32, BF16, F32, F32)
    out_specs = [pl.BlockSpec((1, tm, wd), lambda b, i: (b, i, 0)) for wd in widths]
    out_shape = [jax.ShapeDtypeStruct((bx, t, wd), dt) for wd, dt in zip(widths, dtypes)]
    out_specs += [pl.BlockSpec((1, B_V - B_Q, tm), lambda b, i: (b, 0, i)),
                  pl.BlockSpec((1, B_END - B_GATE, tm), lambda b, i: (b, 0, i))]
    out_shape += [jax.ShapeDtypeStruct((bx, B_V - B_Q, t), BF16),
                  jax.ShapeDtypeStruct((bx, B_END - B_GATE, t), F32)]
    if tq is not None:
        out_specs.append(pl.BlockSpec((1, tm // tq, B_GATE - B_V, tq), lambda b, i: (b, i, 0, 0)))
        out_shape.append(jax.ShapeDtypeStruct((bx, t // tq, B_GATE - B_V, tq), BF16))
    return pl.pallas_call(
        functools.partial(_inproj_body, tq=tq),
        grid=(bx, t // tm),
        in_specs=[pl.BlockSpec((1, tm, d), lambda b, i: (b, i, 0)), ms, ms,
                  pl.BlockSpec((1, d), lambda b, i: (0, 0)),
                  pl.BlockSpec(wa.shape, lambda b, i: (0, 0)),
                  pl.BlockSpec(wbt.shape, lambda b, i: (0, 0))],
        out_specs=out_specs,
        out_shape=out_shape,
        compiler_params=_cparams(("arbitrary", "arbitrary")),
    )(x, sc, sh, g.reshape(1, d), wa, wbt)


def _peerq_body(x_ref, sc_ref, sh_ref, g_ref, w_ref, q_ref, h_ref):
    h = _norm_mod(x_ref[0], g_ref[...], sc_ref[0], sh_ref[0])
    for c in range(h.shape[1] // LANE):
        h_ref[0, :, c, :] = h[:, c * LANE:(c + 1) * LANE]
    u = jnp.dot(h.astype(BF16), w_ref[...], preferred_element_type=F32)
    for j in range(2 * PEER_HEADS):
        q_ref[j, 0] = u[:, j * LANE:(j + 1) * LANE].astype(BF16)


def _peerq(x, sc, sh, g, w, tm):
    bx, t, d = x.shape
    per_row = sc.shape[1] != 1
    ms = _mod_specs(t, tm, d, per_row)
    nq = 2 * PEER_HEADS
    return pl.pallas_call(
        _peerq_body,
        grid=(bx, t // tm),
        in_specs=[pl.BlockSpec((1, tm, d), lambda b, i: (b, i, 0)), ms, ms,
                  pl.BlockSpec((1, d), lambda b, i: (0, 0)),
                  pl.BlockSpec((d, nq * LANE), lambda b, i: (0, 0))],
        out_specs=[pl.BlockSpec((nq, 1, tm, LANE), lambda b, i: (0, b, i, 0)),
                   pl.BlockSpec((1, tm, d // LANE, LANE), lambda b, i: (b, i, 0, 0))],
        out_shape=[jax.ShapeDtypeStruct((nq, bx, t, LANE), BF16),
                   jax.ShapeDtypeStruct((bx, t, d // LANE, LANE), F32)],
        compiler_params=_cparams(("arbitrary", "arbitrary")),
    )(x, sc, sh, g.reshape(1, d), w)


def _compress_rows(rows, a, phi):
    n = rows.shape[0] // CMP_BLOCK
    w = jnp.sum(rows.reshape(n, CMP_BLOCK, rows.shape[1]) * a[None], axis=1)
    return jnp.dot(w, phi, preferred_element_type=F32, precision=lax.Precision.HIGHEST)


def _compress_body(x_ref, a_ref, phi_ref, o_ref):
    o_ref[0] = _compress_rows(x_ref[0], a_ref[...], phi_ref[...])


def _compress(kv4, a, phi, tm):
    b, s, _ = kv4.shape
    return pl.pallas_call(
        _compress_body,
        grid=(b, s // tm),
        in_specs=[pl.BlockSpec((1, tm, 256), lambda bi, i: (bi, i, 0)),
                  pl.BlockSpec((CMP_BLOCK, 256), lambda bi, i: (0, 0)),
                  pl.BlockSpec((256, 256), lambda bi, i: (0, 0))],
        out_specs=pl.BlockSpec((1, tm // CMP_BLOCK, 256), lambda bi, i: (bi, i, 0)),
        out_shape=jax.ShapeDtypeStruct((b, s // CMP_BLOCK, 256), F32),
        compiler_params=_cparams(("arbitrary", "arbitrary")),
    )(kv4, a, phi)


PAGES_PER_STEP = 8


def _wsum_paged_body(pt_ref, *refs):
    page_refs, a_ref, o_ref = refs[:PAGES_PER_STEP], refs[PAGES_PER_STEP], refs[PAGES_PER_STEP + 1]
    a = a_ref[...]
    page = a.shape[1]
    per = page // CMP_BLOCK
    nrow = PAGES_PER_STEP * per
    rowi = lax.broadcasted_iota(I32, (nrow, page), 0)
    posb = lax.broadcasted_iota(I32, (nrow, page), 1) // CMP_BLOCK
    acc = jnp.zeros((nrow, a.shape[0]), F32)
    for p, p_ref in enumerate(page_refs):
        z = p_ref[0, 0].reshape(a.shape) * a
        z_hi = z.astype(BF16)
        z_lo = (z - z_hi.astype(F32)).astype(BF16)
        member = jnp.where(rowi == p * per + posb, 1.0, 0.0).astype(BF16)
        acc = acc + _dot_nt(member, z_hi) + _dot_nt(member, z_lo)
    o_ref[0] = acc


def _wsum_paged(cache_t, page_table, alpha):
    db, n_pages = page_table.shape
    page = cache_t.shape[5]
    per = page // CMP_BLOCK
    at = jnp.tile(jnp.transpose(alpha, (0, 2, 1)), (1, 1, per))
    a_rows = jnp.broadcast_to(at[:, None], (2, N_KV, HEAD_DIM, page)).reshape(2 * N_KV * HEAD_DIM, page)

    def page_spec(r):
        return pl.BlockSpec((1, 1, 2, N_KV, HEAD_DIM, page),
                            lambda b, i, pt: (0, pt[b, PAGES_PER_STEP * i + r], 0, 0, 0, 0))

    grid_spec = pltpu.PrefetchScalarGridSpec(
        num_scalar_prefetch=1,
        grid=(db, n_pages // PAGES_PER_STEP),
        in_specs=[page_spec(r) for r in range(PAGES_PER_STEP)]
        + [pl.BlockSpec(a_rows.shape, lambda b, i, pt: (0, 0))],
        out_specs=pl.BlockSpec((1, PAGES_PER_STEP * per, a_rows.shape[0]), lambda b, i, pt: (b, i, 0)),
    )
    return pl.pallas_call(
        _wsum_paged_body,
        grid_spec=grid_spec,
        out_shape=jax.ShapeDtypeStruct((db, n_pages * per, a_rows.shape[0]), F32),
        compiler_params=_cparams(("arbitrary", "arbitrary")),
    )(page_table, *([cache_t] * PAGES_PER_STEP), a_rows)


def _select_blocks(score, n_pick):
    lane = lax.broadcasted_iota(I32, score.shape, 1)
    nl = score.shape[1]
    sel = jnp.zeros(score.shape, jnp.bool_)
    idxs = []
    for _ in range(n_pick):
        mx = jnp.max(score, axis=1, keepdims=True)
        idx = jnp.min(jnp.where(score == mx, lane, nl), axis=1, keepdims=True)
        hit = lane == idx
        sel = sel | hit
        score = jnp.where(hit, REMOVED, score)
        idxs.append(idx)
    return sel, idxs


def _flash_update(slot, qt, kc, vt, bias, mask, acc_sc, m_sc):
    s = jnp.dot(kc, qt, preferred_element_type=F32)
    if bias is not None:
        s = s + bias
    if mask is not None:
        s = jnp.where(mask, s, NEG)
    m_old = m_sc[slot]
    m_new = jnp.maximum(m_old, jnp.max(s, axis=0, keepdims=True))
    alpha = jnp.exp2(m_old - m_new)
    p = jnp.exp2(s - m_new)
    acc_sc[slot] = alpha * acc_sc[slot] + jnp.dot(vt, p.astype(BF16), preferred_element_type=F32)
    m_sc[slot] = m_new


def _select_rows(score, n_pick):
    row = lax.broadcasted_iota(I32, score.shape, 0)
    n = score.shape[0]
    sel = jnp.zeros(score.shape, jnp.bool_)
    for _ in range(n_pick):
        mx = jnp.max(score, axis=0, keepdims=True)
        idx = jnp.min(jnp.where(score == mx, row, n), axis=0, keepdims=True)
        hit = row == idx
        sel = sel | hit
        score = jnp.where(hit, REMOVED, score)
    return sel


def _nsa_body(rb_ref, q_ref, kk_ref, vt_ref, kc_ref, g_ref, o_ref,
              bdiag_sc, bcmp_sc, acc_sc, m_sc, osum_sc, *, seq, tq):
    i = pl.program_id(1)
    q0 = i * tq
    nb = seq // SEL_BLOCK
    per = tq // SEL_BLOCK
    ncopy = bcmp_sc.shape[2]

    @pl.when((pl.program_id(0) == 0) & (i == 0))
    def _build_bias_tiles():
        k = lax.broadcasted_iota(I32, (tq, tq), 0)
        t = lax.broadcasted_iota(I32, (tq, tq), 1)
        rr = lax.broadcasted_iota(I32, (2 * nb, tq), 0) - nb
        tc = lax.broadcasted_iota(I32, (2 * nb, tq), 1)
        for hd in range(N_HEADS):
            rb = lambda j, hd=hd: rb_ref[j, hd]
            kvh, cols = hd // GQA, slice((hd % GQA) * tq, (hd % GQA + 1) * tq)
            for dl in range(2):
                bdiag_sc[kvh, dl, :, cols] = LOG2E * (_bias_from_dist(t - k + dl * tq, rb) - rb(N_BUCKETS - 1))
            for par in range(2):
                for cp in range(ncopy):
                    bcmp_sc[kvh, par, cp, :, cols] = LOG2E * _bias_from_dist(
                        tc - (SEL_BLOCK * (rr + cp * per) + (par + 1) * CMP_BLOCK - 1), rb)

    gw = GQA * tq
    kc2 = kc_ref[0]
    kce, kco = kc2[:, 0:128].astype(BF16), kc2[:, 256:384].astype(BF16)
    vcet, vcot = kc2[:, 128:256].T.astype(BF16), kc2[:, 384:512].T.astype(BF16)
    qpos = q0 + lax.broadcasted_iota(I32, (nb, tq), 1)
    jl = lax.broadcasted_iota(I32, (nb, tq), 0)
    cur = qpos // SEL_BLOCK
    forced = (jl == 0) | (jl == cur) | (jl == cur - 1)
    causal_blk = jl * SEL_BLOCK <= qpos
    qpos_g = q0 + lax.broadcasted_iota(I32, (nb, gw), 1) % tq
    jl_g = lax.broadcasted_iota(I32, (nb, gw), 0)
    valid_e = qpos_g >= SEL_BLOCK * jl_g + CMP_BLOCK - 1
    valid_o = qpos_g >= SEL_BLOCK * jl_g + 2 * CMP_BLOCK - 1
    cmp_copy = (ncopy - i % ncopy) % ncopy
    cmp_start = pl.multiple_of(nb - (i + cmp_copy) * per, 8)
    kk = lax.broadcasted_iota(I32, (tq, gw), 0)
    tt = lax.broadcasted_iota(I32, (tq, gw), 1) % tq
    causal = tt >= kk
    gates = g_ref[0]
    ek = lax.broadcasted_iota(I32, (tq, nb), 0) // SEL_BLOCK
    ej = lax.broadcasted_iota(I32, (tq, nb), 1)

    def reset_state():
        m_sc[...] = jnp.full(m_sc.shape, NEG, F32)
        acc_sc[...] = jnp.zeros(acc_sc.shape, F32)

    def qgroup(kvh):
        return jnp.concatenate([q_ref[0, hd * LANE:(hd + 1) * LANE, :]
                                for hd in range(kvh * GQA, (kvh + 1) * GQA)], axis=1)

    vrow = lax.broadcasted_iota(I32, (LANE, tq), 0) // HEAD_DIM
    vrow2 = lax.broadcasted_iota(I32, (LANE, 2 * tq), 0) // HEAD_DIM
    ek2 = lax.broadcasted_iota(I32, (2 * tq, nb), 0) // SEL_BLOCK
    ej2 = lax.broadcasted_iota(I32, (2 * tq, nb), 1)

    def kv_chunk(c, branch):
        start = pl.multiple_of(c * tq, tq)
        return (kk_ref[0, pl.ds(start, tq), branch * LANE:(branch + 1) * LANE],
                vt_ref[0, c, branch * LANE:(branch + 1) * LANE, :])

    def both(qs, ka, v, rows, diag, mask):
        for kvh in range(N_KV):
            vt = jnp.where(rows == kvh, v, jnp.ones_like(v))
            bias = None if diag is None else bdiag_sc[kvh, diag]
            _flash_update(kvh, qs[kvh], ka, vt, bias, mask, acc_sc, m_sc)

    def finish(kvh, branch):
        lrow = (1 - kvh) * HEAD_DIM
        acc = acc_sc[kvh]
        o = acc / acc[lrow:lrow + 1, :]
        for g in range(GQA):
            hd = kvh * GQA + g
            r = branch * N_HEADS + hd
            osum_sc[hd] += gates[r:r + 1, :] * o[:, g * tq:(g + 1) * tq]

    scores = []
    for kvh in range(N_KV):
        qt = qgroup(kvh)
        s_e = jnp.dot(kce, qt, preferred_element_type=F32) + bcmp_sc[kvh, 0, cmp_copy, pl.ds(cmp_start, nb), :]
        s_o = jnp.dot(kco, qt, preferred_element_type=F32) + bcmp_sc[kvh, 1, cmp_copy, pl.ds(cmp_start, nb), :]
        s_e = jnp.where(valid_e, s_e, NEG)
        s_o = jnp.where(valid_o, s_o, NEG)
        mx = jnp.maximum(jnp.max(s_e, axis=0, keepdims=True), jnp.max(s_o, axis=0, keepdims=True))
        e_e = jnp.where(valid_e, jnp.exp2(s_e - mx), 0.0)
        e_o = jnp.where(valid_o, jnp.exp2(s_o - mx), 0.0)
        den = jnp.maximum(jnp.sum(e_e, axis=0, keepdims=True) + jnp.sum(e_o, axis=0, keepdims=True), 1e-30)
        p_e, p_o = e_e / den, e_o / den
        pe = p_e + p_o
        imp = (pe[:, 0:tq] + pe[:, tq:2 * tq]) + (pe[:, 2 * tq:3 * tq] + pe[:, 3 * tq:4 * tq])
        o_c = (jnp.dot(vcet, p_e.astype(BF16), preferred_element_type=F32)
               + jnp.dot(vcot, p_o.astype(BF16), preferred_element_type=F32))
        for g in range(GQA):
            hd = kvh * GQA + g
            osum_sc[hd] = gates[hd:hd + 1, :] * o_c[:, g * tq:(g + 1) * tq]
        score = jnp.where(forced, BIG, imp)
        scores.append(jnp.where(causal_blk, score, NEG))

    sel_all = _select_rows(jnp.concatenate(scores, axis=1), min(N_SEL, nb))

    qts = [qgroup(kvh) for kvh in range(N_KV)]
    qaugs = []
    for kvh in range(N_KV):
        pen = jnp.where(sel_all[:, kvh * tq:(kvh + 1) * tq], 0.0, -SEL_PENALTY).astype(BF16)
        qaugs.append(jnp.concatenate([qts[kvh], jnp.concatenate([pen] * GQA, axis=1)], axis=0))

    def kaug(c, kc):
        expand = jnp.where(ej == ek + c * per, 1.0, 0.0).astype(BF16)
        return jnp.concatenate([kc, expand], axis=1)

    reset_state()

    def far_pair(c2, carry):
        c = 2 * c2
        start = pl.multiple_of(c * tq, 2 * tq)
        kc = kk_ref[0, pl.ds(start, 2 * tq), 0:LANE]
        v = jnp.concatenate([vt_ref[0, c, 0:LANE, :], vt_ref[0, c + 1, 0:LANE, :]], axis=1)
        expand = jnp.where(ej2 == ek2 + c * per, 1.0, 0.0).astype(BF16)
        both(qaugs, jnp.concatenate([kc, expand], axis=1), v, vrow2, None, None)
        return carry

    n_far = jnp.maximum(i - 1, 0)
    lax.fori_loop(0, n_far // 2, far_pair, 0)

    @pl.when(n_far % 2 == 1)
    def _far_single():
        kc, v = kv_chunk(n_far - 1, 0)
        both(qaugs, kaug(n_far - 1, kc), v, vrow, None, None)

    @pl.when(i >= 1)
    def _prev_chunk():
        kc, v = kv_chunk(i - 1, 0)
        both(qaugs, kaug(i - 1, kc), v, vrow, 1, None)

    kc, v = kv_chunk(i, 0)
    both(qaugs, kaug(i, kc), v, vrow, 0, causal)
    for kvh in range(N_KV):
        finish(kvh, 1)

    reset_state()

    @pl.when(i >= 2)
    def _win_far():
        kc, v = kv_chunk(i - 2, 1)
        both(qts, kc, v, vrow, None, kk > tt)

    @pl.when(i >= 1)
    def _win_prev():
        kc, v = kv_chunk(i - 1, 1)
        both(qts, kc, v, vrow, 1, None)

    kc, v = kv_chunk(i, 1)
    both(qts, kc, v, vrow, 0, causal)
    for kvh in range(N_KV):
        finish(kvh, 2)

    for hd in range(N_HEADS):
        o_ref[0, :, hd * LANE:(hd + 1) * LANE] = osum_sc[hd].T.astype(BF16)


def _nsa_prompt(qt, kk, vt, kcvc2, gates_t, rel_bias, tq):
    b, _, s = qt.shape
    nb = s // SEL_BLOCK
    assert WINDOW == 2 * tq and tq + 1 >= FAR_DIST and s % tq == 0
    body = functools.partial(_nsa_body, seq=s, tq=tq)
    return pl.pallas_call(
        body,
        grid=(b, s // tq),
        in_specs=[pl.BlockSpec(memory_space=pltpu.SMEM),
                  pl.BlockSpec((1, N_HEADS * LANE, tq), lambda bi, i: (bi, 0, i)),
                  pl.BlockSpec((1, s, 2 * LANE), lambda bi, i: (bi, 0, 0)),
                  pl.BlockSpec((1, s // tq, 2 * LANE, tq), lambda bi, i: (bi, 0, 0, 0)),
                  pl.BlockSpec((1, nb, 512), lambda bi, i: (bi, 0, 0)),
                  pl.BlockSpec((1, LANE, tq), lambda bi, i: (bi, 0, i))],
        out_specs=pl.BlockSpec((1, tq, N_HEADS * LANE), lambda bi, i: (bi, i, 0)),
        out_shape=jax.ShapeDtypeStruct((b, s, N_HEADS * LANE), BF16),
        scratch_shapes=[pltpu.VMEM((N_KV, 2, tq, GQA * tq), F32),
                        pltpu.VMEM((N_KV, 2, max(1, 8 // (tq // SEL_BLOCK)), 2 * nb, GQA * tq), F32),
                        pltpu.VMEM((N_KV, LANE, GQA * tq), F32),
                        pltpu.VMEM((N_KV, 1, GQA * tq), F32),
                        pltpu.VMEM((N_HEADS, LANE, tq), F32)],
        compiler_params=_cparams(("arbitrary", "arbitrary")),
    )(rel_bias, qt, kk, vt, kcvc2, gates_t)


def _head_rows(x):
    sub = lax.broadcasted_iota(I32, (N_HEADS, LANE), 0)
    lane = lax.broadcasted_iota(I32, (N_HEADS, LANE), 1)
    return jnp.sum(jnp.where(sub == lane, jnp.broadcast_to(x, (N_HEADS, LANE)), 0.0), axis=1, keepdims=True)


def _nsa_s1_body(q_ref, kc_ref, cw_ref, wn_ref, g_ref, rbt_ref, phik_ref, phiv_ref, ocw_ref, idx_ref, *, past):
    nb = kc_ref.shape[1]
    wbuf = cw_ref.shape[1]
    q = q_ref[0]
    rbt = rbt_ref[...]
    rb = lambda j: rbt[:, j:j + 1]
    gates = g_ref[0]
    g_c = _head_rows(gates[:, 0:LANE])
    g_w = _head_rows(pltpu.roll(gates, LANE - 2 * N_HEADS, 1))

    ws = kc_ref[0]
    hi_dot = functools.partial(jnp.dot, preferred_element_type=F32, precision=lax.Precision.HIGHEST)
    phik, phiv = phik_ref[...], phiv_ref[...]
    kc2 = jnp.concatenate([hi_dot(ws[:, 0:128], phik), hi_dot(ws[:, 128:256], phiv),
                           hi_dot(ws[:, 256:384], phik), hi_dot(ws[:, 384:512], phiv)], axis=1)
    jl = lax.broadcasted_iota(I32, (N_HEADS, nb), 1)
    d_e = past - (SEL_BLOCK * jl + CMP_BLOCK - 1)
    d_o = past - (SEL_BLOCK * jl + 2 * CMP_BLOCK - 1)
    s_e = _dot_nt(q, kc2[:, 0:128].astype(BF16)) + _bias_from_dist(d_e, rb)
    s_o = _dot_nt(q, kc2[:, 256:384].astype(BF16)) + _bias_from_dist(d_o, rb)
    valid_e, valid_o = d_e >= 0, d_o >= 0
    s_e = jnp.where(valid_e, s_e, NEG)
    s_o = jnp.where(valid_o, s_o, NEG)
    mx = jnp.maximum(jnp.max(s_e, axis=1, keepdims=True), jnp.max(s_o, axis=1, keepdims=True))
    e_e = jnp.where(valid_e, jnp.exp(s_e - mx), 0.0)
    e_o = jnp.where(valid_o, jnp.exp(s_o - mx), 0.0)
    den = jnp.maximum(jnp.sum(e_e, axis=1, keepdims=True) + jnp.sum(e_o, axis=1, keepdims=True), 1e-30)
    p_e, p_o = e_e / den, e_o / den
    o_c = (jnp.dot(p_e.astype(BF16), kc2[:, 128:256].astype(BF16), preferred_element_type=F32)
           + jnp.dot(p_o.astype(BF16), kc2[:, 384:512].astype(BF16), preferred_element_type=F32))

    pe = p_e + p_o
    imp = jnp.concatenate([jnp.sum(pe[0:GQA], axis=0, keepdims=True),
                           jnp.sum(pe[GQA:2 * GQA], axis=0, keepdims=True)], axis=0)
    jl2 = lax.broadcasted_iota(I32, (N_KV, nb), 1)
    cur = past // SEL_BLOCK
    forced = (jl2 == 0) | (jl2 == cur) | (jl2 == cur - 1)
    score = jnp.where(forced, BIG, imp)
    score = jnp.where(jl2 * SEL_BLOCK <= past, score, NEG)
    _, idxs = _select_blocks(score, N_SEL - 1)
    lane = lax.broadcasted_iota(I32, (N_KV, LANE), 1)
    out_idx = jnp.zeros((N_KV, LANE), I32)
    for r, ix in enumerate(idxs):
        out_idx = jnp.where(lane == r, ix, out_idx)
    idx_ref[0] = jnp.concatenate([out_idx, jnp.zeros((8 - N_KV, LANE), I32)], axis=0)

    cw = cw_ref[0]
    il = lax.broadcasted_iota(I32, (N_HEADS, wbuf), 1)
    dq = wbuf - il
    s_w = _dot_nt(q, cw[:, 0:128].astype(BF16)) + _bias_from_dist(dq, rb)
    valid_w = dq < WINDOW
    s_w = jnp.where(valid_w, s_w, NEG)
    wn = wn_ref[0]
    qf = q.astype(F32)
    s_n = jnp.sum(qf * wn[:, 0:128].astype(BF16).astype(F32), axis=1, keepdims=True) + rb(0)
    mw = jnp.maximum(jnp.max(s_w, axis=1, keepdims=True), s_n)
    e_w = jnp.where(valid_w, jnp.exp(s_w - mw), 0.0)
    e_n = jnp.exp(s_n - mw)
    den_w = jnp.sum(e_w, axis=1, keepdims=True) + e_n
    o_w = (jnp.dot(e_w.astype(BF16), cw[:, 128:256].astype(BF16), preferred_element_type=F32)
           + e_n * wn[:, 128:256]) / den_w
    ocw_ref[0] = g_c * o_c + g_w * o_w


def _nsa_s1(q8, wsum2, cache_win3, winn, gates, rbt, phik, phiv, past):
    db = q8.shape[0]
    nb = wsum2.shape[1]
    wbuf = cache_win3.shape[1]
    body = functools.partial(_nsa_s1_body, past=past)
    return pl.pallas_call(
        body,
        grid=(db,),
        in_specs=[pl.BlockSpec((1, N_HEADS, LANE), lambda b: (b, 0, 0)),
                  pl.BlockSpec((1, nb, 512), lambda b: (b, 0, 0)),
                  pl.BlockSpec((1, wbuf, 256), lambda b: (b, 0, 0)),
                  pl.BlockSpec((1, 1, 256), lambda b: (b, 0, 0)),
                  pl.BlockSpec((1, 1, LANE), lambda b: (b, 0, 0)),
                  pl.BlockSpec((N_HEADS, N_BUCKETS), lambda b: (0, 0)),
                  pl.BlockSpec((LANE, LANE), lambda b: (0, 0)),
                  pl.BlockSpec((LANE, LANE), lambda b: (0, 0))],
        out_specs=[pl.BlockSpec((1, N_HEADS, LANE), lambda b: (b, 0, 0)),
                   pl.BlockSpec((1, 8, LANE), lambda b: (b, 0, 0))],
        out_shape=[jax.ShapeDtypeStruct((db, N_HEADS, LANE), F32),
                   jax.ShapeDtypeStruct((db, 8, LANE), I32)],
        compiler_params=_cparams(("arbitrary",)),
    )(q8, wsum2, cache_win3, winn, gates, rbt, phik, phiv)


def _nsa_s2_body(sel_ref, pt_ref, q_ref, blk0_ref, blk1_ref, kvn_ref, g_ref, rbt_ref, ocw_ref, o_ref,
                 acc_sc, m_sc, l_sc, *, past, npick):
    b, k = pl.program_id(0), pl.program_id(1)
    q = q_ref[0]
    rbt = rbt_ref[...]
    rb = lambda j: rbt[:, j:j + 1]
    kvn = kvn_ref[0]
    top = lax.broadcasted_iota(I32, (N_HEADS, 1), 0) < GQA

    @pl.when(k == 0)
    def _init():
        s_n = jnp.sum(q.astype(F32) * kvn[:, 256:384].astype(BF16).astype(F32), axis=1, keepdims=True) + rb(0)
        m_sc[...] = s_n
        l_sc[...] = jnp.ones(l_sc.shape, F32)
        acc_sc[...] = jnp.where(top, kvn[:, 384:384 + HEAD_DIM], kvn[:, 384 + HEAD_DIM:512])

    j0 = sel_ref[(b * N_KV + 0) * npick + k]
    j1 = sel_ref[(b * N_KV + 1) * npick + k]
    kt0 = blk0_ref[0, 0, 0, 0].astype(BF16)
    vt0 = blk0_ref[0, 0, 1, 0].astype(BF16)
    kt1 = blk1_ref[0, 0, 0, 0].astype(BF16)
    vt1 = blk1_ref[0, 0, 1, 0].astype(BF16)
    page = kt0.shape[1]
    bpp = page // SEL_BLOCK
    s = jnp.concatenate([jnp.dot(q[0:GQA, 0:HEAD_DIM], kt0, preferred_element_type=F32),
                         jnp.dot(q[GQA:, HEAD_DIM:], kt1, preferred_element_type=F32)], axis=0)
    pos = lax.broadcasted_iota(I32, (N_HEADS, page), 1)
    jrow = jnp.where(top, j0, j1)
    d = past - ((jrow // bpp) * page + pos)
    s = s + _bias_from_dist(d, rb)
    s = jnp.where((pos // SEL_BLOCK == jrow % bpp) & (d >= 0), s, NEG)
    m_old = m_sc[...]
    m_new = jnp.maximum(m_old, jnp.max(s, axis=1, keepdims=True))
    alpha = jnp.exp(m_old - m_new)
    p = jnp.exp(s - m_new).astype(BF16)
    pv = jnp.concatenate([_dot_nt(p[0:GQA], vt0), _dot_nt(p[GQA:], vt1)], axis=0)
    l_sc[...] = alpha * l_sc[...] + jnp.sum(p.astype(F32), axis=1, keepdims=True)
    acc_sc[...] = alpha * acc_sc[...] + pv
    m_sc[...] = m_new

    @pl.when(k == npick - 1)
    def _fin():
        g_s = _head_rows(pltpu.roll(g_ref[0], LANE - N_HEADS, 1))
        o = acc_sc[...] / l_sc[...]
        o_ref[0] = ocw_ref[0] + g_s * jnp.concatenate([o, o], axis=1)


def _nsa_s2(sel_flat, page_table, q8, cache_kv_l, kv4n, gates, rbt, ocw, past):
    db = q8.shape[0]
    npick = N_SEL - 1
    page = cache_kv_l.shape[5]
    bpp = page // SEL_BLOCK
    body = functools.partial(_nsa_s2_body, past=past, npick=npick)

    def blk_spec(h):
        def blk_map(b, k, sel, pt):
            j = sel[(b * N_KV + h) * npick + k]
            return (0, pt[b, j // bpp], 1, h, 0, 0)
        return pl.BlockSpec((1, 1, 2, 1, HEAD_DIM, page), blk_map)

    grid_spec = pltpu.PrefetchScalarGridSpec(
        num_scalar_prefetch=2,
        grid=(db, npick),
        in_specs=[pl.BlockSpec((1, N_HEADS, LANE), lambda b, k, sel, pt: (b, 0, 0)),
                  blk_spec(0), blk_spec(1),
                  pl.BlockSpec((1, 1, 512), lambda b, k, sel, pt: (b, 0, 0)),
                  pl.BlockSpec((1, 1, LANE), lambda b, k, sel, pt: (b, 0, 0)),
                  pl.BlockSpec((N_HEADS, N_BUCKETS), lambda b, k, sel, pt: (0, 0)),
                  pl.BlockSpec((1, N_HEADS, LANE), lambda b, k, sel, pt: (b, 0, 0))],
        out_specs=pl.BlockSpec((1, N_HEADS, LANE), lambda b, k, sel, pt: (b, 0, 0)),
        scratch_shapes=[pltpu.VMEM((N_HEADS, HEAD_DIM), F32),
                        pltpu.VMEM((N_HEADS, 1), F32),
                        pltpu.VMEM((N_HEADS, 1), F32)],
    )
    return pl.pallas_call(
        body,
        grid_spec=grid_spec,
        out_shape=jax.ShapeDtypeStruct((db, N_HEADS, LANE), F32),
        compiler_params=_cparams(("arbitrary", "arbitrary")),
    )(sel_flat, page_table, q8, cache_kv_l, cache_kv_l, kv4n, gates, rbt, ocw)


def _pool_body(x_ref, halo_ref, prev_ref, w_ref, sc_ref, o_ref, *, tm, pos_base):
    i = pl.program_id(1)
    halo = jnp.where(i == 0, prev_ref[0], halo_ref[0])
    cur = x_ref[0]
    ext = jnp.concatenate([halo, cur], axis=0)
    sums = {1: ext}
    w = 1
    while w < max(POOL_WINDOWS):
        sums[2 * w] = sums[w] + pltpu.roll(sums[w], w, 0)
        w *= 2
    pos = pos_base + i * tm + lax.broadcasted_iota(I32, (tm, 1), 0)
    outs = []
    for gi, wl in enumerate(POOL_WINDOWS):
        ch = slice(gi * POOL_GROUP, (gi + 1) * POOL_GROUP)
        tot = sums[wl][16:, ch]
        cnt = jnp.minimum(wl, pos + 1).astype(F32)
        d = tot / cnt - cur[:, ch]
        outs.append(jnp.dot(d.astype(BF16), w_ref[gi].astype(BF16), preferred_element_type=F32))
    o_ref[0] = jnp.concatenate(outs, axis=1) * sc_ref[...]


def _pool(p, prev16, pool_w, pool_scale, tm, pos_base):
    b, t, c = p.shape
    body = functools.partial(_pool_body, tm=tm, pos_base=pos_base)
    hb = tm // 16
    return pl.pallas_call(
        body,
        grid=(b, t // tm),
        in_specs=[pl.BlockSpec((1, tm, c), lambda bi, i: (bi, i, 0)),
                  pl.BlockSpec((1, 16, c), lambda bi, i: (bi, jnp.maximum(i * hb - 1, 0), 0)),
                  pl.BlockSpec((1, 16, c), lambda bi, i: (bi, 0, 0)),
                  pl.BlockSpec((len(POOL_WINDOWS), POOL_GROUP, POOL_GROUP), lambda bi, i: (0, 0, 0)),
                  pl.BlockSpec((1, c), lambda bi, i: (0, 0))],
        out_specs=pl.BlockSpec((1, tm, c), lambda bi, i: (bi, i, 0)),
        out_shape=jax.ShapeDtypeStruct((b, t, c), F32),
        compiler_params=_cparams(("arbitrary", "arbitrary")),
    )(p, p, prev16, pool_w, pool_scale.reshape(1, c))


def _rms(x, g):
    return x * lax.rsqrt(jnp.mean(x * x, axis=-1, keepdims=True) + EPS) * g


def _outproj_body(oa_ref, op_ref, x_ref, gate_ref, g_ref, wa_ref, wp_ref, o_ref):
    mix = (jnp.dot(oa_ref[0].astype(BF16), wa_ref[...], preferred_element_type=F32)
           + jnp.dot(op_ref[0].astype(BF16), wp_ref[...], preferred_element_type=F32))
    o_ref[0] = x_ref[0] + gate_ref[0] * _rms(mix, g_ref[...])


def _outproj(oa, op, x, gate, g, wa, wp, tm):
    bx, t, d = x.shape
    per_row = gate.shape[1] != 1
    ms = _mod_specs(t, tm, d, per_row)
    return pl.pallas_call(
        _outproj_body,
        grid=(bx, t // tm),
        in_specs=[pl.BlockSpec((1, tm, oa.shape[2]), lambda b, i: (b, i, 0)),
                  pl.BlockSpec((1, tm, op.shape[2]), lambda b, i: (b, i, 0)),
                  pl.BlockSpec((1, tm, d), lambda b, i: (b, i, 0)), ms,
                  pl.BlockSpec((1, d), lambda b, i: (0, 0)),
                  pl.BlockSpec(wa.shape, lambda b, i: (0, 0)),
                  pl.BlockSpec(wp.shape, lambda b, i: (0, 0))],
        out_specs=pl.BlockSpec((1, tm, d), lambda b, i: (b, i, 0)),
        out_shape=jax.ShapeDtypeStruct((bx, t, d), F32),
        compiler_params=_cparams(("arbitrary", "arbitrary")),
    )(oa, op, x, gate, g.reshape(1, d), wa, wp)


def _final_body(f_ref, x_ref, gate_ref, g_ref, o_ref):
    f = jnp.concatenate([f_ref[0, :, c, :] for c in range(f_ref.shape[2])], axis=1)
    o_ref[0] = x_ref[0] + gate_ref[0] * _rms(f, g_ref[...])


def _final(f, x, gate, g, tm):
    bx, t, d = x.shape
    per_row = gate.shape[1] != 1
    ms = _mod_specs(t, tm, d, per_row)
    xs = pl.BlockSpec((1, tm, d), lambda b, i: (b, i, 0))
    fs = pl.BlockSpec((1, tm, d // LANE, LANE), lambda b, i: (b, i, 0, 0))
    return pl.pallas_call(
        _final_body,
        grid=(bx, t // tm),
        in_specs=[fs, xs, ms, pl.BlockSpec((1, d), lambda b, i: (0, 0))],
        out_specs=xs,
        out_shape=jax.ShapeDtypeStruct((bx, t, d), F32),
        compiler_params=_cparams(("arbitrary", "arbitrary")),
    )(f, x, gate, g.reshape(1, d))


def _topk_rows(s, k, ids=None):
    row = lax.broadcasted_iota(I32, s.shape, 0) if ids is None else ids
    krow = lax.broadcasted_iota(I32, (k, s.shape[1]), 0)
    vals = jnp.zeros((k, s.shape[1]), F32)
    idxs = jnp.zeros((k, s.shape[1]), I32)
    for it in range(k):
        mx = jnp.max(s, axis=0, keepdims=True)
        ix = jnp.min(jnp.where(s == mx, row, jnp.int32(2 ** 30)), axis=0, keepdims=True)
        vals = jnp.where(krow == it, mx, vals)
        idxs = jnp.where(krow == it, ix, idxs)
        s = jnp.where(row == ix, REMOVED, s)
    return vals, idxs


def _pair_candidates(kk, t):
    r = lax.broadcasted_iota(I32, (kk + 7 * 8 + 8, t), 0)
    mid = (r - kk) // 8
    a = jnp.where(r < kk, 0, jnp.where(r < kk + 56, 1 + mid, r - (kk + 56) + 8))
    b = jnp.where(r < kk, r, jnp.where(r < kk + 56, r - kk - 8 * mid, 0))
    return a * kk + b, (a + 1) * (b + 1) <= kk


def _pick_rows(table, ix):
    k = table.shape[0]
    out = jnp.zeros(ix.shape, table.dtype)
    for a in range(k):
        out = jnp.where(ix == a, table[a:a + 1, :], out)
    return out


def _peer_topk_body(q_ref, keys_ref, idx_ref, gate_ref, e_sc):
    kk = PEER_TOPK
    assert kk == 16
    flat_id, reachable = _pair_candidates(kk, q_ref.shape[1])

    def head(h, carry):
        s1 = _dot_nt(keys_ref[2 * h], q_ref[2 * h])
        s2 = _dot_nt(keys_ref[2 * h + 1], q_ref[2 * h + 1])
        v1, i1 = _topk_rows(s1, kk)
        v2, i2 = _topk_rows(s2, kk)
        cand = jnp.concatenate([v1[0:1, :] + v2] + [v1[a:a + 1, :] + v2[0:8, :] for a in range(1, 8)]
                               + [v1[8:16, :] + v2[0:1, :]], axis=0)
        best, flat = _topk_rows(jnp.where(reachable, cand, REMOVED), kk, flat_id)
        e = _pick_rows(i1, flat // kk) * PEER_KEYS + _pick_rows(i2, flat % kk)
        ex = jnp.exp(best - jnp.max(best, axis=0, keepdims=True))
        gate = ex / jnp.sum(ex, axis=0, keepdims=True)
        off = pl.multiple_of(h * kk, kk)
        e_sc[pl.ds(off, kk), :] = e * HALF_ROWS + PAD_ROWS
        gate_ref[pl.ds(off, kk), :] = gate
        return carry

    def head_group(j, carry):
        for r in range(HEADS_PER_BODY):
            head(HEADS_PER_BODY * j + r, carry)
        return carry

    lax.fori_loop(0, PEER_HEADS // HEADS_PER_BODY, head_group, 0)
    idx_ref[...] = e_sc[...].T


def _peer_topk(q16, keys16, tt):
    nq, n, _ = q16.shape
    rows = PEER_HEADS * PEER_TOPK
    return pl.pallas_call(
        _peer_topk_body,
        grid=(n // tt,),
        in_specs=[pl.BlockSpec((nq, tt, LANE), lambda i: (0, i, 0)),
                  pl.BlockSpec(keys16.shape, lambda i: (0, 0, 0))],
        out_specs=[pl.BlockSpec((tt, rows), lambda i: (i, 0)),
                   pl.BlockSpec((rows, tt), lambda i: (0, i))],
        out_shape=[jax.ShapeDtypeStruct((n, rows), I32),
                   jax.ShapeDtypeStruct((rows, n), F32)],
        scratch_shapes=[pltpu.VMEM((rows, tt), I32)],
        compiler_params=_cparams(("arbitrary",)),
    )(q16, keys16)


PAD_ROWS = 4
HALF_ROWS = 4
TOKENS_PER_BODY = 8
HEADS_PER_BODY = 8
_SLOT_OF_ROW = (6, 2, 4, 0, 7, 3, 5, 1)


def _expert_slab(tab_ref, row, late):
    off = pl.multiple_of(row - (PAD_ROWS if late else 0), HALF_ROWS)
    w = tab_ref[pl.ds(off, 8), :]
    lo = pltpu.bitcast(w << 16, F32)
    hi = pltpu.bitcast(w & jnp.uint32(0xFFFF0000), F32)
    return lo, hi


def _peer_u_body(idx_ref, x_ref, gate_ref, tab_ref, act_ref, *, tt):
    rows = PEER_HEADS * PEER_TOPK
    ngrp = rows // 8
    lane = lax.broadcasted_iota(I32, (8, tt), 1)
    sub = lax.broadcasted_iota(I32, (8, LANE), 0)
    m1 = (sub & 2) != 0
    m2 = (sub & 1) != 0
    low = sub < HALF_ROWS

    def token(t, accs):
        x8 = x_ref[t]
        xr = pltpu.roll(x8, HALF_ROWS, 0)
        xa = (jnp.where(low, x8, 0.0), jnp.where(low, 0.0, xr))
        xb = (jnp.where(low, xr, 0.0), jnp.where(low, 0.0, x8))
        new = []
        for gidx in range(ngrp):
            prod = [None] * 8
            for r in range(8):
                slot = _SLOT_OF_ROW[r]
                late = slot % 2
                lo, hi = _expert_slab(tab_ref, idx_ref[t, gidx * 8 + r], late)
                prod[slot] = lo * xa[late] + hi * xb[late]
            c = [prod[2 * k] + prod[2 * k + 1] for k in range(4)]
            d = [c[k] + pltpu.roll(c[k], 2 if k % 2 == 0 else 6, 0) for k in range(4)]
            e0, e1 = jnp.where(m1, d[0], d[1]), jnp.where(m1, d[2], d[3])
            f0 = e0 + pltpu.roll(e0, 1, 0)
            f1 = e1 + pltpu.roll(e1, 7, 0)
            dcol = jnp.sum(jnp.where(m2, f0, f1), axis=1, keepdims=True)
            new.append(jnp.where(lane == t, dcol, accs[gidx]))
        return tuple(new)

    def token_group(j, accs):
        for r in range(TOKENS_PER_BODY):
            accs = token(TOKENS_PER_BODY * j + r, accs)
        return accs

    accs = lax.fori_loop(0, tt // TOKENS_PER_BODY, token_group,
                         tuple(jnp.zeros((8, tt), F32) for _ in range(ngrp)))
    d = jnp.concatenate(accs, axis=0)
    act_ref[...] = (jax.nn.gelu(d, approximate=True) * gate_ref[...]).T


def _peer_u(idx, x3, gate, tab, tt):
    n, rows = idx.shape
    body = functools.partial(_peer_u_body, tt=tt)
    return pl.pallas_call(
        body,
        grid=(n // tt,),
        in_specs=[pl.BlockSpec((tt, rows), lambda i: (i, 0), memory_space=pltpu.SMEM),
                  pl.BlockSpec((tt, 8, LANE), lambda i: (i, 0, 0)),
                  pl.BlockSpec((rows, tt), lambda i: (0, i)),
                  pl.BlockSpec(tab.shape, lambda i: (0, 0), pipeline_mode=pl.Buffered(1))],
        out_specs=pl.BlockSpec((tt, rows), lambda i: (i, 0)),
        out_shape=jax.ShapeDtypeStruct((n, rows), F32),
        compiler_params=_cparams(("arbitrary",)),
    )(idx, x3, gate, tab)


def _peer_v_body(idx_ref, act_ref, tab_ref, f_ref, *, tt):
    rows = PEER_HEADS * PEER_TOPK
    sub = lax.broadcasted_iota(I32, (8, LANE), 0)

    def token(t, carry):
        al = [jnp.zeros((8, LANE), F32) for _ in range(2)]
        ah = [jnp.zeros((8, LANE), F32) for _ in range(2)]
        for r in range(rows):
            lo, hi = _expert_slab(tab_ref, idx_ref[t, r], False)
            a = act_ref[t, r]
            al[r % 2] = al[r % 2] + a * lo
            ah[r % 2] = ah[r % 2] + a * hi
        f_ref[t] = jnp.where(sub < HALF_ROWS, al[0] + al[1], pltpu.roll(ah[0] + ah[1], HALF_ROWS, 0))
        return carry

    def token_group(j, carry):
        for r in range(TOKENS_PER_BODY):
            token(TOKENS_PER_BODY * j + r, carry)
        return carry

    lax.fori_loop(0, tt // TOKENS_PER_BODY, token_group, 0)


def _peer_v(idx, act, tab, tt):
    n, rows = idx.shape
    body = functools.partial(_peer_v_body, tt=tt)
    return pl.pallas_call(
        body,
        grid=(n // tt,),
        in_specs=[pl.BlockSpec((tt, rows), lambda i: (i, 0), memory_space=pltpu.SMEM),
                  pl.BlockSpec((tt, rows), lambda i: (i, 0), memory_space=pltpu.SMEM),
                  pl.BlockSpec(tab.shape, lambda i: (0, 0), pipeline_mode=pl.Buffered(1))],
        out_specs=pl.BlockSpec((tt, 8, LANE), lambda i: (i, 0, 0)),
        out_shape=jax.ShapeDtypeStruct((n, 8, LANE), F32),
        compiler_params=_cparams(("arbitrary",)),
    )(idx, act, tab)


def _pack_table(tab):
    n, d = tab.shape
    bits = lax.bitcast_convert_type(tab.astype(jnp.bfloat16), jnp.uint16).astype(U32)
    words = (bits[:, :d // 2] | (bits[:, d // 2:] << 16)).reshape(n * HALF_ROWS, LANE)
    pad = jnp.zeros((PAD_ROWS, LANE), U32)
    return jnp.concatenate([pad, words, pad], axis=0)


def _peer(x1, sc, sh, g, wq, keys16, tab_u, tab_v, tm):
    bx, t, d = x1.shape
    n = bx * t
    q16, h2 = _peerq(x1, sc, sh, g, wq, tm)
    tt = min(128, n)
    idx, gate = _peer_topk(q16.reshape(2 * PEER_HEADS, n, LANE), keys16, tt)
    act = _peer_u(idx, h2.reshape(n, d // LANE, LANE), gate, tab_u, tt)
    f = _peer_v(idx, act, tab_v, tt)
    return f.reshape(bx, t, d // LANE, LANE)


def _pack_w_in(w_in, q_scale):
    d = w_in.shape[0]
    scale = q_scale
    cols = []
    for hd in range(N_HEADS):
        blk = w_in[:, hd * HEAD_DIM:(hd + 1) * HEAD_DIM] * scale
        z = jnp.zeros((d, HEAD_DIM), w_in.dtype)
        cols += [blk, z] if hd // GQA == 0 else [z, blk]
    o0 = N_HEADS * HEAD_DIM
    o1 = o0 + 6 * N_KV * HEAD_DIM
    o2 = o1 + 3 * N_HEADS
    kvw = N_KV * HEAD_DIM
    wa = jnp.concatenate([w_in[:, o0:o1], w_in[:, o2:]], axis=1)
    cols.append(w_in[:, o0 + 3 * kvw:o0 + 4 * kvw])
    cols.append(w_in[:, o0 + 5 * kvw:o0 + 6 * kvw])
    cols.append(w_in[:, o1:o2])
    cols.append(jnp.zeros((d, LANE - 3 * N_HEADS), w_in.dtype))
    wbt = jnp.concatenate(cols, axis=1).T
    return wa.astype(BF16), wbt.astype(BF16)


def _pack_w_out(w_out):
    d = w_out.shape[1]
    rows = []
    for hd in range(N_HEADS):
        blk = w_out[hd * HEAD_DIM:(hd + 1) * HEAD_DIM]
        z = jnp.zeros((HEAD_DIM, d), w_out.dtype)
        rows += [blk, z] if hd // GQA == 0 else [z, blk]
    wa = jnp.concatenate(rows, axis=0).astype(BF16)
    wp = w_out[N_HEADS * HEAD_DIM:].astype(BF16)
    return wa, wp


def _pack_compress(alpha, phi):
    a = jnp.concatenate([alpha[0], alpha[0], alpha[1], alpha[1]], axis=1)
    z = jnp.zeros((HEAD_DIM, HEAD_DIM), phi.dtype)
    blocks = [phi[0], phi[0], phi[1], phi[1]]
    p = jnp.concatenate([jnp.concatenate([blocks[r] if c == r else z for c in range(4)], axis=1)
                         for r in range(4)], axis=0)
    return a, p


def kernel(x_prompt, x_sample, cache_kv, cache_win, state_pool, page_table, c_prompt, c_sample,
           w_ada, b_ada, norm_g, w_in, w_out, cmp_alpha, cmp_phi, rel_bias, pool_w, pool_scale,
           peer_wq, peer_keys, peer_u, peer_v):
    depth = w_ada.shape[0]
    assert depth == 1 and x_sample.shape[1] == 1
    l = 0
    b, s, d = x_prompt.shape
    db = x_sample.shape[0]
    n_pool, page = cache_kv.shape[1], cache_kv.shape[2]
    n_pages = page_table.shape[1]
    past = n_pages * page
    wbuf = cache_win.shape[2]
    tq = WINDOW // 2

    w_in_a, w_in_bt = _pack_w_in(w_in[l], HEAD_DIM ** -0.5 * LOG2E)
    _, w_in_bt_s = _pack_w_in(w_in[l], HEAD_DIM ** -0.5)
    wa, wp = _pack_w_out(w_out[l])
    a_cmp, phi_cmp = _pack_compress(cmp_alpha[l], cmp_phi[l])
    wq = peer_wq[l].astype(BF16)
    keys16 = peer_keys[l].reshape(2 * PEER_HEADS, PEER_KEYS, LANE).astype(BF16)
    tab_u = _pack_table(peer_u[l])
    tab_v = _pack_table(peer_v[l])
    g = norm_g[l]

    c_all = jnp.concatenate([c_prompt, c_sample], axis=0)
    mod = _mod(c_all, w_ada[l], b_ada[l]).reshape(b + db, 6, d)
    mod_p = [mod[:b, k][:, None, :] for k in range(6)]
    mod_s = [mod[b:, k][None, :, :] for k in range(6)]

    tm = min(512, s)
    kv4, kk, win, pool_in, q_t, gates_t, v_t = _inproj(x_prompt, mod_p[1], mod_p[0], g[0], w_in_a, w_in_bt, tm, tq)
    kcvc = _compress(kv4, a_cmp, phi_cmp, min(1024, s))
    o_attn = _nsa_prompt(q_t, kk, v_t, kcvc.reshape(b, s // SEL_BLOCK, 512), gates_t, rel_bias, tq)
    prev0 = jnp.zeros((b, 16, POOL_DIM), F32)
    o_pool = _pool(pool_in, prev0, pool_w[l], pool_scale[l], min(1024, s), 0)
    x1 = _outproj(o_attn, o_pool, x_prompt, mod_p[2], g[1], wa, wp, tm)
    f = _peer(x1, mod_p[4], mod_p[3], g[2], wq, keys16, tab_u, tab_v, tm)
    y_prompt = _final(f, x1, mod_p[5], g[3], tm)

    kv_prompt = kv4.reshape(1, b, s, 4, N_KV, HEAD_DIM)
    win_prompt = win[:, s - min(WINDOW, s):].reshape(1, b, min(WINDOW, s), 2, N_KV, HEAD_DIM)
    pool_prompt = pool_in[:, s - POOL_STATE:][None]

    xs = x_sample.reshape(1, db, d)
    kv4_s, _, win_s, pool_s, q_t_s, gates_t_s = _inproj(xs, mod_s[1], mod_s[0], g[0], w_in_a, w_in_bt_s, db, None)
    cache_l = jnp.transpose(cache_kv, (0, 1, 3, 4, 5, 2))
    wsum = _wsum_paged(cache_l, page_table, cmp_alpha[l])
    q8 = q_t_s[0].T.reshape(db, N_HEADS, LANE)
    kv4n = kv4_s.reshape(db, 1, 512)
    winn = win_s.reshape(db, 1, 256)
    gat = gates_t_s[0].T.reshape(db, 1, LANE)
    cw3 = cache_win[l].reshape(db, wbuf, 256)
    rbt = rel_bias.T
    ocw, sel_idx = _nsa_s1(q8, wsum.reshape(db, past // SEL_BLOCK, 512), cw3, winn, gat, rbt,
                           phi_cmp[:LANE, :LANE], phi_cmp[LANE:, LANE:], past)
    sel_flat = sel_idx[:, :N_KV, :N_SEL - 1].reshape(-1)
    o_attn_s = _nsa_s2(sel_flat, page_table, q8, cache_l, kv4n, gat, rbt, ocw, past)
    ext = jnp.concatenate([jnp.zeros((db, 1, POOL_DIM), F32), state_pool[l], pool_s.reshape(db, 1, POOL_DIM)], axis=1)
    o_pool_s = _pool(ext[:, 1:], jnp.zeros((db, 16, POOL_DIM), F32), pool_w[l], pool_scale[l], 16,
                     past - POOL_STATE)[:, POOL_STATE:]
    x1s = _outproj(o_attn_s.reshape(1, db, N_HEADS * LANE).astype(BF16), o_pool_s.reshape(1, db, POOL_DIM),
                   xs, mod_s[2], g[1], wa, wp, db)
    fs = _peer(x1s, mod_s[4], mod_s[3], g[2], wq, keys16, tab_u, tab_v, db)
    y_sample = _final(fs, x1s, mod_s[5], g[3], db).reshape(db, 1, d)

    kv_sample = kv4_s.reshape(1, db, 1, 4, N_KV, HEAD_DIM)
    win_sample = jnp.concatenate([cache_win[l][:, 1:], win_s.reshape(db, 1, 2, N_KV, HEAD_DIM)], axis=1)[None]
    pool_sample = ext[:, 2:][None]
    return (y_prompt, y_sample, kv_prompt, kv_sample, win_prompt, win_sample, pool_prompt, pool_sample)
```
